```python
import jax, jax.numpy as jnp
from jax import lax
import numpy as np

D_MODEL = 1024
BATCH = 4
SEQ = 4096
DEPTH = 2
DEC_BATCH = 32
DEC_SEQ = 8
PAST_LEN = 8192
PAGE_SIZE = 128

F32 = jnp.float32
NSA_HEADS = 8
NSA_GROUPS = 2
HEAD_DIM = 64
NSA_WIDTH = NSA_HEADS * HEAD_DIM
CMP_LEN = 32
CMP_STRIDE = 16
SEL_LEN = 64
N_SEL = 16
N_LOCAL = 2
WINDOW = 512
Q_BLOCK = 128
FORCE_BONUS = 1e4
HGRN_HEADS = 4
HGRN_DK = 128
HGRN_DV = 128
HGRN_WIDTH = HGRN_HEADS * HGRN_DV
HGRN_CHUNK = 16
MIX_WIDTH = NSA_WIDTH + HGRN_WIDTH
D_FF = 2816
N_EXPERTS = 8
TOP_K = 2
N_DENSE = (DEPTH + 1) // 2
N_MOE = DEPTH // 2
EPS = 1e-6
NEG_INF = -1e30

Q_END = NSA_WIDTH
KV_END = Q_END + 6 * NSA_GROUPS * HEAD_DIM
GATE_END = KV_END + 3 * NSA_HEADS
HQ_END = GATE_END + HGRN_HEADS * HGRN_DK
HF_END = HQ_END + HGRN_HEADS * HGRN_DK
HI_END = HF_END + HGRN_HEADS * HGRN_DV
IN_COLS = HI_END + HGRN_HEADS * HGRN_DV

kernel_name = 'nsa_hgrn2_hybrid_decode_step'


def rms_norm(x, gain):
    xf = x.astype(F32)
    y = xf * lax.rsqrt(jnp.mean(xf * xf, axis=-1, keepdims=True) + EPS)
    return (y * gain.astype(F32)).astype(x.dtype)


def alibi_slopes():
    return 2.0 ** (-8.0 * (jnp.arange(NSA_HEADS, dtype=F32) + 1.0) / NSA_HEADS)


def project(h, w_in, q_gain, k_gain):
    B, T, _ = h.shape
    z = jnp.einsum('btd,dc->btc', h, w_in)
    q = rms_norm(z[..., :Q_END].reshape(B, T, NSA_HEADS, HEAD_DIM), q_gain)
    kv = z[..., Q_END:KV_END].reshape(B, T, 3, 2, NSA_GROUPS, HEAD_DIM)
    rows = jnp.stack([kv[:, :, 0, 0], kv[:, :, 0, 1], rms_norm(kv[:, :, 1, 0], k_gain[1]), kv[:, :, 1, 1]], axis=2)
    win = jnp.stack([rms_norm(kv[:, :, 2, 0], k_gain[2]), kv[:, :, 2, 1]], axis=2)
    gates = jax.nn.sigmoid(z[..., KV_END:GATE_END].astype(F32)).reshape(B, T, 3, NSA_HEADS)
    hz = (z[..., GATE_END:HQ_END], z[..., HQ_END:HF_END], z[..., HF_END:HI_END], z[..., HI_END:IN_COLS])
    return q, rows, win, gates, hz


def compress(k_raw, v_raw, pe, cw, k_gain):
    B, T, G, D = k_raw.shape
    r_n = CMP_LEN // CMP_STRIDE
    n_chunk = T // CMP_STRIDE
    n_cmp = n_chunk - r_n + 1
    ck = k_raw[:, :n_chunk * CMP_STRIDE].reshape(B, n_chunk, CMP_STRIDE, G, D)
    cv = v_raw[:, :n_chunk * CMP_STRIDE].reshape(B, n_chunk, CMP_STRIDE, G, D)
    kc = 0.0
    vc = 0.0
    for r in range(r_n):
        sl = slice(r * CMP_STRIDE, (r + 1) * CMP_STRIDE)
        kc = kc + jnp.einsum('bnsgd,sde->bnge', ck[:, r:r + n_cmp] + pe[0, sl, None, :], cw[0, sl])
        vc = vc + jnp.einsum('bnsgd,sde->bnge', cv[:, r:r + n_cmp] + pe[1, sl, None, :], cw[1, sl])
    return rms_norm(kc, k_gain), vc


def sel_blocks(k):
    B, T, G, D = k.shape
    nb = -(-T // SEL_LEN)
    k = jnp.pad(k, ((0, 0), (0, nb * SEL_LEN - T), (0, 0), (0, 0)))
    return k.reshape(B, nb, SEL_LEN, G, D).transpose(0, 3, 1, 2, 4)


def nsa_attend(q, q_pos, kc, vc, ksb, vsb, kw, vw, kw_pos, gates, slopes):
    B, Q, H, D = q.shape
    G = kc.shape[2]
    hpg = H // G
    qf = q.astype(F32).reshape(B, Q, G, hpg, D) * (D ** -0.5)
    m = slopes.reshape(G, hpg)
    tq = q_pos.astype(F32)
    n_cmp = kc.shape[1]
    c_start = jnp.arange(n_cmp) * CMP_STRIDE
    c_ok = (c_start + CMP_LEN - 1)[None, :] <= q_pos[:, None]
    c_dist = jnp.abs(tq[:, None] - (c_start.astype(F32) + 0.5 * (CMP_LEN - 1))[None, :])
    s = jnp.einsum('bqghd,bngd->bqghn', qf, kc.astype(F32)) - m[None, None, :, :, None] * c_dist[None, :, None, None, :]
    s = jnp.where(c_ok[None, :, None, None, :], s, NEG_INF)
    p_cmp = jax.nn.softmax(s, axis=-1) * jnp.any(c_ok, axis=-1).astype(F32)[None, :, None, None, None]
    o_cmp = jnp.einsum('bqghn,bngd->bqghd', p_cmp, vc.astype(F32))
    n_blk = ksb.shape[2]
    r_s = SEL_LEN // CMP_STRIDE
    r_c = CMP_LEN // CMP_STRIDE
    n_tap = r_s + r_c - 1
    taps = jnp.asarray(np.convolve(np.ones(r_s), np.ones(r_c)) / r_c, F32)
    imp = jnp.pad(jnp.sum(p_cmp, axis=3), ((0, 0), (0, 0), (0, 0), (r_c - 1, r_s * n_blk - n_cmp)))
    tap_idx = (jnp.arange(n_blk) * r_s)[:, None] + jnp.arange(n_tap)[None, :]
    blk_imp = jnp.einsum('bqgjk,k->bqgj', imp[..., tap_idx], taps)
    back = (q_pos // SEL_LEN)[:, None] - jnp.arange(n_blk)[None, :]
    forced = ((jnp.arange(n_blk)[None, :] == 0) | ((back >= 0) & (back < N_LOCAL))).astype(F32)
    score = jnp.where((back >= 0)[None, :, None, :], blk_imp + FORCE_BONUS * forced[None, :, None, :], -1.0)
    n_pick = min(N_SEL, n_blk)
    _, sel = lax.top_k(score, n_pick)
    gather = jax.vmap(jax.vmap(lambda kb, ix: kb[ix]))
    sel_t = sel.transpose(0, 2, 1, 3)
    kg = gather(ksb, sel_t).astype(F32)
    vg = gather(vsb, sel_t).astype(F32)
    k_pos = sel[..., None] * SEL_LEN + jnp.arange(SEL_LEN)
    s_dist = q_pos[None, :, None, None, None] - k_pos
    s = jnp.einsum('bqghd,bgqpld->bqghpl', qf, kg) - m[None, None, :, :, None, None] * jnp.abs(s_dist).astype(F32)[:, :, :, None]
    s = jnp.where((s_dist >= 0)[:, :, :, None], s, NEG_INF).reshape(B, Q, G, hpg, n_pick * SEL_LEN)
    o_sel = jnp.einsum('bqghk,bgqkd->bqghd', jax.nn.softmax(s, axis=-1), vg.reshape(B, G, Q, n_pick * SEL_LEN, D))
    w_dist = q_pos[:, None] - kw_pos[None, :]
    w_ok = (w_dist >= 0) & (w_dist < WINDOW) & (kw_pos >= 0)[None, :]
    s = jnp.einsum('bqghd,bkgd->bqghk', qf, kw.astype(F32)) - m[None, None, :, :, None] * jnp.abs(w_dist).astype(F32)[None, :, None, None, :]
    s = jnp.where(w_ok[None, :, None, None, :], s, NEG_INF)
    o_win = jnp.einsum('bqghk,bkgd->bqghd', jax.nn.softmax(s, axis=-1), vw.astype(F32))
    g = gates.astype(F32).reshape(B, Q, 3, G, hpg)[..., None]
    o = g[:, :, 0] * o_cmp + g[:, :, 1] * o_sel + g[:, :, 2] * o_win
    return o.reshape(B, Q, H * D).astype(q.dtype)


def nsa_prompt(q, rows, win, gates, kc, vc, slopes):
    B, S = q.shape[:2]
    ksb = sel_blocks(rows[:, :, 2])
    vsb = sel_blocks(rows[:, :, 3])
    win_pad = jnp.pad(win, ((0, 0), (WINDOW, 0), (0, 0), (0, 0), (0, 0)))

    def block(i):
        start = i * Q_BLOCK
        wb = lax.dynamic_slice_in_dim(win_pad, start, Q_BLOCK + WINDOW, axis=1)
        return nsa_attend(lax.dynamic_slice_in_dim(q, start, Q_BLOCK, axis=1), start + jnp.arange(Q_BLOCK),
                          kc, vc, ksb, vsb, wb[:, :, 0], wb[:, :, 1], start - WINDOW + jnp.arange(Q_BLOCK + WINDOW),
                          lax.dynamic_slice_in_dim(gates, start, Q_BLOCK, axis=1), slopes)

    out = lax.map(block, jnp.arange(S // Q_BLOCK))
    return out.transpose(1, 0, 2, 3).reshape(B, S, NSA_WIDTH)


def hgrn2_chunked(q, log_f, k, v, s0):
    B, T, H, DK = q.shape
    C = HGRN_CHUNK
    n = -(-T // C)
    pad = n * C - T
    def chunks(a):
        a = jnp.pad(a, ((0, 0), (0, pad), (0, 0), (0, 0)))
        return a.reshape(B, n, C, H, a.shape[-1]).transpose(1, 0, 3, 2, 4)
    causal = jnp.tril(jnp.ones((C, C), bool))[None, None, :, :, None]

    def step(S, inp):
        qi, gi, ki, vi = inp
        b = jnp.cumsum(gi, axis=2)
        dec = jnp.exp(jnp.where(causal, b[:, :, :, None, :] - b[:, :, None, :, :], NEG_INF))
        A = jnp.einsum('bhtc,bhsc,bhtsc->bhts', qi, ki, dec)
        o = jnp.einsum('bhts,bhsv->bhtv', A, vi) + jnp.einsum('bhtc,bhcv->bhtv', qi * jnp.exp(b), S)
        b_last = b[:, :, -1:, :]
        S_new = S * jnp.exp(b_last[:, :, 0, :, None]) + jnp.einsum('bhsc,bhsv->bhcv', ki * jnp.exp(b_last - b), vi)
        return S_new, o

    S, o = lax.scan(step, s0, (chunks(q), chunks(log_f), chunks(k), chunks(v)))
    o = o.transpose(1, 0, 3, 2, 4).reshape(B, n * C, H, v.shape[-1])[:, :T]
    return o, S


def hgrn2_mix(hz, lb, o_gain, s0):
    zq, zf, zi, zg = hz
    B, T, _ = zq.shape
    shp = (B, T, HGRN_HEADS, HGRN_DK)
    q = jax.nn.silu(zq.astype(F32)).reshape(shp)
    zf = zf.astype(F32).reshape(shp)
    lb = lb.astype(F32).reshape(HGRN_HEADS, HGRN_DK)
    log_f = jnp.logaddexp(jnp.log(lb), jnp.log1p(-lb) + jax.nn.log_sigmoid(zf))
    k = (1.0 - lb) * jax.nn.sigmoid(-zf)
    v = zi.astype(F32).reshape(B, T, HGRN_HEADS, HGRN_DV)
    o, S = hgrn2_chunked(q, log_f, k, v, s0.astype(F32))
    o = rms_norm(o, o_gain) * jax.nn.silu(zg.astype(F32)).reshape(B, T, HGRN_HEADS, HGRN_DV)
    return o.reshape(B, T, HGRN_WIDTH).astype(zq.dtype), S


def swiglu(h, wg, wu, wd):
    a = jnp.einsum('btd,df->btf', h, wg)
    b = jnp.einsum('btd,df->btf', h, wu)
    return jnp.einsum('btf,fd->btd', jax.nn.silu(a) * b, wd)


def moe_swiglu(h, router, wg, wu, wd):
    logits = jnp.einsum('btd,de->bte', h, router).astype(F32)
    top_val, top_idx = lax.top_k(logits, TOP_K)
    weights = jax.nn.softmax(top_val, axis=-1)
    gate = jnp.sum(jax.nn.one_hot(top_idx, N_EXPERTS, dtype=F32) * weights[..., None], axis=-2).astype(h.dtype)
    y = jnp.zeros_like(h)
    for e in range(N_EXPERTS):
        y = y + gate[..., e:e + 1] * swiglu(h, wg[e], wu[e], wd[e])
    return y


def setup_inputs(seed: int = 0) -> dict:
    key = jax.random.key(seed)
    ks = jax.random.split(key, 24)

    def nrm(k, shape, scale):
        return jax.random.normal(k, shape, F32) * scale

    n_pages = PAST_LEN // PAGE_SIZE
    n_used = DEC_BATCH * n_pages
    n_pool = n_used + max(1, n_used // 4)
    w_buf = min(WINDOW, PAST_LEN)
    page_table = jax.random.permutation(ks[0], n_pool)[:n_used].reshape(DEC_BATCH, n_pages).astype(jnp.int32)
    return {
        'x_prompt': nrm(ks[1], (BATCH, SEQ, D_MODEL), 1.0),
        'x_sample': nrm(ks[2], (DEC_BATCH, DEC_SEQ, D_MODEL), 1.0),
        'cache_kv': nrm(ks[3], (DEPTH, n_pool, PAGE_SIZE, 4, NSA_GROUPS, HEAD_DIM), 1.0),
        'state_win_kv': nrm(ks[4], (DEPTH, DEC_BATCH, w_buf, 2, NSA_GROUPS, HEAD_DIM), 1.0),
        'state_hgrn': nrm(ks[5], (DEPTH, DEC_BATCH, HGRN_HEADS, HGRN_DK, HGRN_DV), 0.3),
        'page_table': page_table,
        'norm_mix': 1.0 + nrm(ks[6], (DEPTH, D_MODEL), 0.02),
        'norm_ffn': 1.0 + nrm(ks[7], (DEPTH, D_MODEL), 0.02),
        'w_in': nrm(ks[8], (DEPTH, D_MODEL, IN_COLS), D_MODEL ** -0.5),
        'q_gain': 1.0 + nrm(ks[9], (DEPTH, HEAD_DIM), 0.02),
        'k_gain': 1.0 + nrm(ks[10], (DEPTH, 3, HEAD_DIM), 0.02),
        'cmp_pe': nrm(ks[11], (DEPTH, 2, CMP_LEN, HEAD_DIM), 0.1),
        'cmp_w': nrm(ks[12], (DEPTH, 2, CMP_LEN, HEAD_DIM, HEAD_DIM), (CMP_LEN * HEAD_DIM) ** -0.5),
        'hgrn_lb_logits': nrm(ks[13], (DEPTH, HGRN_HEADS * HGRN_DK), 1.0),
        'hgrn_o_gain': 1.0 + nrm(ks[14], (DEPTH, HGRN_DV), 0.02),
        'w_out': nrm(ks[15], (DEPTH, MIX_WIDTH, D_MODEL), MIX_WIDTH ** -0.5),
        'ffn_w_gate': nrm(ks[16], (N_DENSE, D_MODEL, D_FF), D_MODEL ** -0.5),
        'ffn_w_up': nrm(ks[17], (N_DENSE, D_MODEL, D_FF), D_MODEL ** -0.5),
        'ffn_w_down': nrm(ks[18], (N_DENSE, D_FF, D_MODEL), D_FF ** -0.5),
        'moe_router': nrm(ks[19], (N_MOE, D_MODEL, N_EXPERTS), D_MODEL ** -0.5),
        'moe_w_gate': nrm(ks[20], (N_MOE, N_EXPERTS, D_MODEL, D_FF), D_MODEL ** -0.5),
        'moe_w_up': nrm(ks[21], (N_MOE, N_EXPERTS, D_MODEL, D_FF), D_MODEL ** -0.5),
        'moe_w_down': nrm(ks[22], (N_MOE, N_EXPERTS, D_FF, D_MODEL), D_FF ** -0.5),
    }


def reference(x_prompt, x_sample, cache_kv, state_win_kv, state_hgrn, page_table,
              norm_mix, norm_ffn, w_in, q_gain, k_gain, cmp_pe, cmp_w, hgrn_lb_logits, hgrn_o_gain, w_out,
              ffn_w_gate, ffn_w_up, ffn_w_down, moe_router, moe_w_gate, moe_w_up, moe_w_down):
    slopes = alibi_slopes()
    sm = jax.nn.softmax(hgrn_lb_logits.astype(F32), axis=0)
    lower = jnp.concatenate([jnp.zeros_like(sm[:1]), jnp.cumsum(sm[1:], axis=0)], axis=0)
    B, S, _ = x_prompt.shape
    DB, DS, _ = x_sample.shape
    n_pages = page_table.shape[1]
    past_len = n_pages * PAGE_SIZE
    w_buf = state_win_kv.shape[2]
    xp, xs = x_prompt, x_sample
    kv_p, kv_s, win_p, win_s, hs_p, hs_s = [], [], [], [], [], []
    for l in range(DEPTH):
        if l % 2 == 0:
            i = l // 2
            def ffn(h):
                return swiglu(h, ffn_w_gate[i], ffn_w_up[i], ffn_w_down[i])
        else:
            i = l // 2
            def ffn(h):
                return moe_swiglu(h, moe_router[i], moe_w_gate[i], moe_w_up[i], moe_w_down[i])
        hp = rms_norm(xp, norm_mix[l])
        q, rows, win, gates, hz = project(hp, w_in[l], q_gain[l], k_gain[l])
        kc, vc = compress(rows[:, :, 0], rows[:, :, 1], cmp_pe[l], cmp_w[l], k_gain[l, 0])
        o_nsa = nsa_prompt(q, rows, win, gates, kc, vc, slopes)
        o_hg, s_fin = hgrn2_mix(hz, lower[l], hgrn_o_gain[l], jnp.zeros((B, HGRN_HEADS, HGRN_DK, HGRN_DV), F32))
        xp = xp + jnp.einsum('btm,md->btd', jnp.concatenate([o_nsa, o_hg], axis=-1), w_out[l])
        xp = xp + ffn(rms_norm(xp, norm_ffn[l]))
        kv_p.append(rows)
        win_p.append(win[:, -min(WINDOW, S):])
        hs_p.append(s_fin.astype(x_prompt.dtype))
        hs = rms_norm(xs, norm_mix[l])
        q, rows, win, gates, hz = project(hs, w_in[l], q_gain[l], k_gain[l])
        past = cache_kv[l][page_table].reshape(DB, past_len, 4, NSA_GROUPS, HEAD_DIM)
        rows_all = jnp.concatenate([past, rows.astype(past.dtype)], axis=1)
        kc, vc = compress(rows_all[:, :, 0], rows_all[:, :, 1], cmp_pe[l], cmp_w[l], k_gain[l, 0])
        win_all = jnp.concatenate([state_win_kv[l], win.astype(state_win_kv.dtype)], axis=1)
        o_nsa = nsa_attend(q, past_len + jnp.arange(DS), kc, vc, sel_blocks(rows_all[:, :, 2]), sel_blocks(rows_all[:, :, 3]),
                           win_all[:, :, 0], win_all[:, :, 1], past_len - w_buf + jnp.arange(w_buf + DS), gates, slopes)
        o_hg, s_new = hgrn2_mix(hz, lower[l], hgrn_o_gain[l], state_hgrn[l])
        xs = xs + jnp.einsum('btm,md->btd', jnp.concatenate([o_nsa, o_hg], axis=-1), w_out[l])
        xs = xs + ffn(rms_norm(xs, norm_ffn[l]))
        kv_s.append(rows)
        win_s.append(win_all[:, -w_buf:])
        hs_s.append(s_new.astype(state_hgrn.dtype))
    return (xp, xs, jnp.stack(kv_p), jnp.stack(kv_s), jnp.stack(win_p), jnp.stack(win_s), jnp.stack(hs_p), jnp.stack(hs_s))
```

```python
import functools

import numpy as np
import jax
import jax.numpy as jnp
from jax import lax
from jax.experimental import pallas as pl
from jax.experimental.pallas import tpu as pltpu

F32 = jnp.float32
BF16 = jnp.bfloat16

NSA_HEADS = 8
NSA_GROUPS = 2
HEAD_DIM = 64
NSA_WIDTH = NSA_HEADS * HEAD_DIM
CMP_LEN = 32
CMP_STRIDE = 16
SEL_LEN = 64
N_SEL = 16
N_LOCAL = 2
WINDOW = 512
FORCE_BONUS = 1e4
HGRN_HEADS = 4
HGRN_DK = 128
HGRN_DV = 128
HGRN_WIDTH = HGRN_HEADS * HGRN_DV
N_EXPERTS = 8
EPS = 1e-6
NEG_INF = -1e30
PAGE_SIZE = 128

LANES = 128
KV_COLS = 4 * NSA_GROUPS * HEAD_DIM
CHUNK_COLS = CMP_STRIDE * KV_COLS
IN_PAD_COLS = NSA_WIDTH + 6 * LANES + LANES + 4 * HGRN_WIDTH
VMEM_LIMIT = 48 * 1024 * 1024


def _log2(n):
    assert n & (n - 1) == 0
    return n.bit_length() - 1


def _dot(a, b):
    return jnp.dot(a, b, preferred_element_type=F32)


def _dot_nt(a, b):
    return lax.dot_general(a, b, (((1,), (1,)), ((), ())), preferred_element_type=F32)


def _dot_tn(a, b):
    return lax.dot_general(a, b, (((0,), (0,)), ((), ())), preferred_element_type=F32)


def _split2(x):
    hi = x.astype(BF16)
    lo = (x - hi.astype(F32)).astype(BF16)
    return hi, lo


def _split3(x):
    hi = x.astype(BF16)
    r = x - hi.astype(F32)
    mid = r.astype(BF16)
    lo = (r - mid.astype(F32)).astype(BF16)
    return hi, mid, lo


def _segsum(x, seg):
    hi, lo = _split2(x)
    return _dot(hi, seg) + _dot(lo, seg)


def _seg_rms(z, gain, seg):
    ss = _segsum(z * z, seg) * (1.0 / HEAD_DIM)
    return z * lax.rsqrt(ss + EPS) * gain


def _sigmoid(x):
    return 1.0 / (1.0 + jnp.exp(-x))


def _params(sem, vmem=VMEM_LIMIT):
    return pltpu.CompilerParams(dimension_semantics=sem, vmem_limit_bytes=vmem)


def _proj_kernel(x_ref, gain_ref, w_ref, qg_ref, kg_ref, seg_ref, hp_ref,
                 q_ref, rows_ref, win_ref, kvb_ref, gates_ref, hq_ref, hlf_ref, hk_ref, hv_ref, hg_ref):
    x = x_ref[...]
    tm = x.shape[0]
    ms = jnp.mean(x * x, axis=-1, keepdims=True)
    h = (x * lax.rsqrt(ms + EPS) * gain_ref[...]).astype(BF16)

    def mm(c0, c1):
        return _dot(h, w_ref[:, c0:c1])

    seg = seg_ref[...]
    seg1 = seg_ref[0:LANES, 0:LANES]
    qn = _seg_rms(mm(0, NSA_WIDTH), qg_ref[...], seg) * (HEAD_DIM ** -0.5)
    lane = lax.broadcasted_iota(jnp.int32, (tm, LANES), 1)
    for c in range(4):
        blk = qn[:, LANES * c:LANES * (c + 1)]
        q_ref[:, LANES * c:LANES * (c + 1)] = jnp.where(lane < HEAD_DIM, blk, 0.0).astype(BF16)
        q_ref[:, LANES * (4 + c):LANES * (5 + c)] = jnp.where(lane >= HEAD_DIM, blk, 0.0).astype(BF16)
    c0 = NSA_WIDTH
    kv = mm(c0, c0 + 6 * LANES)
    ksel = _seg_rms(kv[:, 2 * LANES:3 * LANES], kg_ref[0:1, :], seg1)
    kwin = _seg_rms(kv[:, 4 * LANES:5 * LANES], kg_ref[1:2, :], seg1)
    rows_ref[:, 0:2 * LANES] = kv[:, 0:2 * LANES]
    rows_ref[:, 2 * LANES:3 * LANES] = ksel
    rows_ref[:, 3 * LANES:4 * LANES] = kv[:, 3 * LANES:4 * LANES]
    win_ref[:, 0:LANES] = kwin
    win_ref[:, LANES:2 * LANES] = kv[:, 5 * LANES:6 * LANES]
    kvb_ref[:, 0:LANES] = ksel.astype(BF16)
    kvb_ref[:, LANES:2 * LANES] = kv[:, 3 * LANES:4 * LANES].astype(BF16)
    kvb_ref[:, 2 * LANES:3 * LANES] = kwin.astype(BF16)
    kvb_ref[:, 3 * LANES:4 * LANES] = kv[:, 5 * LANES:6 * LANES].astype(BF16)
    c0 += 6 * LANES
    gates_ref[...] = _sigmoid(mm(c0, c0 + LANES))
    c0 += LANES
    zq = mm(c0, c0 + HGRN_WIDTH)
    hq_ref[...] = zq * _sigmoid(zq)
    c0 += HGRN_WIDTH
    zf = mm(c0, c0 + HGRN_WIDTH)
    en = jnp.exp(-jnp.abs(zf))
    log_sig = jnp.minimum(zf, 0.0) - jnp.log1p(en)
    a = hp_ref[0:1, :]
    c = hp_ref[1:2, :] + log_sig
    hlf_ref[...] = jnp.maximum(a, c) + jnp.log1p(jnp.exp(-jnp.abs(a - c)))
    hk_ref[...] = hp_ref[2:3, :] * _sigmoid(-zf)
    c0 += HGRN_WIDTH
    hv_ref[...] = mm(c0, c0 + HGRN_WIDTH)
    c0 += HGRN_WIDTH
    zg = mm(c0, c0 + HGRN_WIDTH)
    hg_ref[...] = zg * _sigmoid(zg)


def _proj(x, gain, w, qg, kg, seg, hp, tm):
    t, d = x.shape
    const = lambda shape: pl.BlockSpec(shape, lambda i: (0,) * len(shape))
    row = lambda n: pl.BlockSpec((tm, n), lambda i: (i, 0))
    outs = [(2 * NSA_WIDTH, BF16), (KV_COLS, F32), (2 * LANES, F32), (4 * LANES, BF16), (LANES, F32)] + [(HGRN_WIDTH, F32)] * 5
    return pl.pallas_call(
        _proj_kernel,
        grid=(t // tm,),
        in_specs=[row(d), const((1, d)), const(w.shape), const(qg.shape), const(kg.shape), const(seg.shape), const(hp.shape)],
        out_specs=[row(n) for n, _ in outs],
        out_shape=[jax.ShapeDtypeStruct((t, n), dt) for n, dt in outs],
        compiler_params=_params(("arbitrary",)),
        name="proj",
    )(x, gain, w, qg, kg, seg, hp)


def _compress_rows(x_ref, pe_ref, w_ref, out_ref):
    pairs = CMP_STRIDE // 2
    for kind in range(2):
        acc = None
        bias = None
        for j in range(pairs):
            w = w_ref[(kind * pairs + j) * 2 * LANES:(kind * pairs + j + 1) * 2 * LANES, :]
            ca = (2 * j) * KV_COLS + kind * LANES
            cb = (2 * j + 1) * KV_COLS + kind * LANES
            lhs = jnp.concatenate([x_ref[:, ca:ca + LANES], x_ref[:, cb:cb + LANES]], axis=1).astype(BF16)
            d = _dot(lhs, w)
            acc = d if acc is None else acc + d
            pe0 = pe_ref[(kind * 2) * pairs + j:(kind * 2) * pairs + j + 1, :]
            pe1 = pe_ref[(kind * 2 + 1) * pairs + j:(kind * 2 + 1) * pairs + j + 1, :]
            pel = jnp.concatenate([jnp.broadcast_to(pe0, (8, 2 * LANES)), jnp.broadcast_to(pe1, (8, 2 * LANES))], axis=0)
            pb = _dot(pel.astype(BF16), w)
            bias = pb if bias is None else bias + pb
        out_ref[:, kind * 2 * LANES:kind * 2 * LANES + LANES] = acc[:, 0:LANES] + bias[0:1, 0:LANES]
        out_ref[:, kind * 2 * LANES + LANES:(kind + 1) * 2 * LANES] = acc[:, LANES:2 * LANES] + bias[8:9, LANES:2 * LANES]


def _compress_prompt_kernel(x_ref, pe_ref, w_ref, out_ref):
    _compress_rows(x_ref, pe_ref, w_ref, out_ref)


def _compress_prompt(rows_chunks, pe, w, nch):
    total = rows_chunks.shape[0]
    const = lambda shape: pl.BlockSpec(shape, lambda i: (0,) * len(shape))
    return pl.pallas_call(
        _compress_prompt_kernel,
        grid=(total // nch,),
        in_specs=[pl.BlockSpec((nch, CHUNK_COLS), lambda i: (i, 0)), const(pe.shape), const(w.shape)],
        out_specs=pl.BlockSpec((nch, 4 * LANES), lambda i: (i, 0)),
        out_shape=jax.ShapeDtypeStruct((total, 4 * LANES), F32),
        compiler_params=_params(("arbitrary",)),
        name="compress_prompt",
    )(rows_chunks, pe, w)


def _page_chunk_copy(pt_ref, cache_ref, buf_ref, sem_ref, step, p, slot, *, pages_per_step, steps_per_seq, page_base):
    chunks_per_page = PAGE_SIZE // CMP_STRIDE
    b = step // steps_per_seq
    half = step % steps_per_seq
    page = pt_ref[b, half * pages_per_step + p] + page_base
    return pltpu.make_async_copy(
        cache_ref.at[pl.ds(page * chunks_per_page, chunks_per_page), :],
        buf_ref.at[slot, pl.ds(p * chunks_per_page, chunks_per_page), :],
        sem_ref.at[slot])


def _compress_sample_kernel(pt_ref, cache_ref, pe_ref, w_ref, out_ref, buf_ref, sem_ref, *, pages_per_step, steps_per_seq, page_base):
    step = pl.program_id(0)
    nstep = pl.num_programs(0)
    slot = step % 2
    copy = functools.partial(_page_chunk_copy, pt_ref, cache_ref, buf_ref, sem_ref,
                             pages_per_step=pages_per_step, steps_per_seq=steps_per_seq, page_base=page_base)

    def start_all(s, sl):
        def body(p, carry):
            copy(s, p, sl).start()
            return carry
        lax.fori_loop(0, pages_per_step, body, 0)

    @pl.when(step == 0)
    def _():
        start_all(step, slot)

    @pl.when(step + 1 < nstep)
    def _():
        start_all(step + 1, 1 - slot)

    def wait_body(p, carry):
        copy(step, p, slot).wait()
        return carry
    lax.fori_loop(0, pages_per_step, wait_body, 0)
    _compress_rows(buf_ref.at[slot], pe_ref, w_ref, out_ref)


def _compress_sample(page_table, cache_chunks, pe, w, *, page_base, pages_per_step):
    db, n_pages = page_table.shape
    steps_per_seq = n_pages // pages_per_step
    m = pages_per_step * (PAGE_SIZE // CMP_STRIDE)
    const = lambda shape: pl.BlockSpec(shape, lambda i, pt: (0,) * len(shape))
    kern = functools.partial(_compress_sample_kernel, pages_per_step=pages_per_step, steps_per_seq=steps_per_seq, page_base=page_base)
    return pl.pallas_call(
        kern,
        grid_spec=pltpu.PrefetchScalarGridSpec(
            num_scalar_prefetch=1,
            grid=(db * steps_per_seq,),
            in_specs=[pl.BlockSpec(memory_space=pl.ANY), const(pe.shape), const(w.shape)],
            out_specs=pl.BlockSpec((m, 4 * LANES), lambda i, pt: (i, 0)),
            scratch_shapes=[pltpu.VMEM((2, m, CHUNK_COLS), F32), pltpu.SemaphoreType.DMA((2,))],
        ),
        out_shape=jax.ShapeDtypeStruct((db * steps_per_seq * m, 4 * LANES), F32),
        compiler_params=_params(("arbitrary",)),
        name="compress_sample",
    )(page_table, cache_chunks, pe, w)


def _row_meta(tq, q0):
    r = NSA_HEADS * tq
    rid = lax.broadcasted_iota(jnp.int32, (r, 1), 0)
    hh = rid >> _log2(tq)
    ii = rid & (tq - 1)
    slope = lax.bitcast_convert_type((126 - hh) << 23, F32)
    qpos = q0 + ii
    return ii, slope, qpos


def _cmp_branch(qp, p_all, kg, seg1, slope, qpos):
    nc = p_all.shape[0]
    up = lambda a: pltpu.roll(a, nc - 1, 0)
    kc_raw = p_all[:, 0:LANES] + up(p_all[:, LANES:2 * LANES])
    vc = p_all[:, 2 * LANES:3 * LANES] + up(p_all[:, 3 * LANES:4 * LANES])
    kc = _seg_rms(kc_raw, kg, seg1)
    s = _dot_nt(qp, kc.astype(BF16))
    cstart = lax.broadcasted_iota(jnp.int32, (1, nc), 1) * CMP_STRIDE
    cdist = jnp.abs(qpos.astype(F32) - (cstart.astype(F32) + 0.5 * (CMP_LEN - 1)))
    c_ok = (cstart + (CMP_LEN - 1)) <= qpos
    s = jnp.where(c_ok, s - slope * cdist, NEG_INF)
    mx = jnp.max(s, axis=-1, keepdims=True)
    e = jnp.exp(s - mx)
    p = e / jnp.sum(e, axis=-1, keepdims=True)
    p = p * (qpos >= CMP_LEN - 1).astype(F32)
    o_cmp = _dot(p.astype(BF16), vc.astype(BF16))
    return o_cmp, p


def _group_importance(p, tq):
    hpg = NSA_HEADS // NSA_GROUPS
    imps = []
    for g in range(NSA_GROUPS):
        acc = p[(g * hpg) * tq:(g * hpg + 1) * tq]
        for h in range(1, hpg):
            acc = acc + p[(g * hpg + h) * tq:(g * hpg + h + 1) * tq]
        imps.append(acc)
    return jnp.concatenate(imps, axis=0)


def _flash_init(m_ref, l_ref, acc_ref):
    m_ref[...] = jnp.full(m_ref.shape, NEG_INF, F32)
    l_ref[...] = jnp.zeros(l_ref.shape, F32)
    acc_ref[...] = jnp.zeros(acc_ref.shape, F32)


def _flash_update(m_ref, l_ref, acc_ref, s, v):
    m_old = m_ref[...]
    m_new = jnp.maximum(m_old, jnp.max(s, axis=-1, keepdims=True))
    alpha = jnp.exp(m_old - m_new)
    p = jnp.exp(s - m_new)
    l_ref[...] = alpha * l_ref[...] + jnp.sum(p, axis=-1, keepdims=True)
    acc_ref[...] = alpha * acc_ref[...] + _dot(p.astype(BF16), v)
    m_ref[...] = m_new


def _expand_mask(mask2b, blk0, nkt, tq):
    nbp = mask2b.shape[1]
    j_e = lax.broadcasted_iota(jnp.int32, (nbp, nkt), 0)
    c_e = lax.broadcasted_iota(jnp.int32, (nbp, nkt), 1)
    e = (j_e == blk0 + (c_e >> _log2(SEL_LEN))).astype(BF16)
    me2 = _dot(mask2b, e)
    hpg = NSA_HEADS // NSA_GROUPS
    return jnp.concatenate([me2[0:tq]] * hpg + [me2[tq:2 * tq]] * hpg, axis=0)


def _combine_heads(gates, o_cmp, o_sel, o_win, tq, out_ref):
    lane = lax.broadcasted_iota(jnp.int32, (tq, LANES), 1)
    outs = []
    for h in range(NSA_HEADS):
        rs = slice(h * tq, (h + 1) * tq)
        outs.append(gates[:, h:h + 1] * o_cmp[rs] + gates[:, NSA_HEADS + h:NSA_HEADS + h + 1] * o_sel[rs]
                    + gates[:, 2 * NSA_HEADS + h:2 * NSA_HEADS + h + 1] * o_win[rs])
    for c in range(4):
        out_ref[:, LANES * c:LANES * (c + 1)] = jnp.where(lane < HEAD_DIM, outs[c], outs[4 + c]).astype(out_ref.dtype)


def _nsa_prompt_kernel(q_ref, kvb_ref, p_ref, gates_ref, kg_ref, seg_ref, tt_ref, o_ref,
                       sc_ref, m_ref, l_ref, acc_ref, *, tq, nkt, n_blk):
    q0 = pl.program_id(1) * tq
    q_last = q0 + tq - 1
    r = NSA_HEADS * tq
    qp = jnp.concatenate([q_ref[:, LANES * h:LANES * (h + 1)] for h in range(NSA_HEADS)], axis=0)
    ii, slope, qpos = _row_meta(tq, q0)

    o_cmp, p_cmp = _cmp_branch(qp, p_ref[...], kg_ref[...], seg_ref[...], slope, qpos)

    imp2 = _group_importance(p_cmp, tq)
    tt = tt_ref[...]
    blk_t = None
    for part in _split3(imp2):
        d = _dot_nt(tt, part)
        blk_t = d if blk_t is None else blk_t + d
    nbp = tt.shape[0]
    j_t = lax.broadcasted_iota(jnp.int32, (nbp, 2 * tq), 0)
    qcol = lax.broadcasted_iota(jnp.int32, (nbp, 2 * tq), 1) & (tq - 1)
    back = ((q0 + qcol) >> _log2(SEL_LEN)) - j_t
    forced = (j_t == 0) | ((back >= 0) & (back < N_LOCAL))
    sc_ref[...] = jnp.where(back >= 0, blk_t + jnp.where(forced, FORCE_BONUS, 0.0), -1.0)

    def rank_body(i, cnt):
        row = sc_ref[pl.ds(i, 1), :]
        sc = sc_ref[...]
        return cnt + jnp.where(j_t > i, (row >= sc).astype(F32), (row > sc).astype(F32))
    cnt = lax.fori_loop(0, jnp.minimum(q_last // SEL_LEN + 1, n_blk), rank_body, jnp.zeros((nbp, 2 * tq), F32))
    sel_t = (cnt < N_SEL).astype(F32)
    mask2b = jnp.concatenate([sel_t[:, 0:tq].T, sel_t[:, tq:2 * tq].T], axis=0).astype(BF16)

    _flash_init(m_ref, l_ref, acc_ref)
    dist0 = (ii - lax.broadcasted_iota(jnp.int32, (r, nkt), 1)).astype(F32)

    def sel_body(kt, carry):
        k0 = pl.multiple_of(kt * nkt, nkt)
        kk = kvb_ref[pl.ds(k0, nkt), 0:LANES]
        vv = kvb_ref[pl.ds(k0, nkt), LANES:2 * LANES]
        dist = dist0 + (q0 - k0).astype(F32)
        me = _expand_mask(mask2b, k0 // SEL_LEN, nkt, tq)
        ok = (me > 0.5) & (dist >= 0.0)
        s = jnp.where(ok, _dot_nt(qp, kk) - slope * dist, NEG_INF)
        _flash_update(m_ref, l_ref, acc_ref, s, vv)
        return carry
    lax.fori_loop(0, q_last // nkt + 1, sel_body, 0)
    o_sel = acc_ref[...] / l_ref[...]

    _flash_init(m_ref, l_ref, acc_ref)
    distw0 = (ii - lax.broadcasted_iota(jnp.int32, (r, tq), 1)).astype(F32)
    for t in range(WINDOW // tq + 1):
        off = WINDOW - t * tq
        k0 = q0 - off

        @pl.when(k0 >= 0)
        def _():
            ks = pl.multiple_of(k0, tq)
            kk = kvb_ref[pl.ds(ks, tq), 2 * LANES:3 * LANES]
            vv = kvb_ref[pl.ds(ks, tq), 3 * LANES:4 * LANES]
            dist = distw0 + float(off)
            ok = (dist >= 0.0) & (dist < float(WINDOW))
            s = jnp.where(ok, _dot_nt(qp, kk) - slope * dist, NEG_INF)
            _flash_update(m_ref, l_ref, acc_ref, s, vv)
    o_win = acc_ref[...] / l_ref[...]

    _combine_heads(gates_ref[...], o_cmp, o_sel, o_win, tq, o_ref)


def _nsa_prompt(q, kvb, p, gates, kg, seg1, tt, *, batch, seq, tq, nkt):
    nq = seq // tq
    nc = seq // CMP_STRIDE
    n_blk = -(-seq // SEL_LEN)
    r = NSA_HEADS * tq
    const = lambda shape: pl.BlockSpec(shape, lambda b, i: (0,) * len(shape))
    kern = functools.partial(_nsa_prompt_kernel, tq=tq, nkt=nkt, n_blk=n_blk)
    return pl.pallas_call(
        kern,
        grid=(batch, nq),
        in_specs=[
            pl.BlockSpec((tq, 2 * NSA_WIDTH), lambda b, i: (b * nq + i, 0)),
            pl.BlockSpec((seq, 4 * LANES), lambda b, i: (b, 0)),
            pl.BlockSpec((nc, 4 * LANES), lambda b, i: (b, 0)),
            pl.BlockSpec((tq, LANES), lambda b, i: (b * nq + i, 0)),
            const(kg.shape), const(seg1.shape), const(tt.shape),
        ],
        out_specs=pl.BlockSpec((tq, NSA_WIDTH), lambda b, i: (b * nq + i, 0)),
        out_shape=jax.ShapeDtypeStruct((batch * seq, NSA_WIDTH), BF16),
        scratch_shapes=[pltpu.VMEM((tt.shape[0], 2 * tq), F32), pltpu.VMEM((r, 1), F32), pltpu.VMEM((r, 1), F32), pltpu.VMEM((r, LANES), F32)],
        compiler_params=_params(("arbitrary", "arbitrary")),
        name="nsa_prompt",
    )(q, kvb, p, gates, kg, seg1, tt)


def _sel_page_copy(pt_ref, cache_ref, buf_ref, sem_ref, b, p, slot, *, page_base):
    page = pt_ref[b, p] + page_base
    return pltpu.make_async_copy(
        cache_ref.at[pl.ds(page * PAGE_SIZE, PAGE_SIZE), pl.ds(2 * LANES, 2 * LANES)],
        buf_ref.at[slot, pl.ds(p * PAGE_SIZE, PAGE_SIZE), :],
        sem_ref.at[slot])


def _nsa_sample_kernel(pt_ref, q_ref, p_ref, rows_ref, wnew_ref, wst_ref, gates_ref, kg_ref, seg_ref, tmat_ref, cache_ref,
                       o_ref, wout_ref, buf_ref, sem_ref, m_ref, l_ref, acc_ref, *, tq, nkt, n_pages, page_base):
    b = pl.program_id(0)
    nb = pl.num_programs(0)
    slot = b % 2
    past = n_pages * PAGE_SIZE
    wb = wst_ref.shape[0]
    copy = functools.partial(_sel_page_copy, pt_ref, cache_ref, buf_ref, sem_ref, page_base=page_base)

    def start_all(s, sl):
        def body(p, carry):
            copy(s, p, sl).start()
            return carry
        lax.fori_loop(0, n_pages, body, 0)

    @pl.when(b == 0)
    def _():
        start_all(b, slot)

    @pl.when(b + 1 < nb)
    def _():
        start_all(b + 1, 1 - slot)

    r = NSA_HEADS * tq
    qp = jnp.concatenate([q_ref[:, LANES * h:LANES * (h + 1)] for h in range(NSA_HEADS)], axis=0).astype(BF16)
    ii, slope, qpos = _row_meta(tq, past)
    o_cmp, p_cmp = _cmp_branch(qp, p_ref[...], kg_ref[...], seg_ref[...], slope, qpos)

    imp2 = _group_importance(p_cmp, tq)
    tmat = tmat_ref[...]
    blk = None
    for part in _split3(imp2):
        d = _dot(part, tmat)
        blk = d if blk is None else blk + d
    nbp = tmat.shape[1]
    n_blk = -(-(past + tq) // SEL_LEN)
    j_l = lax.broadcasted_iota(jnp.int32, (2 * tq, nbp), 1)
    qrow = lax.broadcasted_iota(jnp.int32, (2 * tq, nbp), 0) & (tq - 1)
    back = ((past + qrow) >> _log2(SEL_LEN)) - j_l
    forced = (j_l == 0) | ((back >= 0) & (back < N_LOCAL))
    score = jnp.where(back >= 0, blk + jnp.where(forced, FORCE_BONUS, 0.0), -1.0)
    cnt = jnp.zeros((2 * tq, nbp), F32)
    for i in range(n_blk):
        col = score[:, i:i + 1]
        cnt = cnt + jnp.where(j_l > i, (col >= score).astype(F32), (col > score).astype(F32))
    mask2 = (cnt < N_SEL).astype(F32)
    mask2b = mask2.astype(BF16)

    def wait_body(p, carry):
        copy(b, p, slot).wait()
        return carry
    lax.fori_loop(0, n_pages, wait_body, 0)

    _flash_init(m_ref, l_ref, acc_ref)
    dist0 = (past + ii - lax.broadcasted_iota(jnp.int32, (r, nkt), 1)).astype(F32)

    def sel_body(kt, carry):
        k0 = pl.multiple_of(kt * nkt, nkt)
        kk = buf_ref[slot, pl.ds(k0, nkt), 0:LANES].astype(BF16)
        vv = buf_ref[slot, pl.ds(k0, nkt), LANES:2 * LANES].astype(BF16)
        dist = dist0 - k0.astype(F32)
        me = _expand_mask(mask2b, k0 // SEL_LEN, nkt, tq)
        s = jnp.where(me > 0.5, _dot_nt(qp, kk) - slope * dist, NEG_INF)
        _flash_update(m_ref, l_ref, acc_ref, s, vv)
        return carry
    lax.fori_loop(0, past // nkt, sel_body, 0)

    hpg = NSA_HEADS // NSA_GROUPS
    zpad = jnp.zeros((LANES - tq, LANES), F32)
    dist_new = (ii - lax.broadcasted_iota(jnp.int32, (r, LANES), 1)).astype(F32)
    new_blk = past // SEL_LEN
    me_new = jnp.concatenate([mask2[0:tq, new_blk:new_blk + 1]] * hpg + [mask2[tq:2 * tq, new_blk:new_blk + 1]] * hpg, axis=0)
    k_new = jnp.concatenate([rows_ref[:, 2 * LANES:3 * LANES], zpad], axis=0).astype(BF16)
    v_new = jnp.concatenate([rows_ref[:, 3 * LANES:4 * LANES], zpad], axis=0).astype(BF16)
    ok = (me_new > 0.5) & (dist_new >= 0.0)
    s = jnp.where(ok, _dot_nt(qp, k_new) - slope * dist_new, NEG_INF)
    _flash_update(m_ref, l_ref, acc_ref, s, v_new)
    o_sel = acc_ref[...] / l_ref[...]

    _flash_init(m_ref, l_ref, acc_ref)
    wst = wst_ref[...]
    dist_w = (ii + wb - lax.broadcasted_iota(jnp.int32, (r, wb), 1)).astype(F32)
    ok = (dist_w >= 0.0) & (dist_w < float(WINDOW))
    s = jnp.where(ok, _dot_nt(qp, wst[:, 0:LANES].astype(BF16)) - slope * dist_w, NEG_INF)
    _flash_update(m_ref, l_ref, acc_ref, s, wst[:, LANES:2 * LANES].astype(BF16))
    wnew = wnew_ref[...]
    kw_new = jnp.concatenate([wnew[:, 0:LANES], zpad], axis=0).astype(BF16)
    vw_new = jnp.concatenate([wnew[:, LANES:2 * LANES], zpad], axis=0).astype(BF16)
    s = jnp.where(dist_new >= 0.0, _dot_nt(qp, kw_new) - slope * dist_new, NEG_INF)
    _flash_update(m_ref, l_ref, acc_ref, s, vw_new)
    o_win = acc_ref[...] / l_ref[...]

    _combine_heads(gates_ref[...], o_cmp, o_sel, o_win, tq, o_ref)
    wout_ref[0:wb - tq, :] = wst[tq:wb, :]
    wout_ref[wb - tq:wb, :] = wnew


def _nsa_sample(page_table, q, p, rows, wnew, wstate, gates, kg, seg1, tmat, cache_rows, *, tq, nkt, page_base, wstate_base):
    db, n_pages = page_table.shape
    past = n_pages * PAGE_SIZE
    nc = past // CMP_STRIDE
    wb = WINDOW
    r = NSA_HEADS * tq
    const = lambda shape: pl.BlockSpec(shape, lambda b, pt: (0,) * len(shape))
    kern = functools.partial(_nsa_sample_kernel, tq=tq, nkt=nkt, n_pages=n_pages, page_base=page_base)
    return pl.pallas_call(
        kern,
        grid_spec=pltpu.PrefetchScalarGridSpec(
            num_scalar_prefetch=1,
            grid=(db,),
            in_specs=[
                pl.BlockSpec((tq, 2 * NSA_WIDTH), lambda b, pt: (b, 0)),
                pl.BlockSpec((nc, 4 * LANES), lambda b, pt: (b, 0)),
                pl.BlockSpec((tq, KV_COLS), lambda b, pt: (b, 0)),
                pl.BlockSpec((tq, 2 * LANES), lambda b, pt: (b, 0)),
                pl.BlockSpec((wb, 2 * LANES), lambda b, pt: (wstate_base + b, 0)),
                pl.BlockSpec((tq, LANES), lambda b, pt: (b, 0)),
                const(kg.shape), const(seg1.shape), const(tmat.shape),
                pl.BlockSpec(memory_space=pl.ANY),
            ],
            out_specs=[pl.BlockSpec((tq, NSA_WIDTH), lambda b, pt: (b, 0)),
                       pl.BlockSpec((wb, 2 * LANES), lambda b, pt: (b, 0))],
            scratch_shapes=[pltpu.VMEM((2, past, 2 * LANES), F32), pltpu.SemaphoreType.DMA((2,)),
                            pltpu.VMEM((r, 1), F32), pltpu.VMEM((r, 1), F32), pltpu.VMEM((r, LANES), F32)],
        ),
        out_shape=[jax.ShapeDtypeStruct((db * tq, NSA_WIDTH), F32), jax.ShapeDtypeStruct((db * wb, 2 * LANES), F32)],
        compiler_params=_params(("arbitrary",)),
        name="nsa_sample",
    )(page_table, q, p, rows, wnew, wstate, gates, kg, seg1, tmat, cache_rows)


def _hgrn_consts(tc):
    nl = int(np.log2(tc))
    t = np.arange(tc)[:, None]
    u = np.arange(tc)[None, :]
    blocks = [(u <= t), (u > t)]
    masks = [np.eye(tc, dtype=bool)]
    for lv in range(nl):
        hs = 1 << lv
        mid = (t // (2 * hs)) * 2 * hs + hs
        ref = mid - 1
        blocks.append((t >= mid) & (u > ref) & (u <= t))
        blocks.append((t < mid) & (u > t) & (u <= ref))
        masks.append((t // (2 * hs) == u // (2 * hs)) & (t % (2 * hs) >= hs) & (u % (2 * hs) < hs))
    cm = np.concatenate(blocks, axis=0).astype(np.float32)
    mk = np.concatenate(masks, axis=0).astype(np.float32)
    return jnp.asarray(cm, BF16), jnp.asarray(mk, F32), nl


def _hgrn_kernel(cm_ref, mk_ref, hq_ref, hlf_ref, hk_ref, hv_ref, hg_ref, og_ref, s0_ref, o_ref, sout_ref, st_ref, *, tc, nl):
    t = pl.program_id(1)
    nt = pl.num_programs(1)

    @pl.when(t == 0)
    def _():
        for h in range(HGRN_HEADS):
            st_ref[h] = s0_ref[h].T

    cm = cm_ref[...]
    for h in range(HGRN_HEADS):
        sl = slice(HGRN_DK * h, HGRN_DK * (h + 1))
        hi, lo = _split2(hlf_ref[:, sl])
        ee = _dot(cm, jnp.concatenate([hi, lo], axis=1))
        x = jnp.exp(ee[:, 0:LANES] + ee[:, LANES:2 * LANES])
        q = hq_ref[:, sl]
        k = hk_ref[:, sl]
        v = hv_ref[:, sl].astype(BF16)
        xb = x[0:tc]
        xs = x[tc:2 * tc]
        a = mk_ref[0:tc, :] * _dot_nt(q.astype(BF16), k.astype(BF16))
        for lv in range(nl):
            xu = x[(2 + 2 * lv) * tc:(3 + 2 * lv) * tc]
            xl = x[(3 + 2 * lv) * tc:(4 + 2 * lv) * tc]
            a = a + mk_ref[(1 + lv) * tc:(2 + lv) * tc, :] * _dot_nt((q * xu).astype(BF16), (k * xl).astype(BF16))
        st = st_ref[h]
        o = _dot(a.astype(BF16), v) + _dot_nt((q * xb).astype(BF16), st.astype(BF16))
        st_new = st * xb[tc - 1:tc, :] + _dot_tn(v, (k * xs).astype(BF16))
        st_ref[h] = st_new
        on = o * lax.rsqrt(jnp.mean(o * o, axis=-1, keepdims=True) + EPS) * og_ref[...]
        o_ref[:, sl] = (on * hg_ref[:, sl]).astype(o_ref.dtype)

        @pl.when(t == nt - 1)
        def _():
            sout_ref[h] = st_new.T


def _hgrn(hq, hlf, hk, hv, hg, og, s0, *, batch, seq, tc, s0_base):
    cm, mk, nl = _hgrn_consts(tc)
    nt = seq // tc
    const = lambda shape: pl.BlockSpec(shape, lambda b, t: (0,) * len(shape))
    row = pl.BlockSpec((tc, HGRN_WIDTH), lambda b, t: (b * nt + t, 0))
    kern = functools.partial(_hgrn_kernel, tc=tc, nl=nl)
    return pl.pallas_call(
        kern,
        grid=(batch, nt),
        in_specs=[const(cm.shape), const(mk.shape), row, row, row, row, row, const(og.shape),
                  pl.BlockSpec((None, HGRN_HEADS, HGRN_DK, HGRN_DV), lambda b, t: (s0_base + b, 0, 0, 0))],
        out_specs=[row, pl.BlockSpec((None, HGRN_HEADS, HGRN_DK, HGRN_DV), lambda b, t: (b, 0, 0, 0))],
        out_shape=[jax.ShapeDtypeStruct((batch * seq, HGRN_WIDTH), BF16),
                   jax.ShapeDtypeStruct((batch, HGRN_HEADS, HGRN_DK, HGRN_DV), F32)],
        scratch_shapes=[pltpu.VMEM((HGRN_HEADS, HGRN_DV, HGRN_DK), F32)],
        compiler_params=_params(("arbitrary", "arbitrary")),
        name="hgrn",
    )(cm, mk, hq, hlf, hk, hv, hg, og, s0)


def _outproj_kernel(*refs, moe):
    if moe:
        x_ref, on_ref, oh_ref, w_ref, g_ref, r_ref, xo_ref, h_ref, gate_ref = refs
    else:
        x_ref, on_ref, oh_ref, w_ref, g_ref, xo_ref, h_ref = refs
    xn = x_ref[...] + _dot(on_ref[...], w_ref[0:NSA_WIDTH, :]) + _dot(oh_ref[...], w_ref[NSA_WIDTH:NSA_WIDTH + HGRN_WIDTH, :])
    xo_ref[...] = xn
    h = xn * lax.rsqrt(jnp.mean(xn * xn, axis=-1, keepdims=True) + EPS) * g_ref[...]
    h_ref[...] = h.astype(BF16)
    if moe:
        logits = None
        rparts = _split3(r_ref[...])
        hparts = _split3(h)
        for i in range(3):
            for j in range(3 - i):
                d = _dot(hparts[i], rparts[j])
                logits = d if logits is None else logits + d
        lane = lax.broadcasted_iota(jnp.int32, logits.shape, 1).astype(F32)
        lg = jnp.where(lane < N_EXPERTS, logits, NEG_INF)
        m1 = jnp.max(lg, axis=-1, keepdims=True)
        i1 = jnp.min(jnp.where(lg == m1, lane, float(LANES)), axis=-1, keepdims=True)
        lg2 = jnp.where(lane == i1, NEG_INF, lg)
        m2 = jnp.max(lg2, axis=-1, keepdims=True)
        i2 = jnp.min(jnp.where(lg2 == m2, lane, float(LANES)), axis=-1, keepdims=True)
        e2 = jnp.exp(m2 - m1)
        den = 1.0 + e2
        gate_ref[...] = jnp.where(lane == i1, 1.0 / den, 0.0) + jnp.where(lane == i2, e2 / den, 0.0)


def _outproj(x, o_nsa, o_hg, w, g, router, tm):
    t, d = x.shape
    moe = router is not None
    const = lambda shape: pl.BlockSpec(shape, lambda i: (0,) * len(shape))
    row = lambda n: pl.BlockSpec((tm, n), lambda i: (i, 0))
    in_specs = [row(d), row(NSA_WIDTH), row(HGRN_WIDTH), const(w.shape), const((1, d))]
    args = [x, o_nsa, o_hg, w, g]
    out_specs = [row(d), row(d)]
    out_shape = [jax.ShapeDtypeStruct((t, d), F32), jax.ShapeDtypeStruct((t, d), BF16)]
    if moe:
        in_specs.append(const(router.shape))
        args.append(router)
        out_specs.append(row(LANES))
        out_shape.append(jax.ShapeDtypeStruct((t, LANES), F32))
    return pl.pallas_call(
        functools.partial(_outproj_kernel, moe=moe),
        grid=(t // tm,),
        in_specs=in_specs, out_specs=out_specs, out_shape=out_shape,
        compiler_params=_params(("arbitrary",)),
        name="outproj_moe" if moe else "outproj",
    )(*args)


def _ffn_kernel(x_ref, h_ref, wg_ref, wu_ref, wd_ref, o_ref):
    f = pl.program_id(1)
    h = h_ref[...]
    a = _dot(h, wg_ref[...])
    b = _dot(h, wu_ref[...])
    d = _dot((a * _sigmoid(a) * b).astype(BF16), wd_ref[...])

    @pl.when(f == 0)
    def _():
        o_ref[...] = x_ref[...] + d

    @pl.when(f > 0)
    def _():
        o_ref[...] = o_ref[...] + d


def _ffn(x, h, wg, wu, wd, tm, tf):
    t, d = x.shape
    ff = wg.shape[1]
    return pl.pallas_call(
        _ffn_kernel,
        grid=(t // tm, ff // tf),
        in_specs=[pl.BlockSpec((tm, d), lambda i, f: (i, 0)), pl.BlockSpec((tm, d), lambda i, f: (i, 0)),
                  pl.BlockSpec((d, tf), lambda i, f: (0, f)), pl.BlockSpec((d, tf), lambda i, f: (0, f)),
                  pl.BlockSpec((tf, d), lambda i, f: (f, 0))],
        out_specs=pl.BlockSpec((tm, d), lambda i, f: (i, 0)),
        out_shape=jax.ShapeDtypeStruct((t, d), F32),
        compiler_params=_params(("arbitrary", "arbitrary")),
        name="ffn",
    )(x, h, wg, wu, wd)


def _moe_kernel(x_ref, h_ref, gate_ref, wg_ref, wu_ref, wd_ref, o_ref):
    e = pl.program_id(1)
    f = pl.program_id(2)
    h = h_ref[...]
    a = _dot(h, wg_ref[...])
    b = _dot(h, wu_ref[...])
    d = _dot((a * _sigmoid(a) * b).astype(BF16), wd_ref[...])
    gate = gate_ref[...]
    lane = lax.broadcasted_iota(jnp.int32, gate.shape, 1)
    gcol = jnp.sum(jnp.where(lane == e, gate, 0.0), axis=-1, keepdims=True)
    first = (e == 0) & (f == 0)

    @pl.when(first)
    def _():
        o_ref[...] = x_ref[...] + gcol * d

    @pl.when(jnp.logical_not(first))
    def _():
        o_ref[...] = o_ref[...] + gcol * d


def _moe(x, h, gate, wg, wu, wd, tm, tf):
    t, d = x.shape
    ne, _, ff = wg.shape
    return pl.pallas_call(
        _moe_kernel,
        grid=(t // tm, ne, ff // tf),
        in_specs=[pl.BlockSpec((tm, d), lambda i, e, f: (i, 0)), pl.BlockSpec((tm, d), lambda i, e, f: (i, 0)),
                  pl.BlockSpec((tm, LANES), lambda i, e, f: (i, 0)),
                  pl.BlockSpec((None, d, tf), lambda i, e, f: (e, 0, f)), pl.BlockSpec((None, d, tf), lambda i, e, f: (e, 0, f)),
                  pl.BlockSpec((None, tf, d), lambda i, e, f: (e, f, 0))],
        out_specs=pl.BlockSpec((tm, d), lambda i, e, f: (i, 0)),
        out_shape=jax.ShapeDtypeStruct((t, d), F32),
        compiler_params=_params(("arbitrary", "arbitrary", "arbitrary")),
        name="moe",
    )(x, h, gate, wg, wu, wd)


def _head_perm():
    idx = []
    for c in range(4):
        idx += list(range(HEAD_DIM * c, HEAD_DIM * (c + 1))) + list(range(HEAD_DIM * (4 + c), HEAD_DIM * (5 + c)))
    return np.asarray(idx, np.int32)


def _tap_matrix(n_cmp, nc_pad, n_blk, nb_pad):
    r_s = SEL_LEN // CMP_STRIDE
    r_c = CMP_LEN // CMP_STRIDE
    taps = np.convolve(np.ones(r_s), np.ones(r_c)) / r_c
    tm = np.zeros((nc_pad, nb_pad), np.float32)
    for j in range(n_blk):
        for kk, w in enumerate(taps):
            n = j * r_s + kk - (r_c - 1)
            if 0 <= n < n_cmp:
                tm[n, j] = w
    return tm


def _layer_weights(l, w_in, q_gain, k_gain, cmp_pe, cmp_w, w_out):
    perm = _head_perm()
    wl = w_in[l]
    q_end = NSA_WIDTH
    kv_end = q_end + 6 * NSA_GROUPS * HEAD_DIM
    gate_end = kv_end + 3 * NSA_HEADS
    d = wl.shape[0]
    w_pad = jnp.concatenate([wl[:, :q_end][:, perm], wl[:, q_end:kv_end], wl[:, kv_end:gate_end],
                             jnp.zeros((d, LANES - 3 * NSA_HEADS), wl.dtype), wl[:, gate_end:]], axis=1).astype(BF16)
    qg = jnp.tile(q_gain[l], NSA_HEADS)[None, :]
    kg_proj = jnp.stack([jnp.tile(k_gain[l, 1], NSA_GROUPS), jnp.tile(k_gain[l, 2], NSA_GROUPS)])
    kg_cmp = jnp.tile(k_gain[l, 0], NSA_GROUPS)[None, :]
    pairs = CMP_STRIDE // 2
    cw = cmp_w[l].reshape(2, 2, pairs, 2, HEAD_DIM, HEAD_DIM)
    eye = jnp.eye(NSA_GROUPS, dtype=cw.dtype)
    w_cmp = jnp.einsum('krjsde,gh->kjsgdrhe', cw, eye).reshape(2 * pairs * 2 * LANES, 2 * LANES).astype(BF16)
    pe = cmp_pe[l].reshape(2, 2, pairs, 2, 1, HEAD_DIM)
    pe = jnp.broadcast_to(pe, (2, 2, pairs, 2, NSA_GROUPS, HEAD_DIM)).reshape(4 * pairs, 2 * LANES)
    wo = w_out[l]
    wo = jnp.concatenate([wo[:NSA_WIDTH][perm], wo[NSA_WIDTH:]], axis=0).astype(BF16)
    return w_pad, qg, kg_proj, kg_cmp, w_cmp, pe, wo


def kernel(x_prompt, x_sample, cache_kv, state_win_kv, state_hgrn, page_table, norm_mix, norm_ffn, w_in, q_gain, k_gain, cmp_pe, cmp_w, hgrn_lb_logits, hgrn_o_gain, w_out, ffn_w_gate, ffn_w_up, ffn_w_down, moe_router, moe_w_gate, moe_w_up, moe_w_down):
    depth = w_in.shape[0]
    batch, seq, d_model = x_prompt.shape
    db, ds, _ = x_sample.shape
    n_pool = cache_kv.shape[1]
    n_pages = page_table.shape[1]
    past = n_pages * PAGE_SIZE
    wb = state_win_kv.shape[2]
    assert wb == WINDOW and seq % 256 == 0 and seq >= WINDOW and ds == 8 and past % 512 == 0

    tq, nkt_p, nkt_s, tc = 128, 256, 512, 128
    tm_p = 256
    tm_f = 512 if (batch * seq) % 512 == 0 else 256
    ts = db * ds
    d_ff = ffn_w_gate.shape[-1]
    tf = d_ff // 2

    sm = jax.nn.softmax(hgrn_lb_logits.astype(F32), axis=0)
    lower = jnp.concatenate([jnp.zeros_like(sm[:1]), jnp.cumsum(sm[1:], axis=0)], axis=0)
    seg = jnp.asarray(np.kron(np.eye(NSA_HEADS), np.ones((HEAD_DIM, HEAD_DIM))), BF16)
    seg1 = seg[:LANES, :LANES]

    nc_p = seq // CMP_STRIDE
    nb_p = -(-seq // SEL_LEN)
    tt_p = jnp.asarray(_tap_matrix(nc_p - 1, nc_p, nb_p, max(LANES, nb_p)).T, BF16)
    nc_s = past // CMP_STRIDE
    nb_s = -(-(past + ds) // SEL_LEN)
    nbp_s = -(-nb_s // LANES) * LANES
    tm_s = jnp.asarray(_tap_matrix(nc_s - 1, nc_s, nb_s, nbp_s), BF16)

    cache_chunks = cache_kv.reshape(depth * n_pool * (PAGE_SIZE // CMP_STRIDE), CHUNK_COLS)
    cache_rows = cache_kv.reshape(depth * n_pool * PAGE_SIZE, KV_COLS)
    wstate = state_win_kv.reshape(depth * db * wb, 2 * LANES)
    hstate = state_hgrn.reshape(depth * db, HGRN_HEADS, HGRN_DK, HGRN_DV)
    zero_state = jnp.zeros((batch, HGRN_HEADS, HGRN_DK, HGRN_DV), F32)

    xp = x_prompt.reshape(batch * seq, d_model)
    xs = x_sample.reshape(ts, d_model)
    kv_p, kv_s, win_p, win_s, hs_p, hs_s = [], [], [], [], [], []
    for l in range(depth):
        w_pad, qg, kg_proj, kg_cmp, w_cmp, pe, wo = _layer_weights(l, w_in, q_gain, k_gain, cmp_pe, cmp_w, w_out)
        lb = lower[l]
        hp = jnp.stack([jnp.log(lb), jnp.log1p(-lb), 1.0 - lb])
        og = hgrn_o_gain[l][None, :]
        g1 = norm_mix[l][None, :]
        g2 = norm_ffn[l][None, :]
        i = l // 2
        if l % 2 == 0:
            router = None
            wg, wu, wd = ffn_w_gate[i].astype(BF16), ffn_w_up[i].astype(BF16), ffn_w_down[i].astype(BF16)
        else:
            router = jnp.pad(moe_router[i], ((0, 0), (0, LANES - N_EXPERTS)))
            wg, wu, wd = moe_w_gate[i].astype(BF16), moe_w_up[i].astype(BF16), moe_w_down[i].astype(BF16)

        def mixer(x, h, gate):
            tm = tm_f if x.shape[0] % tm_f == 0 else x.shape[0]
            if router is None:
                return _ffn(x, h, wg, wu, wd, tm, tf)
            return _moe(x, h, gate, wg, wu, wd, tm, tf)

        q, rows, win, kvb, gates, hq, hlf, hk, hv, hg = _proj(xp, g1, w_pad, qg, kg_proj, seg, hp, tm_p)
        pp = _compress_prompt(rows.reshape(batch * nc_p, CHUNK_COLS), pe, w_cmp, nc_p)
        o_nsa = _nsa_prompt(q, kvb, pp, gates, kg_cmp, seg1, tt_p, batch=batch, seq=seq, tq=tq, nkt=nkt_p)
        o_hg, s_fin = _hgrn(hq, hlf, hk, hv, hg, og, zero_state, batch=batch, seq=seq, tc=tc, s0_base=0)
        res = _outproj(xp, o_nsa, o_hg, wo, g2, router, tm_p)
        xp = mixer(res[0], res[1], res[2] if router is not None else None)
        kv_p.append(rows.reshape(batch, seq, 4, NSA_GROUPS, HEAD_DIM))
        win_p.append(win.reshape(batch, seq, 2, NSA_GROUPS, HEAD_DIM)[:, seq - min(WINDOW, seq):])
        hs_p.append(s_fin)

        q, rows, win, kvb, gates, hq, hlf, hk, hv, hg = _proj(xs, g1, w_pad, qg, kg_proj, seg, hp, ts)
        ps = _compress_sample(page_table, cache_chunks, pe, w_cmp, page_base=l * n_pool, pages_per_step=n_pages // 2)
        o_nsa, win_new = _nsa_sample(page_table, q.astype(F32), ps, rows, win, wstate, gates, kg_cmp, seg1, tm_s, cache_rows,
                                     tq=ds, nkt=nkt_s, page_base=l * n_pool, wstate_base=l * db)
        padt = lambda a: jnp.pad(a.reshape(db, ds, HGRN_WIDTH), ((0, 0), (0, tc - ds), (0, 0))).reshape(db * tc, HGRN_WIDTH)
        o_hg, s_new = _hgrn(padt(hq), padt(hlf), padt(hk), padt(hv), padt(hg), og, hstate, batch=db, seq=tc, tc=tc, s0_base=l * db)
        o_hg = o_hg.reshape(db, tc, HGRN_WIDTH)[:, :ds].reshape(ts, HGRN_WIDTH)
        res = _outproj(xs, o_nsa.astype(BF16), o_hg, wo, g2, router, ts)
        xs = mixer(res[0], res[1], res[2] if router is not None else None)
        kv_s.append(rows.reshape(db, ds, 4, NSA_GROUPS, HEAD_DIM))
        win_s.append(win_new.reshape(db, wb, 2, NSA_GROUPS, HEAD_DIM))
        hs_s.append(s_new)

    return (xp.reshape(batch, seq, d_model), xs.reshape(db, ds, d_model), jnp.stack(kv_p), jnp.stack(kv_s),
            jnp.stack(win_p), jnp.stack(win_s), jnp.stack(hs_p), jnp.stack(hs_s))
```

```python
import functools

import numpy as np
import jax
import jax.numpy as jnp
from jax import lax
from jax.experimental import pallas as pl
from jax.experimental.pallas import tpu as pltpu

F32 = jnp.float32
BF16 = jnp.bfloat16

NSA_HEADS = 8
NSA_GROUPS = 2
HEAD_DIM = 64
NSA_WIDTH = NSA_HEADS * HEAD_DIM
CMP_LEN = 32
CMP_STRIDE = 16
SEL_LEN = 64
N_SEL = 16
N_LOCAL = 2
WINDOW = 512
FORCE_BONUS = 1e4
HGRN_HEADS = 4
HGRN_DK = 128
HGRN_DV = 128
HGRN_WIDTH = HGRN_HEADS * HGRN_DV
N_EXPERTS = 8
EPS = 1e-6
NEG_INF = -1e30
PAGE_SIZE = 128

LANES = 128
KV_COLS = 4 * NSA_GROUPS * HEAD_DIM
VMEM_LIMIT = 48 * 1024 * 1024


def _log2(n):
    assert n & (n - 1) == 0
    return n.bit_length() - 1


def _dot(a, b):
    return jnp.dot(a, b, preferred_element_type=F32)


def _dot_nt(a, b):
    return lax.dot_general(a, b, (((1,), (1,)), ((), ())), preferred_element_type=F32)


def _dot_tn(a, b):
    return lax.dot_general(a, b, (((0,), (0,)), ((), ())), preferred_element_type=F32)


def _split2(x):
    hi = x.astype(BF16)
    lo = (x - hi.astype(F32)).astype(BF16)
    return hi, lo


def _split3(x):
    hi = x.astype(BF16)
    r = x - hi.astype(F32)
    mid = r.astype(BF16)
    lo = (r - mid.astype(F32)).astype(BF16)
    return hi, mid, lo


def _segsum(x, seg):
    hi, lo = _split2(x)
    return _dot(hi, seg) + _dot(lo, seg)


def _seg_rms(z, gain, seg):
    ss = _segsum(z * z, seg) * (1.0 / HEAD_DIM)
    return z * lax.rsqrt(ss + EPS) * gain


def _sigmoid(x):
    return 1.0 / (1.0 + jnp.exp(-x))


def _params(sem, vmem=VMEM_LIMIT):
    return pltpu.CompilerParams(dimension_semantics=sem, vmem_limit_bytes=vmem)


def _proj_core(x_ref, gain_ref, w_ref, qg_ref, kg_ref, seg_ref, hp_ref, hq_ref, hlf_ref, hk_ref, hv_ref, hg_ref):
    x = x_ref[...]
    ms = jnp.mean(x * x, axis=-1, keepdims=True)
    h = (x * lax.rsqrt(ms + EPS) * gain_ref[...]).astype(BF16)

    def mm(c0, c1):
        return _dot(h, w_ref[:, c0:c1])

    seg1 = seg_ref[0:LANES, 0:LANES]
    qn = _seg_rms(mm(0, NSA_WIDTH), qg_ref[...], seg_ref[...]) * (HEAD_DIM ** -0.5)
    c0 = NSA_WIDTH
    kv = mm(c0, c0 + 6 * LANES)
    kvs = [kv[:, LANES * j:LANES * (j + 1)] for j in range(6)]
    kvs[2] = _seg_rms(kvs[2], kg_ref[0:1, :], seg1)
    kvs[4] = _seg_rms(kvs[4], kg_ref[1:2, :], seg1)
    c0 += 6 * LANES
    gates = _sigmoid(mm(c0, c0 + LANES))
    c0 += LANES
    zq = mm(c0, c0 + HGRN_WIDTH)
    hq_ref[...] = zq * _sigmoid(zq)
    c0 += HGRN_WIDTH
    zf = mm(c0, c0 + HGRN_WIDTH)
    log_sig = jnp.minimum(zf, 0.0) - jnp.log1p(jnp.exp(-jnp.abs(zf)))
    a = hp_ref[0:1, :]
    c = hp_ref[1:2, :] + log_sig
    hlf_ref[...] = jnp.maximum(a, c) + jnp.log1p(jnp.exp(-jnp.abs(a - c)))
    hk_ref[...] = hp_ref[2:3, :] * _sigmoid(-zf)
    c0 += HGRN_WIDTH
    hv_ref[...] = mm(c0, c0 + HGRN_WIDTH)
    c0 += HGRN_WIDTH
    zg = mm(c0, c0 + HGRN_WIDTH)
    hg_ref[...] = zg * _sigmoid(zg)
    return qn, kvs, gates


def _proj_prompt_kernel(x_ref, gain_ref, w_ref, qg_ref, kg_ref, seg_ref, hp_ref,
                        qt_ref, rowst_ref, wint_ref, kb_ref, vt_ref, kc_ref, vc_ref, gt_ref,
                        hq_ref, hlf_ref, hk_ref, hv_ref, hg_ref):
    qn, kvs, gates = _proj_core(x_ref, gain_ref, w_ref, qg_ref, kg_ref, seg_ref, hp_ref, hq_ref, hlf_ref, hk_ref, hv_ref, hg_ref)
    for c in range(4):
        qt_ref[LANES * c:LANES * (c + 1), :] = qn[:, LANES * c:LANES * (c + 1)].T.astype(BF16)
    kvt = [a.T for a in kvs]
    for j in range(4):
        rowst_ref[LANES * j:LANES * (j + 1), :] = kvt[j]
    wint_ref[0:LANES, :] = kvt[4]
    wint_ref[LANES:2 * LANES, :] = kvt[5]
    kb_ref[:, 0:LANES] = kvs[2].astype(BF16)
    kb_ref[:, LANES:2 * LANES] = kvs[4].astype(BF16)
    vt_ref[0:LANES, :] = kvt[3].astype(BF16)
    vt_ref[LANES:2 * LANES, :] = kvt[5].astype(BF16)
    kc_ref[...] = kvs[0]
    vc_ref[...] = kvs[1]
    gt_ref[...] = gates.T


def _proj_sample_kernel(x_ref, gain_ref, w_ref, qg_ref, kg_ref, seg_ref, hp_ref,
                        q_ref, rows_ref, win_ref, gates_ref, hq_ref, hlf_ref, hk_ref, hv_ref, hg_ref):
    qn, kvs, gates = _proj_core(x_ref, gain_ref, w_ref, qg_ref, kg_ref, seg_ref, hp_ref, hq_ref, hlf_ref, hk_ref, hv_ref, hg_ref)
    lane = lax.broadcasted_iota(jnp.int32, (qn.shape[0], LANES), 1)
    for c in range(4):
        blk = qn[:, LANES * c:LANES * (c + 1)]
        q_ref[:, LANES * c:LANES * (c + 1)] = jnp.where(lane < HEAD_DIM, blk, 0.0)
        q_ref[:, LANES * (4 + c):LANES * (5 + c)] = jnp.where(lane >= HEAD_DIM, blk, 0.0)
    for j in range(4):
        rows_ref[:, LANES * j:LANES * (j + 1)] = kvs[j]
    win_ref[:, 0:LANES] = kvs[4]
    win_ref[:, LANES:2 * LANES] = kvs[5]
    gates_ref[...] = gates


def _proj_prompt(x, gain, w, qg, kg, seg, hp, *, batch, seq, tm):
    t, d = x.shape
    nt = seq // tm
    const = lambda shape: pl.BlockSpec(shape, lambda b, i: (0,) * len(shape))
    row = lambda n: pl.BlockSpec((tm, n), lambda b, i: (b * nt + i, 0))
    colt = lambda n: pl.BlockSpec((None, n, tm), lambda b, i: (b, 0, i))
    tshape = lambda n, dt: jax.ShapeDtypeStruct((batch, n, seq), dt)
    rshape = lambda n, dt: jax.ShapeDtypeStruct((t, n), dt)
    return pl.pallas_call(
        _proj_prompt_kernel,
        grid=(batch, nt),
        in_specs=[row(d), const((1, d)), const(w.shape), const(qg.shape), const(kg.shape), const(seg.shape), const(hp.shape)],
        out_specs=[colt(NSA_WIDTH), colt(KV_COLS), colt(2 * LANES), row(2 * LANES), colt(2 * LANES), row(LANES), row(LANES),
                   colt(LANES)] + [row(HGRN_WIDTH)] * 5,
        out_shape=[tshape(NSA_WIDTH, BF16), tshape(KV_COLS, F32), tshape(2 * LANES, F32), rshape(2 * LANES, BF16),
                   tshape(2 * LANES, BF16), rshape(LANES, F32), rshape(LANES, F32), tshape(LANES, F32)]
                  + [rshape(HGRN_WIDTH, F32)] * 5,
        compiler_params=_params(("arbitrary", "arbitrary")),
        name="proj_prompt",
    )(x, gain, w, qg, kg, seg, hp)


def _proj_sample(x, gain, w, qg, kg, seg, hp):
    t, d = x.shape
    const = lambda shape: pl.BlockSpec(shape, lambda i: (0,) * len(shape))
    row = lambda n: pl.BlockSpec((t, n), lambda i: (0, 0))
    outs = [2 * NSA_WIDTH, KV_COLS, 2 * LANES, LANES] + [HGRN_WIDTH] * 5
    return pl.pallas_call(
        _proj_sample_kernel,
        grid=(1,),
        in_specs=[row(d), const((1, d)), const(w.shape), const(qg.shape), const(kg.shape), const(seg.shape), const(hp.shape)],
        out_specs=[row(n) for n in outs],
        out_shape=[jax.ShapeDtypeStruct((t, n), F32) for n in outs],
        compiler_params=_params(("arbitrary",)),
        name="proj_sample",
    )(x, gain, w, qg, kg, seg, hp)


def _compress_rows(k_ref, v_ref, pe_ref, w_ref, out_ref, m):
    pairs = CMP_STRIDE // 2
    for kind, ref in ((0, k_ref), (1, v_ref)):
        acc = None
        bias = None
        for j in range(pairs):
            w = w_ref[(kind * pairs + j) * 2 * LANES:(kind * pairs + j + 1) * 2 * LANES, :]
            xa = ref[pl.ds(2 * j, m, stride=CMP_STRIDE), :]
            xb = ref[pl.ds(2 * j + 1, m, stride=CMP_STRIDE), :]
            d = _dot(jnp.concatenate([xa, xb], axis=1).astype(BF16), w)
            acc = d if acc is None else acc + d
            pe0 = pe_ref[(kind * 2) * pairs + j:(kind * 2) * pairs + j + 1, :]
            pe1 = pe_ref[(kind * 2 + 1) * pairs + j:(kind * 2 + 1) * pairs + j + 1, :]
            pel = jnp.concatenate([jnp.broadcast_to(pe0, (8, 2 * LANES)), jnp.broadcast_to(pe1, (8, 2 * LANES))], axis=0)
            pb = _dot(pel.astype(BF16), w)
            bias = pb if bias is None else bias + pb
        out_ref[:, kind * 2 * LANES:kind * 2 * LANES + LANES] = acc[:, 0:LANES] + bias[0:1, 0:LANES]
        out_ref[:, kind * 2 * LANES + LANES:(kind + 1) * 2 * LANES] = acc[:, LANES:2 * LANES] + bias[8:9, LANES:2 * LANES]


def _compress_prompt_kernel(k_ref, v_ref, pe_ref, w_ref, out_ref, *, m):
    _compress_rows(k_ref, v_ref, pe_ref, w_ref, out_ref, m)


def _compress_prompt(kc, vc, pe, w, *, batch, seq):
    m = seq // CMP_STRIDE
    const = lambda shape: pl.BlockSpec(shape, lambda i: (0,) * len(shape))
    return pl.pallas_call(
        functools.partial(_compress_prompt_kernel, m=m),
        grid=(batch,),
        in_specs=[pl.BlockSpec((seq, LANES), lambda i: (i, 0)), pl.BlockSpec((seq, LANES), lambda i: (i, 0)), const(pe.shape), const(w.shape)],
        out_specs=pl.BlockSpec((m, 4 * LANES), lambda i: (i, 0)),
        out_shape=jax.ShapeDtypeStruct((batch * m, 4 * LANES), F32),
        compiler_params=_params(("arbitrary",)),
        name="compress_prompt",
    )(kc, vc, pe, w)


def _page_copy(pt_ref, cache_ref, buf_ref, sem_ref, b, p, slot, *, page_base, kind0):
    page = pt_ref[b, p] + page_base
    return pltpu.make_async_copy(
        cache_ref.at[pl.ds(page * KV_COLS + kind0 * LANES, 2 * LANES), :],
        buf_ref.at[slot, pl.ds(p * 2 * LANES, 2 * LANES), :],
        sem_ref.at[slot])


def _page_pipeline(copy, b, nb, n_pages):
    slot = b % 2

    def start_all(s, sl):
        def body(p, carry):
            copy(s, p, sl).start()
            return carry
        lax.fori_loop(0, n_pages, body, 0)

    @pl.when(b == 0)
    def _():
        start_all(b, slot)

    @pl.when(b + 1 < nb)
    def _():
        start_all(b + 1, 1 - slot)

    def wait_all():
        def body(p, carry):
            copy(b, p, slot).wait()
            return carry
        lax.fori_loop(0, n_pages, body, 0)
    return wait_all


def _compress_sample_kernel(pt_ref, cache_ref, pe_ref, w_ref, out_ref, buf_ref, sem_ref, ktok_ref, vtok_ref, *, n_pages, page_base):
    b = pl.program_id(0)
    slot = b % 2
    copy = functools.partial(_page_copy, pt_ref, cache_ref, buf_ref, sem_ref, page_base=page_base, kind0=0)
    wait_all = _page_pipeline(copy, b, pl.num_programs(0), n_pages)
    wait_all()

    def tr_body(p, carry):
        r0 = pl.multiple_of(p * 2 * LANES, 2 * LANES)
        t0 = pl.multiple_of(p * PAGE_SIZE, PAGE_SIZE)
        ktok_ref[pl.ds(t0, PAGE_SIZE), :] = buf_ref[slot, pl.ds(r0, LANES), :].T
        vtok_ref[pl.ds(t0, PAGE_SIZE), :] = buf_ref[slot, pl.ds(r0 + LANES, LANES), :].T
        return carry
    lax.fori_loop(0, n_pages, tr_body, 0)
    _compress_rows(ktok_ref, vtok_ref, pe_ref, w_ref, out_ref, n_pages * PAGE_SIZE // CMP_STRIDE)


def _compress_sample(page_table, cache_t, pe, w, *, page_base):
    db, n_pages = page_table.shape
    past = n_pages * PAGE_SIZE
    m = past // CMP_STRIDE
    const = lambda shape: pl.BlockSpec(shape, lambda i, pt: (0,) * len(shape))
    kern = functools.partial(_compress_sample_kernel, n_pages=n_pages, page_base=page_base)
    return pl.pallas_call(
        kern,
        grid_spec=pltpu.PrefetchScalarGridSpec(
            num_scalar_prefetch=1,
            grid=(db,),
            in_specs=[pl.BlockSpec(memory_space=pl.ANY), const(pe.shape), const(w.shape)],
            out_specs=pl.BlockSpec((m, 4 * LANES), lambda i, pt: (i, 0)),
            scratch_shapes=[pltpu.VMEM((2, n_pages * 2 * LANES, LANES), F32), pltpu.SemaphoreType.DMA((2,)),
                            pltpu.VMEM((past, LANES), F32), pltpu.VMEM((past, LANES), F32)],
        ),
        out_shape=jax.ShapeDtypeStruct((db * m, 4 * LANES), F32),
        compiler_params=_params(("arbitrary",)),
        name="compress_sample",
    )(page_table, cache_t, pe, w)


def _compressed_kv(p_all, kg, seg1):
    nc = p_all.shape[0]
    up = lambda a: pltpu.roll(a, nc - 1, 0)
    kc_raw = p_all[:, 0:LANES] + up(p_all[:, LANES:2 * LANES])
    vc = p_all[:, 2 * LANES:3 * LANES] + up(p_all[:, 3 * LANES:4 * LANES])
    return _seg_rms(kc_raw, kg, seg1), vc


def _head_slope(h):
    return 2.0 ** -(h + 1)


def _nsa_prompt_kernel(qt_ref, k_ref, vt_ref, p_ref, gt_ref, kg_ref, seg_ref, tt_ref, o_ref,
                       sc_ref, sel_ref, m_ref, l_ref, acc_ref, *, tq, nkt, n_blk):
    q0 = pl.program_id(1) * tq
    q_last = q0 + tq - 1
    hpg = NSA_HEADS // NSA_GROUPS
    lane_blk = lambda a, h: a[:, h * tq:(h + 1) * tq]

    row = lax.broadcasted_iota(jnp.int32, (LANES, tq), 0)
    cols = [None] * NSA_HEADS
    for c in range(4):
        blk = qt_ref[LANES * c:LANES * (c + 1), :]
        cols[c] = jnp.where(row < HEAD_DIM, blk, jnp.zeros_like(blk))
        cols[4 + c] = jnp.where(row >= HEAD_DIM, blk, jnp.zeros_like(blk))
    qpt = jnp.concatenate(cols, axis=1)

    qpos = q0 + lax.broadcasted_iota(jnp.int32, (1, tq), 1)
    qpos_f = qpos.astype(F32)

    kc, vc = _compressed_kv(p_ref[...], kg_ref[...], seg_ref[...])
    nc = kc.shape[0]
    s_c = _dot(kc.astype(BF16), qpt)
    cstart = lax.broadcasted_iota(jnp.int32, (nc, 1), 0) * CMP_STRIDE
    cdist = jnp.abs(qpos_f - (cstart.astype(F32) + 0.5 * (CMP_LEN - 1)))
    c_ok = (cstart + (CMP_LEN - 1)) <= qpos
    any_ok = (qpos >= CMP_LEN - 1).astype(F32)
    ps = []
    for h in range(NSA_HEADS):
        s = jnp.where(c_ok, lane_blk(s_c, h) - _head_slope(h) * cdist, NEG_INF)
        e = jnp.exp(s - jnp.max(s, axis=0, keepdims=True))
        ps.append(e * (any_ok / jnp.sum(e, axis=0, keepdims=True)))
    o_cmp = _dot(vc.T.astype(BF16), jnp.concatenate(ps, axis=1).astype(BF16))

    imps = []
    for g in range(NSA_GROUPS):
        acc = ps[g * hpg]
        for h in range(1, hpg):
            acc = acc + ps[g * hpg + h]
        imps.append(acc)
    tt = tt_ref[...]
    blk_t = None
    for part in _split3(jnp.concatenate(imps, axis=1)):
        d = _dot(tt, part)
        blk_t = d if blk_t is None else blk_t + d
    nbp = tt.shape[0]
    j_t = lax.broadcasted_iota(jnp.int32, (nbp, 2 * tq), 0)
    qcol = lax.broadcasted_iota(jnp.int32, (nbp, 2 * tq), 1) & (tq - 1)
    back = ((q0 + qcol) >> _log2(SEL_LEN)) - j_t
    forced = (j_t == 0) | ((back >= 0) & (back < N_LOCAL))
    sc_ref[...] = jnp.where(back >= 0, blk_t + jnp.where(forced, FORCE_BONUS, 0.0), -1.0)

    def rank_body(i, cnt):
        r = sc_ref[pl.ds(i, 1), :]
        sc = sc_ref[...]
        return cnt + jnp.where(j_t > i, (r >= sc).astype(F32), (r > sc).astype(F32))
    cnt = lax.fori_loop(0, jnp.minimum(q_last // SEL_LEN + 1, n_blk), rank_body, jnp.zeros((nbp, 2 * tq), F32))
    sel_ref[...] = jnp.where(cnt < N_SEL, 0.0, NEG_INF)

    def flash_init():
        m_ref[...] = jnp.full(m_ref.shape, NEG_INF, F32)
        l_ref[...] = jnp.zeros(l_ref.shape, F32)
        acc_ref[...] = jnp.zeros(acc_ref.shape, F32)

    def flash_tile(s_t, dist, mb, vt):
        m_old = m_ref[...]
        l_old = l_ref[...]
        pbs, ms, ls, als = [], [], [], []
        for h in range(NSA_HEADS):
            t = lane_blk(s_t, h) - _head_slope(h) * dist + mb[h // hpg]
            m_h = jnp.maximum(lane_blk(m_old, h), jnp.max(t, axis=0, keepdims=True))
            al = jnp.exp(lane_blk(m_old, h) - m_h)
            p = jnp.exp(t - m_h)
            ls.append(al * lane_blk(l_old, h) + jnp.sum(p, axis=0, keepdims=True))
            pbs.append(p.astype(BF16))
            ms.append(m_h)
            als.append(al)
        acc_ref[...] = acc_ref[...] * jnp.concatenate(als, axis=1) + _dot(vt, jnp.concatenate(pbs, axis=1))
        m_ref[...] = jnp.concatenate(ms, axis=1)
        l_ref[...] = jnp.concatenate(ls, axis=1)

    flash_init()
    d0 = (lax.broadcasted_iota(jnp.int32, (nkt, tq), 1) - lax.broadcasted_iota(jnp.int32, (nkt, tq), 0)).astype(F32)

    def sel_body(kt, carry):
        k0 = pl.multiple_of(kt * nkt, nkt)
        dist = d0 + jnp.asarray(q0 - k0, F32)
        causal = jnp.where(dist < 0.0, NEG_INF, 0.0)
        j0 = k0 // SEL_LEN
        rows = [jnp.broadcast_to(sel_ref[pl.ds(j0 + b, 1), :], (SEL_LEN, 2 * tq)) for b in range(nkt // SEL_LEN)]
        mrow = jnp.concatenate(rows, axis=0)
        mb = [mrow[:, 0:tq] + causal, mrow[:, tq:2 * tq] + causal]
        s_t = _dot(k_ref[pl.ds(k0, nkt), 0:LANES], qpt)
        flash_tile(s_t, dist, mb, vt_ref[0:LANES, pl.ds(k0, nkt)])
        return carry
    lax.fori_loop(0, q_last // nkt + 1, sel_body, 0)
    o_sel = acc_ref[...] * (1.0 / l_ref[...])

    flash_init()
    dw0 = (lax.broadcasted_iota(jnp.int32, (tq, tq), 1) - lax.broadcasted_iota(jnp.int32, (tq, tq), 0)).astype(F32)
    for t in range(WINDOW // tq + 1):
        off = WINDOW - t * tq
        k0 = q0 - off

        @pl.when(k0 >= 0)
        def _():
            ks = pl.multiple_of(k0, tq)
            dist = dw0 + float(off)
            wmask = jnp.where((dist >= 0.0) & (dist < float(WINDOW)), 0.0, NEG_INF)
            s_t = _dot(k_ref[pl.ds(ks, tq), LANES:2 * LANES], qpt)
            flash_tile(s_t, dist, [wmask, wmask], vt_ref[LANES:2 * LANES, pl.ds(ks, tq)])
    o_win = acc_ref[...] * (1.0 / l_ref[...])

    gt = gt_ref[...]
    for c in range(4):
        halves = []
        for h, lo in ((c, 0), (4 + c, HEAD_DIM)):
            g = lambda br: gt[br * NSA_HEADS + h:br * NSA_HEADS + h + 1, :]
            pick = lambda a: a[lo:lo + HEAD_DIM, h * tq:(h + 1) * tq]
            halves.append(g(0) * pick(o_cmp) + g(1) * pick(o_sel) + g(2) * pick(o_win))
        o_ref[:, LANES * c:LANES * (c + 1)] = jnp.concatenate(halves, axis=0).T.astype(o_ref.dtype)


def _nsa_prompt(qt, kb, vt, p, gt, kg, seg1, tt, *, batch, seq, tq, nkt):
    nq = seq // tq
    nc = seq // CMP_STRIDE
    n_blk = -(-seq // SEL_LEN)
    r = NSA_HEADS * tq
    nbp = tt.shape[0]
    const = lambda shape: pl.BlockSpec(shape, lambda b, i: (0,) * len(shape))
    kern = functools.partial(_nsa_prompt_kernel, tq=tq, nkt=nkt, n_blk=n_blk)
    return pl.pallas_call(
        kern,
        grid=(batch, nq),
        in_specs=[
            pl.BlockSpec((None, NSA_WIDTH, tq), lambda b, i: (b, 0, i)),
            pl.BlockSpec((seq, 2 * LANES), lambda b, i: (b, 0)),
            pl.BlockSpec((None, 2 * LANES, seq), lambda b, i: (b, 0, 0)),
            pl.BlockSpec((nc, 4 * LANES), lambda b, i: (b, 0)),
            pl.BlockSpec((None, LANES, tq), lambda b, i: (b, 0, i)),
            const(kg.shape), const(seg1.shape), const(tt.shape),
        ],
        out_specs=pl.BlockSpec((tq, NSA_WIDTH), lambda b, i: (b * nq + i, 0)),
        out_shape=jax.ShapeDtypeStruct((batch * seq, NSA_WIDTH), BF16),
        scratch_shapes=[pltpu.VMEM((nbp, 2 * tq), F32), pltpu.VMEM((nbp, 2 * tq), F32),
                        pltpu.VMEM((1, r), F32), pltpu.VMEM((1, r), F32), pltpu.VMEM((LANES, r), F32)],
        compiler_params=_params(("arbitrary", "arbitrary")),
        name="nsa_prompt",
    )(qt, kb, vt, p, gt, kg, seg1, tt)


def _row_meta(tq, q0):
    r = NSA_HEADS * tq
    rid = lax.broadcasted_iota(jnp.int32, (r, 1), 0)
    hh = rid >> _log2(tq)
    ii = rid & (tq - 1)
    slope = lax.bitcast_convert_type((126 - hh) << 23, F32)
    qpos = q0 + ii
    return ii, slope, qpos


def _flash_init(m_ref, l_ref, acc_ref):
    m_ref[...] = jnp.full(m_ref.shape, NEG_INF, F32)
    l_ref[...] = jnp.zeros(l_ref.shape, F32)
    acc_ref[...] = jnp.zeros(acc_ref.shape, F32)


def _flash_update(m_ref, l_ref, acc_ref, s, pv):
    m_old = m_ref[...]
    m_new = jnp.maximum(m_old, jnp.max(s, axis=-1, keepdims=True))
    alpha = jnp.exp(m_old - m_new)
    p = jnp.exp(s - m_new)
    l_ref[...] = alpha * l_ref[...] + jnp.sum(p, axis=-1, keepdims=True)
    acc_ref[...] = alpha * acc_ref[...] + pv(p.astype(BF16))
    m_ref[...] = m_new


def _expand_mask(mask2b, blk0, nkt, tq):
    nbp = mask2b.shape[1]
    j_e = lax.broadcasted_iota(jnp.int32, (nbp, nkt), 0)
    c_e = lax.broadcasted_iota(jnp.int32, (nbp, nkt), 1)
    e = (j_e == blk0 + (c_e >> _log2(SEL_LEN))).astype(BF16)
    me2 = _dot(mask2b, e)
    hpg = NSA_HEADS // NSA_GROUPS
    return jnp.concatenate([me2[0:tq]] * hpg + [me2[tq:2 * tq]] * hpg, axis=0)


def _nsa_sample_kernel(pt_ref, q_ref, p_ref, rows_ref, wnew_ref, wst_ref, gates_ref, kg_ref, seg_ref, tmat_ref, cache_ref,
                       o_ref, buf_ref, sem_ref, m_ref, l_ref, acc_ref, *, tq, pages_per_tile, n_pages, page_base):
    b = pl.program_id(0)
    slot = b % 2
    past = n_pages * PAGE_SIZE
    wb = wst_ref.shape[1]
    hpg = NSA_HEADS // NSA_GROUPS
    copy = functools.partial(_page_copy, pt_ref, cache_ref, buf_ref, sem_ref, page_base=page_base, kind0=2)
    wait_all = _page_pipeline(copy, b, pl.num_programs(0), n_pages)

    r = NSA_HEADS * tq
    qp = jnp.concatenate([q_ref[:, LANES * h:LANES * (h + 1)] for h in range(NSA_HEADS)], axis=0).astype(BF16)
    ii, slope, qpos = _row_meta(tq, past)

    kc, vc = _compressed_kv(p_ref[...], kg_ref[...], seg_ref[...])
    nc = kc.shape[0]
    s = _dot_nt(qp, kc.astype(BF16))
    cstart = lax.broadcasted_iota(jnp.int32, (1, nc), 1) * CMP_STRIDE
    cdist = jnp.abs(qpos.astype(F32) - (cstart.astype(F32) + 0.5 * (CMP_LEN - 1)))
    c_ok = (cstart + (CMP_LEN - 1)) <= qpos
    s = jnp.where(c_ok, s - slope * cdist, NEG_INF)
    e = jnp.exp(s - jnp.max(s, axis=-1, keepdims=True))
    p_cmp = e / jnp.sum(e, axis=-1, keepdims=True) * (qpos >= CMP_LEN - 1).astype(F32)
    o_cmp = _dot(p_cmp.astype(BF16), vc.astype(BF16))

    imps = []
    for g in range(NSA_GROUPS):
        acc = p_cmp[(g * hpg) * tq:(g * hpg + 1) * tq]
        for h in range(1, hpg):
            acc = acc + p_cmp[(g * hpg + h) * tq:(g * hpg + h + 1) * tq]
        imps.append(acc)
    tmat = tmat_ref[...]
    blk = None
    for part in _split3(jnp.concatenate(imps, axis=0)):
        d = _dot(part, tmat)
        blk = d if blk is None else blk + d
    nbp = tmat.shape[1]
    n_blk = -(-(past + tq) // SEL_LEN)
    j_l = lax.broadcasted_iota(jnp.int32, (2 * tq, nbp), 1)
    qrow = lax.broadcasted_iota(jnp.int32, (2 * tq, nbp), 0) & (tq - 1)
    back = ((past + qrow) >> _log2(SEL_LEN)) - j_l
    forced = (j_l == 0) | ((back >= 0) & (back < N_LOCAL))
    score = jnp.where(back >= 0, blk + jnp.where(forced, FORCE_BONUS, 0.0), -1.0)
    cnt = jnp.zeros((2 * tq, nbp), F32)
    for i in range(n_blk):
        col = score[:, i:i + 1]
        cnt = cnt + jnp.where(j_l > i, (col >= score).astype(F32), (col > score).astype(F32))
    mask2 = (cnt < N_SEL).astype(F32)
    mask2b = mask2.astype(BF16)

    wait_all()

    _flash_init(m_ref, l_ref, acc_ref)
    nkt = pages_per_tile * PAGE_SIZE
    dist0 = (past + ii - lax.broadcasted_iota(jnp.int32, (r, nkt), 1)).astype(F32)

    def sel_body(kt, carry):
        k0 = kt * nkt
        tiles = []
        for i in range(pages_per_tile):
            r0 = pl.multiple_of((kt * pages_per_tile + i) * 2 * LANES, 2 * LANES)
            tiles.append((buf_ref[slot, pl.ds(r0, LANES), :].astype(BF16), buf_ref[slot, pl.ds(r0 + LANES, LANES), :].astype(BF16)))
        s = jnp.concatenate([_dot(qp, kt_i) for kt_i, _ in tiles], axis=1)
        me = _expand_mask(mask2b, k0 // SEL_LEN, nkt, tq)
        s = jnp.where(me > 0.5, s - slope * (dist0 - jnp.asarray(k0, F32)), NEG_INF)

        def pv(pb):
            out = None
            for i, (_, vt_i) in enumerate(tiles):
                d = _dot_nt(pb[:, PAGE_SIZE * i:PAGE_SIZE * (i + 1)], vt_i)
                out = d if out is None else out + d
            return out
        _flash_update(m_ref, l_ref, acc_ref, s, pv)
        return carry
    lax.fori_loop(0, n_pages // pages_per_tile, sel_body, 0)

    zpad = jnp.zeros((LANES - tq, LANES), F32)
    dist_new = (ii - lax.broadcasted_iota(jnp.int32, (r, LANES), 1)).astype(F32)
    new_blk = past // SEL_LEN
    me_new = jnp.concatenate([mask2[0:tq, new_blk:new_blk + 1]] * hpg + [mask2[tq:2 * tq, new_blk:new_blk + 1]] * hpg, axis=0)
    k_new = jnp.concatenate([rows_ref[:, 2 * LANES:3 * LANES], zpad], axis=0).astype(BF16)
    v_new = jnp.concatenate([rows_ref[:, 3 * LANES:4 * LANES], zpad], axis=0).astype(BF16)
    ok = (me_new > 0.5) & (dist_new >= 0.0)
    s = jnp.where(ok, _dot_nt(qp, k_new) - slope * dist_new, NEG_INF)
    _flash_update(m_ref, l_ref, acc_ref, s, lambda pb: _dot(pb, v_new))
    o_sel = acc_ref[...] / l_ref[...]

    _flash_init(m_ref, l_ref, acc_ref)
    dist_w = (ii + wb - lax.broadcasted_iota(jnp.int32, (r, wb), 1)).astype(F32)
    ok = (dist_w >= 0.0) & (dist_w < float(WINDOW))
    s = jnp.where(ok, _dot(qp, wst_ref[0:LANES, :].astype(BF16)) - slope * dist_w, NEG_INF)
    vwt = wst_ref[LANES:2 * LANES, :].astype(BF16)
    _flash_update(m_ref, l_ref, acc_ref, s, lambda pb: _dot_nt(pb, vwt))
    wnew = wnew_ref[...]
    kw_new = jnp.concatenate([wnew[:, 0:LANES], zpad], axis=0).astype(BF16)
    vw_new = jnp.concatenate([wnew[:, LANES:2 * LANES], zpad], axis=0).astype(BF16)
    s = jnp.where(dist_new >= 0.0, _dot_nt(qp, kw_new) - slope * dist_new, NEG_INF)
    _flash_update(m_ref, l_ref, acc_ref, s, lambda pb: _dot(pb, vw_new))
    o_win = acc_ref[...] / l_ref[...]

    gates = gates_ref[...]
    lane = lax.broadcasted_iota(jnp.int32, (tq, LANES), 1)
    outs = []
    for h in range(NSA_HEADS):
        rs = slice(h * tq, (h + 1) * tq)
        outs.append(gates[:, h:h + 1] * o_cmp[rs] + gates[:, NSA_HEADS + h:NSA_HEADS + h + 1] * o_sel[rs]
                    + gates[:, 2 * NSA_HEADS + h:2 * NSA_HEADS + h + 1] * o_win[rs])
    for c in range(4):
        o_ref[:, LANES * c:LANES * (c + 1)] = jnp.where(lane < HEAD_DIM, outs[c], outs[4 + c])


def _nsa_sample(page_table, q, p, rows, wnew, wstate_t, gates, kg, seg1, tmat, cache_t, *, tq, pages_per_tile, page_base, wstate_base):
    db, n_pages = page_table.shape
    past = n_pages * PAGE_SIZE
    nc = past // CMP_STRIDE
    wb = wstate_t.shape[1]
    r = NSA_HEADS * tq
    const = lambda shape: pl.BlockSpec(shape, lambda b, pt: (0,) * len(shape))
    kern = functools.partial(_nsa_sample_kernel, tq=tq, pages_per_tile=pages_per_tile, n_pages=n_pages, page_base=page_base)
    return pl.pallas_call(
        kern,
        grid_spec=pltpu.PrefetchScalarGridSpec(
            num_scalar_prefetch=1,
            grid=(db,),
            in_specs=[
                pl.BlockSpec((tq, 2 * NSA_WIDTH), lambda b, pt: (b, 0)),
                pl.BlockSpec((nc, 4 * LANES), lambda b, pt: (b, 0)),
                pl.BlockSpec((tq, KV_COLS), lambda b, pt: (b, 0)),
                pl.BlockSpec((tq, 2 * LANES), lambda b, pt: (b, 0)),
                pl.BlockSpec((2 * LANES, wb), lambda b, pt: (wstate_base + b, 0)),
                pl.BlockSpec((tq, LANES), lambda b, pt: (b, 0)),
                const(kg.shape), const(seg1.shape), const(tmat.shape),
                pl.BlockSpec(memory_space=pl.ANY),
            ],
            out_specs=pl.BlockSpec((tq, NSA_WIDTH), lambda b, pt: (b, 0)),
            scratch_shapes=[pltpu.VMEM((2, n_pages * 2 * LANES, LANES), F32), pltpu.SemaphoreType.DMA((2,)),
                            pltpu.VMEM((r, 1), F32), pltpu.VMEM((r, 1), F32), pltpu.VMEM((r, LANES), F32)],
        ),
        out_shape=jax.ShapeDtypeStruct((db * tq, NSA_WIDTH), F32),
        compiler_params=_params(("arbitrary",)),
        name="nsa_sample",
    )(page_table, q, p, rows, wnew, wstate_t, gates, kg, seg1, tmat, cache_t)


def _hgrn_consts(tc):
    nl = int(np.log2(tc))
    t = np.arange(tc)[:, None]
    u = np.arange(tc)[None, :]
    blocks = [(u <= t), (u > t)]
    masks = [np.eye(tc, dtype=bool)]
    for lv in range(nl):
        hs = 1 << lv
        mid = (t // (2 * hs)) * 2 * hs + hs
        ref = mid - 1
        blocks.append((t >= mid) & (u > ref) & (u <= t))
        blocks.append((t < mid) & (u > t) & (u <= ref))
        masks.append((t // (2 * hs) == u // (2 * hs)) & (t % (2 * hs) >= hs) & (u % (2 * hs) < hs))
    cm = np.concatenate(blocks, axis=0).astype(np.float32)
    mk = np.concatenate(masks, axis=0).astype(np.float32)
    return jnp.asarray(cm, BF16), jnp.asarray(mk, F32), nl


def _hgrn_kernel(cm_ref, mk_ref, hq_ref, hlf_ref, hk_ref, hv_ref, hg_ref, og_ref, s0_ref, o_ref, sout_ref, st_ref, *, tc, nl):
    t = pl.program_id(1)
    nt = pl.num_programs(1)

    @pl.when(t == 0)
    def _():
        for h in range(HGRN_HEADS):
            st_ref[h] = s0_ref[h].T

    cm = cm_ref[...]
    for h in range(HGRN_HEADS):
        sl = slice(HGRN_DK * h, HGRN_DK * (h + 1))
        hi, lo = _split2(hlf_ref[:, sl])
        ee = _dot(cm, jnp.concatenate([hi, lo], axis=1))
        x = jnp.exp(ee[:, 0:LANES] + ee[:, LANES:2 * LANES])
        q = hq_ref[:, sl]
        k = hk_ref[:, sl]
        v = hv_ref[:, sl].astype(BF16)
        xb = x[0:tc]
        xs = x[tc:2 * tc]
        a = mk_ref[0:tc, :] * _dot_nt(q.astype(BF16), k.astype(BF16))
        for lv in range(nl):
            xu = x[(2 + 2 * lv) * tc:(3 + 2 * lv) * tc]
            xl = x[(3 + 2 * lv) * tc:(4 + 2 * lv) * tc]
            a = a + mk_ref[(1 + lv) * tc:(2 + lv) * tc, :] * _dot_nt((q * xu).astype(BF16), (k * xl).astype(BF16))
        st = st_ref[h]
        o = _dot(a.astype(BF16), v) + _dot_nt((q * xb).astype(BF16), st.astype(BF16))
        st_new = st * xb[tc - 1:tc, :] + _dot_tn(v, (k * xs).astype(BF16))
        st_ref[h] = st_new
        on = o * lax.rsqrt(jnp.mean(o * o, axis=-1, keepdims=True) + EPS) * og_ref[...]
        o_ref[:, sl] = (on * hg_ref[:, sl]).astype(o_ref.dtype)

        @pl.when(t == nt - 1)
        def _():
            sout_ref[h] = st_new.T


def _hgrn(hq, hlf, hk, hv, hg, og, s0, *, batch, seq, tc, s0_base):
    cm, mk, nl = _hgrn_consts(tc)
    nt = seq // tc
    const = lambda shape: pl.BlockSpec(shape, lambda b, t: (0,) * len(shape))
    row = pl.BlockSpec((tc, HGRN_WIDTH), lambda b, t: (b * nt + t, 0))
    kern = functools.partial(_hgrn_kernel, tc=tc, nl=nl)
    return pl.pallas_call(
        kern,
        grid=(batch, nt),
        in_specs=[const(cm.shape), const(mk.shape), row, row, row, row, row, const(og.shape),
                  pl.BlockSpec((None, HGRN_HEADS, HGRN_DK, HGRN_DV), lambda b, t: (s0_base + b, 0, 0, 0))],
        out_specs=[row, pl.BlockSpec((None, HGRN_HEADS, HGRN_DK, HGRN_DV), lambda b, t: (b, 0, 0, 0))],
        out_shape=[jax.ShapeDtypeStruct((batch * seq, HGRN_WIDTH), BF16),
                   jax.ShapeDtypeStruct((batch, HGRN_HEADS, HGRN_DK, HGRN_DV), F32)],
        scratch_shapes=[pltpu.VMEM((HGRN_HEADS, HGRN_DV, HGRN_DK), F32)],
        compiler_params=_params(("arbitrary", "arbitrary")),
        name="hgrn",
    )(cm, mk, hq, hlf, hk, hv, hg, og, s0)


def _outproj_kernel(*refs, moe):
    if moe:
        x_ref, on_ref, oh_ref, w_ref, g_ref, r_ref, xo_ref, h_ref, gate_ref = refs
    else:
        x_ref, on_ref, oh_ref, w_ref, g_ref, xo_ref, h_ref = refs
    xn = x_ref[...] + _dot(on_ref[...], w_ref[0:NSA_WIDTH, :]) + _dot(oh_ref[...], w_ref[NSA_WIDTH:NSA_WIDTH + HGRN_WIDTH, :])
    xo_ref[...] = xn
    h = xn * lax.rsqrt(jnp.mean(xn * xn, axis=-1, keepdims=True) + EPS) * g_ref[...]
    h_ref[...] = h.astype(BF16)
    if moe:
        logits = None
        rparts = _split3(r_ref[...])
        hparts = _split3(h)
        for i in range(3):
            for j in range(3 - i):
                d = _dot(hparts[i], rparts[j])
                logits = d if logits is None else logits + d
        lane = lax.broadcasted_iota(jnp.int32, logits.shape, 1).astype(F32)
        lg = jnp.where(lane < N_EXPERTS, logits, NEG_INF)
        m1 = jnp.max(lg, axis=-1, keepdims=True)
        i1 = jnp.min(jnp.where(lg == m1, lane, float(LANES)), axis=-1, keepdims=True)
        lg2 = jnp.where(lane == i1, NEG_INF, lg)
        m2 = jnp.max(lg2, axis=-1, keepdims=True)
        i2 = jnp.min(jnp.where(lg2 == m2, lane, float(LANES)), axis=-1, keepdims=True)
        e2 = jnp.exp(m2 - m1)
        den = 1.0 + e2
        gate_ref[...] = jnp.where(lane == i1, 1.0 / den, 0.0) + jnp.where(lane == i2, e2 / den, 0.0)


def _outproj(x, o_nsa, o_hg, w, g, router, tm):
    t, d = x.shape
    moe = router is not None
    const = lambda shape: pl.BlockSpec(shape, lambda i: (0,) * len(shape))
    row = lambda n: pl.BlockSpec((tm, n), lambda i: (i, 0))
    in_specs = [row(d), row(NSA_WIDTH), row(HGRN_WIDTH), const(w.shape), const((1, d))]
    args = [x, o_nsa, o_hg, w, g]
    out_specs = [row(d), row(d)]
    out_shape = [jax.ShapeDtypeStruct((t, d), F32), jax.ShapeDtypeStruct((t, d), BF16)]
    if moe:
        in_specs.append(const(router.shape))
        args.append(router)
        out_specs.append(row(LANES))
        out_shape.append(jax.ShapeDtypeStruct((t, LANES), F32))
    return pl.pallas_call(
        functools.partial(_outproj_kernel, moe=moe),
        grid=(t // tm,),
        in_specs=in_specs, out_specs=out_specs, out_shape=out_shape,
        compiler_params=_params(("arbitrary",)),
        name="outproj_moe" if moe else "outproj",
    )(*args)


def _ffn_kernel(x_ref, h_ref, wg_ref, wu_ref, wd_ref, o_ref):
    f = pl.program_id(1)
    h = h_ref[...]
    a = _dot(h, wg_ref[...])
    b = _dot(h, wu_ref[...])
    d = _dot((a * _sigmoid(a) * b).astype(BF16), wd_ref[...])

    @pl.when(f == 0)
    def _():
        o_ref[...] = x_ref[...] + d

    @pl.when(f > 0)
    def _():
        o_ref[...] = o_ref[...] + d


def _ffn(x, h, wg, wu, wd, tm, tf):
    t, d = x.shape
    ff = wg.shape[1]
    return pl.pallas_call(
        _ffn_kernel,
        grid=(t // tm, ff // tf),
        in_specs=[pl.BlockSpec((tm, d), lambda i, f: (i, 0)), pl.BlockSpec((tm, d), lambda i, f: (i, 0)),
                  pl.BlockSpec((d, tf), lambda i, f: (0, f)), pl.BlockSpec((d, tf), lambda i, f: (0, f)),
                  pl.BlockSpec((tf, d), lambda i, f: (f, 0))],
        out_specs=pl.BlockSpec((tm, d), lambda i, f: (i, 0)),
        out_shape=jax.ShapeDtypeStruct((t, d), F32),
        compiler_params=_params(("arbitrary", "arbitrary")),
        name="ffn",
    )(x, h, wg, wu, wd)


def _moe_kernel(x_ref, h_ref, gate_ref, wg_ref, wu_ref, wd_ref, o_ref):
    e = pl.program_id(1)
    f = pl.program_id(2)
    h = h_ref[...]
    a = _dot(h, wg_ref[...])
    b = _dot(h, wu_ref[...])
    d = _dot((a * _sigmoid(a) * b).astype(BF16), wd_ref[...])
    gate = gate_ref[...]
    lane = lax.broadcasted_iota(jnp.int32, gate.shape, 1)
    gcol = jnp.sum(jnp.where(lane == e, gate, 0.0), axis=-1, keepdims=True)
    first = (e == 0) & (f == 0)

    @pl.when(first)
    def _():
        o_ref[...] = x_ref[...] + gcol * d

    @pl.when(jnp.logical_not(first))
    def _():
        o_ref[...] = o_ref[...] + gcol * d


def _moe(x, h, gate, wg, wu, wd, tm, tf):
    t, d = x.shape
    ne, _, ff = wg.shape
    return pl.pallas_call(
        _moe_kernel,
        grid=(t // tm, ne, ff // tf),
        in_specs=[pl.BlockSpec((tm, d), lambda i, e, f: (i, 0)), pl.BlockSpec((tm, d), lambda i, e, f: (i, 0)),
                  pl.BlockSpec((tm, LANES), lambda i, e, f: (i, 0)),
                  pl.BlockSpec((None, d, tf), lambda i, e, f: (e, 0, f)), pl.BlockSpec((None, d, tf), lambda i, e, f: (e, 0, f)),
                  pl.BlockSpec((None, tf, d), lambda i, e, f: (e, f, 0))],
        out_specs=pl.BlockSpec((tm, d), lambda i, e, f: (i, 0)),
        out_shape=jax.ShapeDtypeStruct((t, d), F32),
        compiler_params=_params(("arbitrary", "arbitrary", "arbitrary")),
        name="moe",
    )(x, h, gate, wg, wu, wd)


def _head_perm():
    idx = []
    for c in range(4):
        idx += list(range(HEAD_DIM * c, HEAD_DIM * (c + 1))) + list(range(HEAD_DIM * (4 + c), HEAD_DIM * (5 + c)))
    return np.asarray(idx, np.int32)


def _tap_matrix(n_cmp, nc_pad, n_blk, nb_pad):
    r_s = SEL_LEN // CMP_STRIDE
    r_c = CMP_LEN // CMP_STRIDE
    taps = np.convolve(np.ones(r_s), np.ones(r_c)) / r_c
    tm = np.zeros((nc_pad, nb_pad), np.float32)
    for j in range(n_blk):
        for kk, w in enumerate(taps):
            n = j * r_s + kk - (r_c - 1)
            if 0 <= n < n_cmp:
                tm[n, j] = w
    return tm


def _layer_weights(l, w_in, q_gain, k_gain, cmp_pe, cmp_w, w_out):
    perm = _head_perm()
    wl = w_in[l]
    q_end = NSA_WIDTH
    kv_end = q_end + 6 * NSA_GROUPS * HEAD_DIM
    gate_end = kv_end + 3 * NSA_HEADS
    d = wl.shape[0]
    w_pad = jnp.concatenate([wl[:, :q_end][:, perm], wl[:, q_end:kv_end], wl[:, kv_end:gate_end],
                             jnp.zeros((d, LANES - 3 * NSA_HEADS), wl.dtype), wl[:, gate_end:]], axis=1).astype(BF16)
    qg = jnp.tile(q_gain[l], NSA_HEADS)[None, :]
    kg_proj = jnp.stack([jnp.tile(k_gain[l, 1], NSA_GROUPS), jnp.tile(k_gain[l, 2], NSA_GROUPS)])
    kg_cmp = jnp.tile(k_gain[l, 0], NSA_GROUPS)[None, :]
    pairs = CMP_STRIDE // 2
    cw = cmp_w[l].reshape(2, 2, pairs, 2, HEAD_DIM, HEAD_DIM)
    eye = jnp.eye(NSA_GROUPS, dtype=cw.dtype)
    w_cmp = jnp.einsum('krjsde,gh->kjsgdrhe', cw, eye).reshape(2 * pairs * 2 * LANES, 2 * LANES).astype(BF16)
    pe = cmp_pe[l].reshape(2, 2, pairs, 2, 1, HEAD_DIM)
    pe = jnp.broadcast_to(pe, (2, 2, pairs, 2, NSA_GROUPS, HEAD_DIM)).reshape(4 * pairs, 2 * LANES)
    wo = w_out[l]
    wo = jnp.concatenate([wo[:NSA_WIDTH][perm], wo[NSA_WIDTH:]], axis=0).astype(BF16)
    return w_pad, qg, kg_proj, kg_cmp, w_cmp, pe, wo


def _token_minor(a):
    n = a.ndim
    return jnp.transpose(a, tuple(range(n - 4)) + (n - 3, n - 2, n - 1, n - 4))


def _token_major(a):
    n = a.ndim
    return jnp.transpose(a, tuple(range(n - 4)) + (n - 1, n - 4, n - 3, n - 2))


def kernel(x_prompt, x_sample, cache_kv, state_win_kv, state_hgrn, page_table, norm_mix, norm_ffn, w_in, q_gain, k_gain, cmp_pe, cmp_w, hgrn_lb_logits, hgrn_o_gain, w_out, ffn_w_gate, ffn_w_up, ffn_w_down, moe_router, moe_w_gate, moe_w_up, moe_w_down):
    depth = w_in.shape[0]
    batch, seq, d_model = x_prompt.shape
    db, ds, _ = x_sample.shape
    n_pool = cache_kv.shape[1]
    n_pages = page_table.shape[1]
    past = n_pages * PAGE_SIZE
    wb = state_win_kv.shape[2]
    assert wb == WINDOW and seq % 256 == 0 and seq >= WINDOW and ds == 8 and n_pages % 4 == 0

    tq, nkt_p, tc = 128, 256, 128
    tm_p = 256
    tm_f = 512 if (batch * seq) % 512 == 0 else 256
    ts = db * ds
    d_ff = ffn_w_gate.shape[-1]
    tf = d_ff // 2

    sm = jax.nn.softmax(hgrn_lb_logits.astype(F32), axis=0)
    lower = jnp.concatenate([jnp.zeros_like(sm[:1]), jnp.cumsum(sm[1:], axis=0)], axis=0)
    seg = jnp.asarray(np.kron(np.eye(NSA_HEADS), np.ones((HEAD_DIM, HEAD_DIM))), BF16)
    seg1 = seg[:LANES, :LANES]

    nc_p = seq // CMP_STRIDE
    nb_p = -(-seq // SEL_LEN)
    tt_p = jnp.asarray(_tap_matrix(nc_p - 1, nc_p, nb_p, -(-nb_p // 8) * 8).T, BF16)
    nc_s = past // CMP_STRIDE
    nb_s = -(-(past + ds) // SEL_LEN)
    nbp_s = -(-nb_s // LANES) * LANES
    tm_s = jnp.asarray(_tap_matrix(nc_s - 1, nc_s, nb_s, nbp_s), BF16)

    cache_t = _token_minor(cache_kv).reshape(depth * n_pool * KV_COLS, PAGE_SIZE)
    wstate_t = _token_minor(state_win_kv).reshape(depth * db * 2 * LANES, wb)
    hstate = state_hgrn.reshape(depth * db, HGRN_HEADS, HGRN_DK, HGRN_DV)
    zero_state = jnp.zeros((batch, HGRN_HEADS, HGRN_DK, HGRN_DV), F32)

    xp = x_prompt.reshape(batch * seq, d_model)
    xs = x_sample.reshape(ts, d_model)
    kv_p, kv_s, win_p, win_s, hs_p, hs_s = [], [], [], [], [], []
    for l in range(depth):
        w_pad, qg, kg_proj, kg_cmp, w_cmp, pe, wo = _layer_weights(l, w_in, q_gain, k_gain, cmp_pe, cmp_w, w_out)
        lb = lower[l]
        hp = jnp.stack([jnp.log(lb), jnp.log1p(-lb), 1.0 - lb])
        og = hgrn_o_gain[l][None, :]
        g1 = norm_mix[l][None, :]
        g2 = norm_ffn[l][None, :]
        i = l // 2
        if l % 2 == 0:
            router = None
            wg, wu, wd = ffn_w_gate[i].astype(BF16), ffn_w_up[i].astype(BF16), ffn_w_down[i].astype(BF16)
        else:
            router = jnp.pad(moe_router[i], ((0, 0), (0, LANES - N_EXPERTS)))
            wg, wu, wd = moe_w_gate[i].astype(BF16), moe_w_up[i].astype(BF16), moe_w_down[i].astype(BF16)

        def mixer(x, h, gate):
            tm = tm_f if x.shape[0] % tm_f == 0 else x.shape[0]
            if router is None:
                return _ffn(x, h, wg, wu, wd, tm, tf)
            return _moe(x, h, gate, wg, wu, wd, tm, tf)

        qt, rows_t, win_t, kb, vt, kc, vc, gt, hq, hlf, hk, hv, hg = _proj_prompt(xp, g1, w_pad, qg, kg_proj, seg, hp, batch=batch, seq=seq, tm=tm_p)
        pp = _compress_prompt(kc, vc, pe, w_cmp, batch=batch, seq=seq)
        o_nsa = _nsa_prompt(qt, kb, vt, pp, gt, kg_cmp, seg1, tt_p, batch=batch, seq=seq, tq=tq, nkt=nkt_p)
        o_hg, s_fin = _hgrn(hq, hlf, hk, hv, hg, og, zero_state, batch=batch, seq=seq, tc=tc, s0_base=0)
        res = _outproj(xp, o_nsa, o_hg, wo, g2, router, tm_p)
        xp = mixer(res[0], res[1], res[2] if router is not None else None)
        kv_p.append(rows_t.reshape(batch, 4, NSA_GROUPS, HEAD_DIM, seq))
        win_p.append(win_t.reshape(batch, 2, NSA_GROUPS, HEAD_DIM, seq)[..., seq - min(WINDOW, seq):])
        hs_p.append(s_fin)

        q, rows, win, gates, hq, hlf, hk, hv, hg = _proj_sample(xs, g1, w_pad, qg, kg_proj, seg, hp)
        ps = _compress_sample(page_table, cache_t, pe, w_cmp, page_base=l * n_pool)
        o_nsa = _nsa_sample(page_table, q, ps, rows, win, wstate_t, gates, kg_cmp, seg1, tm_s, cache_t,
                            tq=ds, pages_per_tile=4, page_base=l * n_pool, wstate_base=l * db)
        padt = lambda a: jnp.pad(a.reshape(db, ds, HGRN_WIDTH), ((0, 0), (0, tc - ds), (0, 0))).reshape(db * tc, HGRN_WIDTH)
        o_hg, s_new = _hgrn(padt(hq), padt(hlf), padt(hk), padt(hv), padt(hg), og, hstate, batch=db, seq=tc, tc=tc, s0_base=l * db)
        o_hg = o_hg.reshape(db, tc, HGRN_WIDTH)[:, :ds].reshape(ts, HGRN_WIDTH)
        res = _outproj(xs, o_nsa.astype(BF16), o_hg, wo, g2, router, ts)
        xs = mixer(res[0], res[1], res[2] if router is not None else None)
        kv_s.append(rows.reshape(db, ds, 4, NSA_GROUPS, HEAD_DIM))
        win_s.append(win.reshape(db, ds, 2, NSA_GROUPS, HEAD_DIM))
        hs_s.append(s_new)

    new_win_sample = jnp.concatenate([state_win_kv[:, :, ds:], jnp.stack(win_s)], axis=2)
    return (xp.reshape(batch, seq, d_model), xs.reshape(db, ds, d_model), _token_major(jnp.stack(kv_p)), jnp.stack(kv_s),
            _token_major(jnp.stack(win_p)), new_win_sample, jnp.stack(hs_p), jnp.stack(hs_s))
```

```python
import functools

import numpy as np
import jax
import jax.numpy as jnp
from jax import lax
from jax.experimental import pallas as pl
from jax.experimental.pallas import tpu as pltpu

F32 = jnp.float32
BF16 = jnp.bfloat16

NSA_HEADS = 8
NSA_GROUPS = 2
HEAD_DIM = 64
NSA_WIDTH = NSA_HEADS * HEAD_DIM
CMP_LEN = 32
CMP_STRIDE = 16
SEL_LEN = 64
N_SEL = 16
N_LOCAL = 2
WINDOW = 512
FORCE_BONUS = 1e4
HGRN_HEADS = 4
HGRN_DK = 128
HGRN_DV = 128
HGRN_WIDTH = HGRN_HEADS * HGRN_DV
N_EXPERTS = 8
EPS = 1e-6
NEG_INF = -1e30
PAGE_SIZE = 128

LANES = 128
KV_COLS = 4 * NSA_GROUPS * HEAD_DIM
CMP_PITCH = CMP_STRIDE + 1
VMEM_LIMIT = 48 * 1024 * 1024


def _log2(n):
    assert n & (n - 1) == 0
    return n.bit_length() - 1


def _dot(a, b):
    return jnp.dot(a, b, preferred_element_type=F32)


def _dot_nt(a, b):
    return lax.dot_general(a, b, (((1,), (1,)), ((), ())), preferred_element_type=F32)


def _dot_tn(a, b):
    return lax.dot_general(a, b, (((0,), (0,)), ((), ())), preferred_element_type=F32)


def _split2(x):
    hi = x.astype(BF16)
    lo = (x - hi.astype(F32)).astype(BF16)
    return hi, lo


def _split3(x):
    hi = x.astype(BF16)
    r = x - hi.astype(F32)
    mid = r.astype(BF16)
    lo = (r - mid.astype(F32)).astype(BF16)
    return hi, mid, lo


def _segsum(x, seg):
    hi, lo = _split2(x)
    return _dot(hi, seg) + _dot(lo, seg)


def _seg_rms(z, gain, seg):
    ss = _segsum(z * z, seg) * (1.0 / HEAD_DIM)
    return z * lax.rsqrt(ss + EPS) * gain


def _sigmoid(x):
    return 1.0 / (1.0 + jnp.exp(-x))


def _params(sem, vmem=VMEM_LIMIT):
    return pltpu.CompilerParams(dimension_semantics=sem, vmem_limit_bytes=vmem)


def _proj_core(x_ref, gain_ref, w_ref, qg_ref, kg_ref, seg_ref, hp_ref, hq_ref, hlf_ref, hk_ref, hv_ref, hg_ref):
    x = x_ref[...]
    ms = jnp.mean(x * x, axis=-1, keepdims=True)
    h = (x * lax.rsqrt(ms + EPS) * gain_ref[...]).astype(BF16)

    def mm(c0, c1):
        return _dot(h, w_ref[:, c0:c1])

    seg1 = seg_ref[0:LANES, 0:LANES]
    qn = _seg_rms(mm(0, NSA_WIDTH), qg_ref[...], seg_ref[...]) * (HEAD_DIM ** -0.5)
    c0 = NSA_WIDTH
    kv = mm(c0, c0 + 6 * LANES)
    kvs = [kv[:, LANES * j:LANES * (j + 1)] for j in range(6)]
    kvs[2] = _seg_rms(kvs[2], kg_ref[0:1, :], seg1)
    kvs[4] = _seg_rms(kvs[4], kg_ref[1:2, :], seg1)
    c0 += 6 * LANES
    gates = _sigmoid(mm(c0, c0 + LANES))
    c0 += LANES
    zq = mm(c0, c0 + HGRN_WIDTH)
    hq_ref[...] = zq * _sigmoid(zq)
    c0 += HGRN_WIDTH
    zf = mm(c0, c0 + HGRN_WIDTH)
    log_sig = jnp.minimum(zf, 0.0) - jnp.log1p(jnp.exp(-jnp.abs(zf)))
    a = hp_ref[0:1, :]
    c = hp_ref[1:2, :] + log_sig
    hlf_ref[...] = jnp.maximum(a, c) + jnp.log1p(jnp.exp(-jnp.abs(a - c)))
    hk_ref[...] = hp_ref[2:3, :] * _sigmoid(-zf)
    c0 += HGRN_WIDTH
    hv_ref[...] = mm(c0, c0 + HGRN_WIDTH)
    c0 += HGRN_WIDTH
    zg = mm(c0, c0 + HGRN_WIDTH)
    hg_ref[...] = zg * _sigmoid(zg)
    return qn, kvs, gates


def _proj_prompt_kernel(x_ref, gain_ref, w_ref, qg_ref, kg_ref, seg_ref, hp_ref,
                        qt_ref, rowst_ref, wint_ref, kb_ref, vt_ref, kc_ref, vc_ref, gt_ref,
                        hq_ref, hlf_ref, hk_ref, hv_ref, hg_ref):
    qn, kvs, gates = _proj_core(x_ref, gain_ref, w_ref, qg_ref, kg_ref, seg_ref, hp_ref, hq_ref, hlf_ref, hk_ref, hv_ref, hg_ref)
    for c in range(4):
        qt_ref[LANES * c:LANES * (c + 1), :] = qn[:, LANES * c:LANES * (c + 1)].T.astype(BF16)
    kvt = [a.T for a in kvs]
    for j in range(4):
        rowst_ref[LANES * j:LANES * (j + 1), :] = kvt[j]
    wint_ref[0:LANES, :] = kvt[4]
    wint_ref[LANES:2 * LANES, :] = kvt[5]
    kb_ref[:, 0:LANES] = kvs[2].astype(BF16)
    kb_ref[:, LANES:2 * LANES] = kvs[4].astype(BF16)
    vt_ref[0:LANES, :] = kvt[3].astype(BF16)
    vt_ref[LANES:2 * LANES, :] = kvt[5].astype(BF16)
    kc_ref[...] = kvs[0]
    vc_ref[...] = kvs[1]
    gt_ref[...] = gates.T


def _proj_sample_kernel(x_ref, gain_ref, w_ref, qg_ref, kg_ref, seg_ref, hp_ref,
                        q_ref, rows_ref, win_ref, gates_ref, hq_ref, hlf_ref, hk_ref, hv_ref, hg_ref):
    qn, kvs, gates = _proj_core(x_ref, gain_ref, w_ref, qg_ref, kg_ref, seg_ref, hp_ref, hq_ref, hlf_ref, hk_ref, hv_ref, hg_ref)
    lane = lax.broadcasted_iota(jnp.int32, (qn.shape[0], LANES), 1)
    for c in range(4):
        blk = qn[:, LANES * c:LANES * (c + 1)]
        q_ref[:, LANES * c:LANES * (c + 1)] = jnp.where(lane < HEAD_DIM, blk, 0.0)
        q_ref[:, LANES * (4 + c):LANES * (5 + c)] = jnp.where(lane >= HEAD_DIM, blk, 0.0)
    for j in range(4):
        rows_ref[:, LANES * j:LANES * (j + 1)] = kvs[j]
    win_ref[:, 0:LANES] = kvs[4]
    win_ref[:, LANES:2 * LANES] = kvs[5]
    gates_ref[...] = gates


def _proj_prompt(x, gain, w, qg, kg, seg, hp, *, batch, seq, tm):
    t, d = x.shape
    nt = seq // tm
    const = lambda shape: pl.BlockSpec(shape, lambda b, i: (0,) * len(shape))
    row = lambda n: pl.BlockSpec((tm, n), lambda b, i: (b * nt + i, 0))
    colt = lambda n: pl.BlockSpec((None, n, tm), lambda b, i: (b, 0, i))
    tshape = lambda n, dt: jax.ShapeDtypeStruct((batch, n, seq), dt)
    rshape = lambda n, dt: jax.ShapeDtypeStruct((t, n), dt)
    return pl.pallas_call(
        _proj_prompt_kernel,
        grid=(batch, nt),
        in_specs=[row(d), const((1, d)), const(w.shape), const(qg.shape), const(kg.shape), const(seg.shape), const(hp.shape)],
        out_specs=[colt(NSA_WIDTH), colt(KV_COLS), colt(2 * LANES), row(2 * LANES), colt(2 * LANES), row(LANES), row(LANES),
                   colt(LANES)] + [row(HGRN_WIDTH)] * 5,
        out_shape=[tshape(NSA_WIDTH, BF16), tshape(KV_COLS, F32), tshape(2 * LANES, F32), rshape(2 * LANES, BF16),
                   tshape(2 * LANES, BF16), rshape(LANES, F32), rshape(LANES, F32), tshape(LANES, F32)]
                  + [rshape(HGRN_WIDTH, F32)] * 5,
        compiler_params=_params(("arbitrary", "arbitrary")),
        name="proj_prompt",
    )(x, gain, w, qg, kg, seg, hp)


def _proj_sample(x, gain, w, qg, kg, seg, hp):
    t, d = x.shape
    const = lambda shape: pl.BlockSpec(shape, lambda i: (0,) * len(shape))
    row = lambda n: pl.BlockSpec((t, n), lambda i: (0, 0))
    outs = [2 * NSA_WIDTH, KV_COLS, 2 * LANES, LANES] + [HGRN_WIDTH] * 5
    return pl.pallas_call(
        _proj_sample_kernel,
        grid=(1,),
        in_specs=[row(d), const((1, d)), const(w.shape), const(qg.shape), const(kg.shape), const(seg.shape), const(hp.shape)],
        out_specs=[row(n) for n in outs],
        out_shape=[jax.ShapeDtypeStruct((t, n), F32) for n in outs],
        compiler_params=_params(("arbitrary",)),
        name="proj_sample",
    )(x, gain, w, qg, kg, seg, hp)


def _compress_rows(k_ref, v_ref, pe_ref, w_ref, out_ref, m, pitch=CMP_STRIDE):
    pairs = CMP_STRIDE // 2
    for kind, ref in ((0, k_ref), (1, v_ref)):
        acc = None
        bias = None
        for j in range(pairs):
            w = w_ref[(kind * pairs + j) * 2 * LANES:(kind * pairs + j + 1) * 2 * LANES, :]
            xa = ref[pl.ds(2 * j, m, stride=pitch), :]
            xb = ref[pl.ds(2 * j + 1, m, stride=pitch), :]
            d = _dot(jnp.concatenate([xa, xb], axis=1).astype(BF16), w)
            acc = d if acc is None else acc + d
            pe0 = pe_ref[(kind * 2) * pairs + j:(kind * 2) * pairs + j + 1, :]
            pe1 = pe_ref[(kind * 2 + 1) * pairs + j:(kind * 2 + 1) * pairs + j + 1, :]
            pel = jnp.concatenate([jnp.broadcast_to(pe0, (8, 2 * LANES)), jnp.broadcast_to(pe1, (8, 2 * LANES))], axis=0)
            pb = _dot(pel.astype(BF16), w)
            bias = pb if bias is None else bias + pb
        out_ref[:, kind * 2 * LANES:kind * 2 * LANES + LANES] = acc[:, 0:LANES] + bias[0:1, 0:LANES]
        out_ref[:, kind * 2 * LANES + LANES:(kind + 1) * 2 * LANES] = acc[:, LANES:2 * LANES] + bias[8:9, LANES:2 * LANES]


def _compress_prompt_kernel(k_ref, v_ref, pe_ref, w_ref, out_ref, *, m):
    _compress_rows(k_ref, v_ref, pe_ref, w_ref, out_ref, m)


def _compress_prompt(kc, vc, pe, w, *, batch, seq):
    m = seq // CMP_STRIDE
    const = lambda shape: pl.BlockSpec(shape, lambda i: (0,) * len(shape))
    return pl.pallas_call(
        functools.partial(_compress_prompt_kernel, m=m),
        grid=(batch,),
        in_specs=[pl.BlockSpec((seq, LANES), lambda i: (i, 0)), pl.BlockSpec((seq, LANES), lambda i: (i, 0)), const(pe.shape), const(w.shape)],
        out_specs=pl.BlockSpec((m, 4 * LANES), lambda i: (i, 0)),
        out_shape=jax.ShapeDtypeStruct((batch * m, 4 * LANES), F32),
        compiler_params=_params(("arbitrary",)),
        name="compress_prompt",
    )(kc, vc, pe, w)


def _page_copy(pt_ref, cache_ref, buf_ref, sem_ref, b, p, slot, *, page_base, kind0):
    page = pt_ref[b, p] + page_base
    return pltpu.make_async_copy(
        cache_ref.at[pl.ds(page * KV_COLS + kind0 * LANES, 2 * LANES), :],
        buf_ref.at[slot, pl.ds(p * 2 * LANES, 2 * LANES), :],
        sem_ref.at[slot])


def _page_pipeline(copy, b, nb, n_pages):
    slot = b % 2

    def start_all(s, sl):
        def body(p, carry):
            copy(s, p, sl).start()
            return carry
        lax.fori_loop(0, n_pages, body, 0)

    @pl.when(b == 0)
    def _():
        start_all(b, slot)

    @pl.when(b + 1 < nb)
    def _():
        start_all(b + 1, 1 - slot)

    def wait_all():
        def body(p, carry):
            copy(b, p, slot).wait()
            return carry
        lax.fori_loop(0, n_pages, body, 0)
    return wait_all


def _compress_sample_kernel(pt_ref, cache_ref, pe_ref, w_ref, out_ref, buf_ref, sem_ref, ktok_ref, vtok_ref, *, n_pages, page_base):
    b = pl.program_id(0)
    slot = b % 2
    copy = functools.partial(_page_copy, pt_ref, cache_ref, buf_ref, sem_ref, page_base=page_base, kind0=0)
    wait_all = _page_pipeline(copy, b, pl.num_programs(0), n_pages)
    wait_all()

    chunks = PAGE_SIZE // CMP_STRIDE

    def tr_body(p, carry):
        r0 = pl.multiple_of(p * 2 * LANES, 2 * LANES)
        t0 = pl.multiple_of(p * chunks * CMP_PITCH, 8)
        kt = buf_ref[slot, pl.ds(r0, LANES), :].T
        vt = buf_ref[slot, pl.ds(r0 + LANES, LANES), :].T
        for c in range(chunks):
            ktok_ref[pl.ds(t0 + c * CMP_PITCH, CMP_STRIDE), :] = kt[c * CMP_STRIDE:(c + 1) * CMP_STRIDE]
            vtok_ref[pl.ds(t0 + c * CMP_PITCH, CMP_STRIDE), :] = vt[c * CMP_STRIDE:(c + 1) * CMP_STRIDE]
        return carry
    lax.fori_loop(0, n_pages, tr_body, 0)
    _compress_rows(ktok_ref, vtok_ref, pe_ref, w_ref, out_ref, n_pages * chunks, pitch=CMP_PITCH)


def _compress_sample(page_table, cache_t, pe, w, *, page_base):
    db, n_pages = page_table.shape
    past = n_pages * PAGE_SIZE
    m = past // CMP_STRIDE
    const = lambda shape: pl.BlockSpec(shape, lambda i, pt: (0,) * len(shape))
    kern = functools.partial(_compress_sample_kernel, n_pages=n_pages, page_base=page_base)
    return pl.pallas_call(
        kern,
        grid_spec=pltpu.PrefetchScalarGridSpec(
            num_scalar_prefetch=1,
            grid=(db,),
            in_specs=[pl.BlockSpec(memory_space=pl.ANY), const(pe.shape), const(w.shape)],
            out_specs=pl.BlockSpec((m, 4 * LANES), lambda i, pt: (i, 0)),
            scratch_shapes=[pltpu.VMEM((2, n_pages * 2 * LANES, LANES), F32), pltpu.SemaphoreType.DMA((2,)),
                            pltpu.VMEM((m * CMP_PITCH, LANES), F32), pltpu.VMEM((m * CMP_PITCH, LANES), F32)],
        ),
        out_shape=jax.ShapeDtypeStruct((db * m, 4 * LANES), F32),
        compiler_params=_params(("arbitrary",)),
        name="compress_sample",
    )(page_table, cache_t, pe, w)


def _compressed_kv(p_all, kg, seg1):
    nc = p_all.shape[0]
    up = lambda a: pltpu.roll(a, nc - 1, 0)
    kc_raw = p_all[:, 0:LANES] + up(p_all[:, LANES:2 * LANES])
    vc = p_all[:, 2 * LANES:3 * LANES] + up(p_all[:, 3 * LANES:4 * LANES])
    return _seg_rms(kc_raw, kg, seg1), vc


def _head_slope(h):
    return 2.0 ** -(h + 1)


def _nsa_prompt_kernel(qt_ref, k_ref, vt_ref, p_ref, gt_ref, kg_ref, seg_ref, tt_ref, o_ref,
                       sc_ref, sel_ref, m_ref, l_ref, acc_ref, *, tq, nkt, n_blk):
    q0 = pl.program_id(1) * tq
    q_last = q0 + tq - 1
    hpg = NSA_HEADS // NSA_GROUPS
    lane_blk = lambda a, h: a[:, h * tq:(h + 1) * tq]

    row = lax.broadcasted_iota(jnp.int32, (LANES, tq), 0)
    cols = [None] * NSA_HEADS
    for c in range(4):
        blk = qt_ref[LANES * c:LANES * (c + 1), :]
        cols[c] = jnp.where(row < HEAD_DIM, blk, jnp.zeros_like(blk))
        cols[4 + c] = jnp.where(row >= HEAD_DIM, blk, jnp.zeros_like(blk))
    qpt = jnp.concatenate(cols, axis=1)

    qpos = q0 + lax.broadcasted_iota(jnp.int32, (1, tq), 1)
    qpos_f = qpos.astype(F32)

    kc, vc = _compressed_kv(p_ref[...], kg_ref[...], seg_ref[...])
    nc = kc.shape[0]
    s_c = _dot(kc.astype(BF16), qpt)
    cstart = lax.broadcasted_iota(jnp.int32, (nc, 1), 0) * CMP_STRIDE
    cdist = jnp.abs(qpos_f - (cstart.astype(F32) + 0.5 * (CMP_LEN - 1)))
    c_ok = (cstart + (CMP_LEN - 1)) <= qpos
    any_ok = (qpos >= CMP_LEN - 1).astype(F32)
    ps = []
    for h in range(NSA_HEADS):
        s = jnp.where(c_ok, lane_blk(s_c, h) - _head_slope(h) * cdist, NEG_INF)
        e = jnp.exp(s - jnp.max(s, axis=0, keepdims=True))
        ps.append(e * (any_ok / jnp.sum(e, axis=0, keepdims=True)))
    o_cmp = _dot(vc.T.astype(BF16), jnp.concatenate(ps, axis=1).astype(BF16))

    imps = []
    for g in range(NSA_GROUPS):
        acc = ps[g * hpg]
        for h in range(1, hpg):
            acc = acc + ps[g * hpg + h]
        imps.append(acc)
    tt = tt_ref[...]
    blk_t = None
    for part in _split3(jnp.concatenate(imps, axis=1)):
        d = _dot(tt, part)
        blk_t = d if blk_t is None else blk_t + d
    nbp = tt.shape[0]
    j_t = lax.broadcasted_iota(jnp.int32, (nbp, 2 * tq), 0)
    qcol = lax.broadcasted_iota(jnp.int32, (nbp, 2 * tq), 1) & (tq - 1)
    back = ((q0 + qcol) >> _log2(SEL_LEN)) - j_t
    forced = (j_t == 0) | ((back >= 0) & (back < N_LOCAL))
    sc_ref[...] = jnp.where(back >= 0, blk_t + jnp.where(forced, FORCE_BONUS, 0.0), -1.0)

    def rank_body(i, cnt):
        r = sc_ref[pl.ds(i, 1), :]
        sc = sc_ref[...]
        return cnt + jnp.where(j_t > i, (r >= sc).astype(F32), (r > sc).astype(F32))
    cnt = lax.fori_loop(0, jnp.minimum(q_last // SEL_LEN + 1, n_blk), rank_body, jnp.zeros((nbp, 2 * tq), F32))
    sel_ref[...] = jnp.where(cnt < N_SEL, 0.0, NEG_INF)

    def flash_init():
        m_ref[...] = jnp.full(m_ref.shape, NEG_INF, F32)
        l_ref[...] = jnp.zeros(l_ref.shape, F32)
        acc_ref[...] = jnp.zeros(acc_ref.shape, F32)

    def flash_tile(s_t, dist, mb, vt):
        m_old = m_ref[...]
        l_old = l_ref[...]
        pbs, ms, ls, als = [], [], [], []
        for h in range(NSA_HEADS):
            t = lane_blk(s_t, h) - _head_slope(h) * dist + mb[h // hpg]
            m_h = jnp.maximum(lane_blk(m_old, h), jnp.max(t, axis=0, keepdims=True))
            al = jnp.exp(lane_blk(m_old, h) - m_h)
            p = jnp.exp(t - m_h)
            ls.append(al * lane_blk(l_old, h) + jnp.sum(p, axis=0, keepdims=True))
            pbs.append(p.astype(BF16))
            ms.append(m_h)
            als.append(al)
        acc_ref[...] = acc_ref[...] * jnp.concatenate(als, axis=1) + _dot(vt, jnp.concatenate(pbs, axis=1))
        m_ref[...] = jnp.concatenate(ms, axis=1)
        l_ref[...] = jnp.concatenate(ls, axis=1)

    flash_init()
    d0 = (lax.broadcasted_iota(jnp.int32, (nkt, tq), 1) - lax.broadcasted_iota(jnp.int32, (nkt, tq), 0)).astype(F32)

    def sel_tile(kt):
        k0 = pl.multiple_of(kt * nkt, nkt)
        dist = d0 + jnp.asarray(q0 - k0, F32)
        causal = jnp.where(dist < 0.0, NEG_INF, 0.0)
        j0 = k0 // SEL_LEN
        rows = [jnp.broadcast_to(sel_ref[pl.ds(j0 + b, 1), :], (SEL_LEN, 2 * tq)) for b in range(nkt // SEL_LEN)]
        mrow = jnp.concatenate(rows, axis=0)
        mb = [mrow[:, 0:tq] + causal, mrow[:, tq:2 * tq] + causal]
        s_t = _dot(k_ref[pl.ds(k0, nkt), 0:LANES], qpt)
        flash_tile(s_t, dist, mb, vt_ref[0:LANES, pl.ds(k0, nkt)])

    def sel_pair(kp, carry):
        sel_tile(2 * kp)
        sel_tile(2 * kp + 1)
        return carry
    n_kt = q_last // nkt + 1
    lax.fori_loop(0, n_kt // 2, sel_pair, 0)

    @pl.when(n_kt % 2 == 1)
    def _():
        sel_tile(n_kt - 1)
    o_sel = acc_ref[...] * (1.0 / l_ref[...])

    flash_init()
    dw0 = (lax.broadcasted_iota(jnp.int32, (tq, tq), 1) - lax.broadcasted_iota(jnp.int32, (tq, tq), 0)).astype(F32)
    for t in range(WINDOW // tq + 1):
        off = WINDOW - t * tq
        k0 = q0 - off

        @pl.when(k0 >= 0)
        def _():
            ks = pl.multiple_of(k0, tq)
            dist = dw0 + float(off)
            wmask = jnp.where((dist >= 0.0) & (dist < float(WINDOW)), 0.0, NEG_INF)
            s_t = _dot(k_ref[pl.ds(ks, tq), LANES:2 * LANES], qpt)
            flash_tile(s_t, dist, [wmask, wmask], vt_ref[LANES:2 * LANES, pl.ds(ks, tq)])
    o_win = acc_ref[...] * (1.0 / l_ref[...])

    gt = gt_ref[...]
    for c in range(4):
        halves = []
        for h, lo in ((c, 0), (4 + c, HEAD_DIM)):
            g = lambda br: gt[br * NSA_HEADS + h:br * NSA_HEADS + h + 1, :]
            pick = lambda a: a[lo:lo + HEAD_DIM, h * tq:(h + 1) * tq]
            halves.append(g(0) * pick(o_cmp) + g(1) * pick(o_sel) + g(2) * pick(o_win))
        o_ref[:, LANES * c:LANES * (c + 1)] = jnp.concatenate(halves, axis=0).T.astype(o_ref.dtype)


def _nsa_prompt(qt, kb, vt, p, gt, kg, seg1, tt, *, batch, seq, tq, nkt):
    nq = seq // tq
    nc = seq // CMP_STRIDE
    n_blk = -(-seq // SEL_LEN)
    r = NSA_HEADS * tq
    nbp = tt.shape[0]
    const = lambda shape: pl.BlockSpec(shape, lambda b, i: (0,) * len(shape))
    kern = functools.partial(_nsa_prompt_kernel, tq=tq, nkt=nkt, n_blk=n_blk)
    return pl.pallas_call(
        kern,
        grid=(batch, nq),
        in_specs=[
            pl.BlockSpec((None, NSA_WIDTH, tq), lambda b, i: (b, 0, i)),
            pl.BlockSpec((seq, 2 * LANES), lambda b, i: (b, 0)),
            pl.BlockSpec((None, 2 * LANES, seq), lambda b, i: (b, 0, 0)),
            pl.BlockSpec((nc, 4 * LANES), lambda b, i: (b, 0)),
            pl.BlockSpec((None, LANES, tq), lambda b, i: (b, 0, i)),
            const(kg.shape), const(seg1.shape), const(tt.shape),
        ],
        out_specs=pl.BlockSpec((tq, NSA_WIDTH), lambda b, i: (b * nq + i, 0)),
        out_shape=jax.ShapeDtypeStruct((batch * seq, NSA_WIDTH), BF16),
        scratch_shapes=[pltpu.VMEM((nbp, 2 * tq), F32), pltpu.VMEM((nbp, 2 * tq), F32),
                        pltpu.VMEM((1, r), F32), pltpu.VMEM((1, r), F32), pltpu.VMEM((LANES, r), F32)],
        compiler_params=_params(("arbitrary", "arbitrary")),
        name="nsa_prompt",
    )(qt, kb, vt, p, gt, kg, seg1, tt)


def _row_meta(tq, q0):
    r = NSA_HEADS * tq
    rid = lax.broadcasted_iota(jnp.int32, (r, 1), 0)
    hh = rid >> _log2(tq)
    ii = rid & (tq - 1)
    slope = lax.bitcast_convert_type((126 - hh) << 23, F32)
    qpos = q0 + ii
    return ii, slope, qpos


def _flash_init(m_ref, l_ref, acc_ref):
    m_ref[...] = jnp.full(m_ref.shape, NEG_INF, F32)
    l_ref[...] = jnp.zeros(l_ref.shape, F32)
    acc_ref[...] = jnp.zeros(acc_ref.shape, F32)


def _flash_update(m_ref, l_ref, acc_ref, s, pv):
    m_old = m_ref[...]
    m_new = jnp.maximum(m_old, jnp.max(s, axis=-1, keepdims=True))
    alpha = jnp.exp(m_old - m_new)
    p = jnp.exp(s - m_new)
    l_ref[...] = alpha * l_ref[...] + jnp.sum(p, axis=-1, keepdims=True)
    acc_ref[...] = alpha * acc_ref[...] + pv(p.astype(BF16))
    m_ref[...] = m_new


def _expand_mask(mask2b, blk0, nkt, tq):
    nbp = mask2b.shape[1]
    j_e = lax.broadcasted_iota(jnp.int32, (nbp, nkt), 0)
    c_e = lax.broadcasted_iota(jnp.int32, (nbp, nkt), 1)
    e = (j_e == blk0 + (c_e >> _log2(SEL_LEN))).astype(BF16)
    me2 = _dot(mask2b, e)
    hpg = NSA_HEADS // NSA_GROUPS
    return jnp.concatenate([me2[0:tq]] * hpg + [me2[tq:2 * tq]] * hpg, axis=0)


def _nsa_sample_kernel(pt_ref, q_ref, p_ref, rows_ref, wnew_ref, wst_ref, gates_ref, kg_ref, seg_ref, tmat_ref, cache_ref,
                       o_ref, buf_ref, sem_ref, m_ref, l_ref, acc_ref, *, tq, pages_per_tile, n_pages, page_base):
    b = pl.program_id(0)
    slot = b % 2
    past = n_pages * PAGE_SIZE
    wb = wst_ref.shape[1]
    hpg = NSA_HEADS // NSA_GROUPS
    copy = functools.partial(_page_copy, pt_ref, cache_ref, buf_ref, sem_ref, page_base=page_base, kind0=2)
    wait_all = _page_pipeline(copy, b, pl.num_programs(0), n_pages)

    r = NSA_HEADS * tq
    qp = jnp.concatenate([q_ref[:, LANES * h:LANES * (h + 1)] for h in range(NSA_HEADS)], axis=0).astype(BF16)
    ii, slope, qpos = _row_meta(tq, past)

    kc, vc = _compressed_kv(p_ref[...], kg_ref[...], seg_ref[...])
    nc = kc.shape[0]
    s = _dot_nt(qp, kc.astype(BF16))
    cstart = lax.broadcasted_iota(jnp.int32, (1, nc), 1) * CMP_STRIDE
    cdist = jnp.abs(qpos.astype(F32) - (cstart.astype(F32) + 0.5 * (CMP_LEN - 1)))
    c_ok = (cstart + (CMP_LEN - 1)) <= qpos
    s = jnp.where(c_ok, s - slope * cdist, NEG_INF)
    e = jnp.exp(s - jnp.max(s, axis=-1, keepdims=True))
    p_cmp = e / jnp.sum(e, axis=-1, keepdims=True) * (qpos >= CMP_LEN - 1).astype(F32)
    o_cmp = _dot(p_cmp.astype(BF16), vc.astype(BF16))

    imps = []
    for g in range(NSA_GROUPS):
        acc = p_cmp[(g * hpg) * tq:(g * hpg + 1) * tq]
        for h in range(1, hpg):
            acc = acc + p_cmp[(g * hpg + h) * tq:(g * hpg + h + 1) * tq]
        imps.append(acc)
    tmat = tmat_ref[...]
    blk = None
    for part in _split3(jnp.concatenate(imps, axis=0)):
        d = _dot(part, tmat)
        blk = d if blk is None else blk + d
    nbp = tmat.shape[1]
    n_blk = -(-(past + tq) // SEL_LEN)
    j_l = lax.broadcasted_iota(jnp.int32, (2 * tq, nbp), 1)
    qrow = lax.broadcasted_iota(jnp.int32, (2 * tq, nbp), 0) & (tq - 1)
    back = ((past + qrow) >> _log2(SEL_LEN)) - j_l
    forced = (j_l == 0) | ((back >= 0) & (back < N_LOCAL))
    score = jnp.where(back >= 0, blk + jnp.where(forced, FORCE_BONUS, 0.0), -1.0)
    cnt = jnp.zeros((2 * tq, nbp), F32)
    for i in range(n_blk):
        col = score[:, i:i + 1]
        cnt = cnt + jnp.where(j_l > i, (col >= score).astype(F32), (col > score).astype(F32))
    mask2 = (cnt < N_SEL).astype(F32)
    mask2b = mask2.astype(BF16)

    wait_all()

    _flash_init(m_ref, l_ref, acc_ref)
    nkt = pages_per_tile * PAGE_SIZE
    dist0 = (past + ii - lax.broadcasted_iota(jnp.int32, (r, nkt), 1)).astype(F32)

    def sel_body(kt, carry):
        k0 = kt * nkt
        tiles = []
        for i in range(pages_per_tile):
            r0 = pl.multiple_of((kt * pages_per_tile + i) * 2 * LANES, 2 * LANES)
            tiles.append((buf_ref[slot, pl.ds(r0, LANES), :].astype(BF16), buf_ref[slot, pl.ds(r0 + LANES, LANES), :].astype(BF16)))
        s = jnp.concatenate([_dot(qp, kt_i) for kt_i, _ in tiles], axis=1)
        me = _expand_mask(mask2b, k0 // SEL_LEN, nkt, tq)
        s = jnp.where(me > 0.5, s - slope * (dist0 - jnp.asarray(k0, F32)), NEG_INF)

        def pv(pb):
            out = None
            for i, (_, vt_i) in enumerate(tiles):
                d = _dot_nt(pb[:, PAGE_SIZE * i:PAGE_SIZE * (i + 1)], vt_i)
                out = d if out is None else out + d
            return out
        _flash_update(m_ref, l_ref, acc_ref, s, pv)
        return carry
    lax.fori_loop(0, n_pages // pages_per_tile, sel_body, 0)

    zpad = jnp.zeros((LANES - tq, LANES), F32)
    dist_new = (ii - lax.broadcasted_iota(jnp.int32, (r, LANES), 1)).astype(F32)
    new_blk = past // SEL_LEN
    me_new = jnp.concatenate([mask2[0:tq, new_blk:new_blk + 1]] * hpg + [mask2[tq:2 * tq, new_blk:new_blk + 1]] * hpg, axis=0)
    k_new = jnp.concatenate([rows_ref[:, 2 * LANES:3 * LANES], zpad], axis=0).astype(BF16)
    v_new = jnp.concatenate([rows_ref[:, 3 * LANES:4 * LANES], zpad], axis=0).astype(BF16)
    ok = (me_new > 0.5) & (dist_new >= 0.0)
    s = jnp.where(ok, _dot_nt(qp, k_new) - slope * dist_new, NEG_INF)
    _flash_update(m_ref, l_ref, acc_ref, s, lambda pb: _dot(pb, v_new))
    o_sel = acc_ref[...] / l_ref[...]

    _flash_init(m_ref, l_ref, acc_ref)
    dist_w = (ii + wb - lax.broadcasted_iota(jnp.int32, (r, wb), 1)).astype(F32)
    ok = (dist_w >= 0.0) & (dist_w < float(WINDOW))
    s = jnp.where(ok, _dot(qp, wst_ref[0:LANES, :].astype(BF16)) - slope * dist_w, NEG_INF)
    vwt = wst_ref[LANES:2 * LANES, :].astype(BF16)
    _flash_update(m_ref, l_ref, acc_ref, s, lambda pb: _dot_nt(pb, vwt))
    wnew = wnew_ref[...]
    kw_new = jnp.concatenate([wnew[:, 0:LANES], zpad], axis=0).astype(BF16)
    vw_new = jnp.concatenate([wnew[:, LANES:2 * LANES], zpad], axis=0).astype(BF16)
    s = jnp.where(dist_new >= 0.0, _dot_nt(qp, kw_new) - slope * dist_new, NEG_INF)
    _flash_update(m_ref, l_ref, acc_ref, s, lambda pb: _dot(pb, vw_new))
    o_win = acc_ref[...] / l_ref[...]

    gates = gates_ref[...]
    lane = lax.broadcasted_iota(jnp.int32, (tq, LANES), 1)
    outs = []
    for h in range(NSA_HEADS):
        rs = slice(h * tq, (h + 1) * tq)
        outs.append(gates[:, h:h + 1] * o_cmp[rs] + gates[:, NSA_HEADS + h:NSA_HEADS + h + 1] * o_sel[rs]
                    + gates[:, 2 * NSA_HEADS + h:2 * NSA_HEADS + h + 1] * o_win[rs])
    for c in range(4):
        o_ref[:, LANES * c:LANES * (c + 1)] = jnp.where(lane < HEAD_DIM, outs[c], outs[4 + c])


def _nsa_sample(page_table, q, p, rows, wnew, wstate_t, gates, kg, seg1, tmat, cache_t, *, tq, pages_per_tile, page_base, wstate_base):
    db, n_pages = page_table.shape
    past = n_pages * PAGE_SIZE
    nc = past // CMP_STRIDE
    wb = wstate_t.shape[1]
    r = NSA_HEADS * tq
    const = lambda shape: pl.BlockSpec(shape, lambda b, pt: (0,) * len(shape))
    kern = functools.partial(_nsa_sample_kernel, tq=tq, pages_per_tile=pages_per_tile, n_pages=n_pages, page_base=page_base)
    return pl.pallas_call(
        kern,
        grid_spec=pltpu.PrefetchScalarGridSpec(
            num_scalar_prefetch=1,
            grid=(db,),
            in_specs=[
                pl.BlockSpec((tq, 2 * NSA_WIDTH), lambda b, pt: (b, 0)),
                pl.BlockSpec((nc, 4 * LANES), lambda b, pt: (b, 0)),
                pl.BlockSpec((tq, KV_COLS), lambda b, pt: (b, 0)),
                pl.BlockSpec((tq, 2 * LANES), lambda b, pt: (b, 0)),
                pl.BlockSpec((2 * LANES, wb), lambda b, pt: (wstate_base + b, 0)),
                pl.BlockSpec((tq, LANES), lambda b, pt: (b, 0)),
                const(kg.shape), const(seg1.shape), const(tmat.shape),
                pl.BlockSpec(memory_space=pl.ANY),
            ],
            out_specs=pl.BlockSpec((tq, NSA_WIDTH), lambda b, pt: (b, 0)),
            scratch_shapes=[pltpu.VMEM((2, n_pages * 2 * LANES, LANES), F32), pltpu.SemaphoreType.DMA((2,)),
                            pltpu.VMEM((r, 1), F32), pltpu.VMEM((r, 1), F32), pltpu.VMEM((r, LANES), F32)],
        ),
        out_shape=jax.ShapeDtypeStruct((db * tq, NSA_WIDTH), F32),
        compiler_params=_params(("arbitrary",)),
        name="nsa_sample",
    )(page_table, q, p, rows, wnew, wstate_t, gates, kg, seg1, tmat, cache_t)


def _hgrn_consts(tc):
    nl = int(np.log2(tc))
    t = np.arange(tc)[:, None]
    u = np.arange(tc)[None, :]
    blocks = [(u <= t), (u > t)]
    masks = [np.eye(tc, dtype=bool)]
    for lv in range(nl):
        hs = 1 << lv
        mid = (t // (2 * hs)) * 2 * hs + hs
        ref = mid - 1
        blocks.append((t >= mid) & (u > ref) & (u <= t))
        blocks.append((t < mid) & (u > t) & (u <= ref))
        masks.append((t // (2 * hs) == u // (2 * hs)) & (t % (2 * hs) >= hs) & (u % (2 * hs) < hs))
    cm = np.concatenate(blocks, axis=0).astype(np.float32)
    mk = np.concatenate(masks, axis=0).astype(np.float32)
    return jnp.asarray(cm, BF16), jnp.asarray(mk, F32), nl


def _hgrn_kernel(cm_ref, mk_ref, hq_ref, hlf_ref, hk_ref, hv_ref, hg_ref, og_ref, s0_ref, o_ref, sout_ref, st_ref, *, tc, nl):
    t = pl.program_id(1)
    nt = pl.num_programs(1)

    @pl.when(t == 0)
    def _():
        for h in range(HGRN_HEADS):
            st_ref[h] = s0_ref[h].T

    cm = cm_ref[...]
    for h in range(HGRN_HEADS):
        sl = slice(HGRN_DK * h, HGRN_DK * (h + 1))
        hi, lo = _split2(hlf_ref[:, sl])
        ee = _dot(cm, jnp.concatenate([hi, lo], axis=1))
        x = jnp.exp(ee[:, 0:LANES] + ee[:, LANES:2 * LANES])
        q = hq_ref[:, sl]
        k = hk_ref[:, sl]
        v = hv_ref[:, sl].astype(BF16)
        xb = x[0:tc]
        xs = x[tc:2 * tc]
        a = mk_ref[0:tc, :] * _dot_nt(q.astype(BF16), k.astype(BF16))
        for lv in range(nl):
            xu = x[(2 + 2 * lv) * tc:(3 + 2 * lv) * tc]
            xl = x[(3 + 2 * lv) * tc:(4 + 2 * lv) * tc]
            a = a + mk_ref[(1 + lv) * tc:(2 + lv) * tc, :] * _dot_nt((q * xu).astype(BF16), (k * xl).astype(BF16))
        st = st_ref[h]
        o = _dot(a.astype(BF16), v) + _dot_nt((q * xb).astype(BF16), st.astype(BF16))
        st_new = st * xb[tc - 1:tc, :] + _dot_tn(v, (k * xs).astype(BF16))
        st_ref[h] = st_new
        on = o * lax.rsqrt(jnp.mean(o * o, axis=-1, keepdims=True) + EPS) * og_ref[...]
        o_ref[:, sl] = (on * hg_ref[:, sl]).astype(o_ref.dtype)

        @pl.when(t == nt - 1)
        def _():
            sout_ref[h] = st_new.T


def _hgrn(hq, hlf, hk, hv, hg, og, s0, *, batch, seq, tc, s0_base):
    cm, mk, nl = _hgrn_consts(tc)
    nt = seq // tc
    const = lambda shape: pl.BlockSpec(shape, lambda b, t: (0,) * len(shape))
    row = pl.BlockSpec((tc, HGRN_WIDTH), lambda b, t: (b * nt + t, 0))
    kern = functools.partial(_hgrn_kernel, tc=tc, nl=nl)
    return pl.pallas_call(
        kern,
        grid=(batch, nt),
        in_specs=[const(cm.shape), const(mk.shape), row, row, row, row, row, const(og.shape),
                  pl.BlockSpec((None, HGRN_HEADS, HGRN_DK, HGRN_DV), lambda b, t: (s0_base + b, 0, 0, 0))],
        out_specs=[row, pl.BlockSpec((None, HGRN_HEADS, HGRN_DK, HGRN_DV), lambda b, t: (b, 0, 0, 0))],
        out_shape=[jax.ShapeDtypeStruct((batch * seq, HGRN_WIDTH), BF16),
                   jax.ShapeDtypeStruct((batch, HGRN_HEADS, HGRN_DK, HGRN_DV), F32)],
        scratch_shapes=[pltpu.VMEM((HGRN_HEADS, HGRN_DV, HGRN_DK), F32)],
        compiler_params=_params(("arbitrary", "arbitrary")),
        name="hgrn",
    )(cm, mk, hq, hlf, hk, hv, hg, og, s0)


def _outproj_kernel(*refs, moe):
    if moe:
        x_ref, on_ref, oh_ref, w_ref, g_ref, r_ref, xo_ref, h_ref, gate_ref = refs
    else:
        x_ref, on_ref, oh_ref, w_ref, g_ref, xo_ref, h_ref = refs
    xn = x_ref[...] + _dot(on_ref[...], w_ref[0:NSA_WIDTH, :]) + _dot(oh_ref[...], w_ref[NSA_WIDTH:NSA_WIDTH + HGRN_WIDTH, :])
    xo_ref[...] = xn
    h = xn * lax.rsqrt(jnp.mean(xn * xn, axis=-1, keepdims=True) + EPS) * g_ref[...]
    h_ref[...] = h.astype(BF16)
    if moe:
        logits = None
        rparts = _split3(r_ref[...])
        hparts = _split3(h)
        for i in range(3):
            for j in range(3 - i):
                d = _dot(hparts[i], rparts[j])
                logits = d if logits is None else logits + d
        lane = lax.broadcasted_iota(jnp.int32, logits.shape, 1).astype(F32)
        lg = jnp.where(lane < N_EXPERTS, logits, NEG_INF)
        m1 = jnp.max(lg, axis=-1, keepdims=True)
        i1 = jnp.min(jnp.where(lg == m1, lane, float(LANES)), axis=-1, keepdims=True)
        lg2 = jnp.where(lane == i1, NEG_INF, lg)
        m2 = jnp.max(lg2, axis=-1, keepdims=True)
        i2 = jnp.min(jnp.where(lg2 == m2, lane, float(LANES)), axis=-1, keepdims=True)
        e2 = jnp.exp(m2 - m1)
        den = 1.0 + e2
        gate_ref[...] = (jnp.where(lane == 0.0, i1, 0.0) + jnp.where(lane == 1.0, i2, 0.0)
                         + jnp.where(lane == 2.0, 1.0 / den, 0.0) + jnp.where(lane == 3.0, e2 / den, 0.0))


def _outproj(x, o_nsa, o_hg, w, g, router, tm):
    t, d = x.shape
    moe = router is not None
    const = lambda shape: pl.BlockSpec(shape, lambda i: (0,) * len(shape))
    row = lambda n: pl.BlockSpec((tm, n), lambda i: (i, 0))
    in_specs = [row(d), row(NSA_WIDTH), row(HGRN_WIDTH), const(w.shape), const((1, d))]
    args = [x, o_nsa, o_hg, w, g]
    out_specs = [row(d), row(d)]
    out_shape = [jax.ShapeDtypeStruct((t, d), F32), jax.ShapeDtypeStruct((t, d), BF16)]
    if moe:
        in_specs.append(const(router.shape))
        args.append(router)
        out_specs.append(row(LANES))
        out_shape.append(jax.ShapeDtypeStruct((t, LANES), F32))
    return pl.pallas_call(
        functools.partial(_outproj_kernel, moe=moe),
        grid=(t // tm,),
        in_specs=in_specs, out_specs=out_specs, out_shape=out_shape,
        compiler_params=_params(("arbitrary",)),
        name="outproj_moe" if moe else "outproj",
    )(*args)


def _ffn_kernel(x_ref, h_ref, wg_ref, wu_ref, wd_ref, o_ref):
    f = pl.program_id(1)
    h = h_ref[...]
    a = _dot(h, wg_ref[...])
    b = _dot(h, wu_ref[...])
    d = _dot((a * _sigmoid(a) * b).astype(BF16), wd_ref[...])

    @pl.when(f == 0)
    def _():
        o_ref[...] = x_ref[...] + d

    @pl.when(f > 0)
    def _():
        o_ref[...] = o_ref[...] + d


def _ffn(x, h, wg, wu, wd, tm, tf):
    t, d = x.shape
    ff = wg.shape[1]
    return pl.pallas_call(
        _ffn_kernel,
        grid=(t // tm, ff // tf),
        in_specs=[pl.BlockSpec((tm, d), lambda i, f: (i, 0)), pl.BlockSpec((tm, d), lambda i, f: (i, 0)),
                  pl.BlockSpec((d, tf), lambda i, f: (0, f)), pl.BlockSpec((d, tf), lambda i, f: (0, f)),
                  pl.BlockSpec((tf, d), lambda i, f: (f, 0))],
        out_specs=pl.BlockSpec((tm, d), lambda i, f: (i, 0)),
        out_shape=jax.ShapeDtypeStruct((t, d), F32),
        compiler_params=_params(("arbitrary", "arbitrary")),
        name="ffn",
    )(x, h, wg, wu, wd)


def _route_kernel(rt_ref, pos_ref, post_ref, cnt_ref, carry_ref, *, ts):
    s = pl.program_id(0)

    @pl.when(s == 0)
    def _():
        carry_ref[...] = jnp.zeros(carry_ref.shape, F32)

    rt = rt_ref[...]
    lane = lax.broadcasted_iota(jnp.int32, (ts, LANES), 1).astype(F32)
    c = jnp.where((lane == rt[:, 0:1]) | (lane == rt[:, 1:2]), 1.0, 0.0)
    lt = (lax.broadcasted_iota(jnp.int32, (ts, ts), 1) < lax.broadcasted_iota(jnp.int32, (ts, ts), 0)).astype(BF16)
    rank = _dot(lt, c.astype(BF16)) + carry_ref[...]
    pos = jnp.where(c > 0.0, rank, -1.0)
    pos_ref[...] = pos
    post_ref[...] = pos.T[0:8, :]
    n = jnp.sum(c, axis=0, keepdims=True)
    cnt_ref[...] = jnp.broadcast_to(n, cnt_ref.shape)
    carry_ref[...] = carry_ref[...] + n


def _route(rt, ts):
    t = rt.shape[0]
    ns = t // ts
    return pl.pallas_call(
        functools.partial(_route_kernel, ts=ts),
        grid=(ns,),
        in_specs=[pl.BlockSpec((ts, LANES), lambda s: (s, 0))],
        out_specs=[pl.BlockSpec((ts, LANES), lambda s: (s, 0)), pl.BlockSpec((8, ts), lambda s: (0, s)),
                   pl.BlockSpec((None, 8, LANES), lambda s: (s, 0, 0))],
        out_shape=[jax.ShapeDtypeStruct((t, LANES), F32), jax.ShapeDtypeStruct((8, t), F32),
                   jax.ShapeDtypeStruct((ns, 8, LANES), F32)],
        scratch_shapes=[pltpu.VMEM((1, LANES), F32)],
        compiler_params=_params(("arbitrary",)),
        name="moe_route",
    )(rt)


def _moe_gather_kernel(texp_ref, trank_ref, slo_ref, nsrc_ref, post_ref, h_ref, o_ref, hbuf_ref, sem_ref, acc_ref, *, tm, ts):
    j = pl.program_id(0)
    e = texp_ref[j]
    r0 = trank_ref[j]
    s0 = slo_ref[j]
    n = nsrc_ref[j]
    acc_ref[...] = jnp.zeros(acc_ref.shape, F32)
    row = lax.broadcasted_iota(jnp.int32, (tm, 1), 0).astype(F32) + r0.astype(F32)

    def copy(s, slot):
        return pltpu.make_async_copy(h_ref.at[pl.ds(pl.multiple_of(s * ts, ts), ts), :], hbuf_ref.at[slot], sem_ref.at[slot])

    @pl.when(n > 0)
    def _():
        copy(s0, 0).start()

    def body(k, carry):
        slot = k % 2

        @pl.when(k + 1 < n)
        def _():
            copy(s0 + k + 1, 1 - slot).start()
        copy(s0 + k, slot).wait()
        prow = post_ref[pl.ds(e, 1), pl.ds(pl.multiple_of((s0 + k) * ts, ts), ts)]
        onehot = jnp.where(prow == row, 1.0, 0.0).astype(BF16)
        acc_ref[...] = acc_ref[...] + _dot(onehot, hbuf_ref[slot])
        return carry
    lax.fori_loop(0, n, body, 0)
    o_ref[...] = acc_ref[...].astype(o_ref.dtype)


def _moe_gather(tables, post, h, *, n_tiles, tm, ts):
    t, d = h.shape
    return pl.pallas_call(
        functools.partial(_moe_gather_kernel, tm=tm, ts=ts),
        grid_spec=pltpu.PrefetchScalarGridSpec(
            num_scalar_prefetch=4,
            grid=(n_tiles,),
            in_specs=[pl.BlockSpec(post.shape, lambda j, *_: (0, 0)), pl.BlockSpec(memory_space=pl.ANY)],
            out_specs=pl.BlockSpec((tm, d), lambda j, *_: (j, 0)),
            scratch_shapes=[pltpu.VMEM((2, ts, d), BF16), pltpu.SemaphoreType.DMA((2,)), pltpu.VMEM((tm, d), F32)],
        ),
        out_shape=jax.ShapeDtypeStruct((n_tiles * tm, d), BF16),
        compiler_params=_params(("arbitrary",)),
        name="moe_gather",
    )(*tables, post, h)


def _moe_ffn_kernel(texp_ref, h_ref, wg_ref, wu_ref, wd_ref, o_ref, acc_ref):
    f = pl.program_id(1)
    h = h_ref[...]
    a = _dot(h, wg_ref[...])
    b = _dot(h, wu_ref[...])
    d = _dot((a * _sigmoid(a) * b).astype(BF16), wd_ref[...])

    @pl.when(f == 0)
    def _():
        acc_ref[...] = d

    @pl.when(f > 0)
    def _():
        acc_ref[...] = acc_ref[...] + d

    @pl.when(f == pl.num_programs(1) - 1)
    def _():
        o_ref[...] = acc_ref[...].astype(o_ref.dtype)


def _moe_ffn(texp, hs, wg, wu, wd, *, tm, tf):
    rows, d = hs.shape
    ff = wg.shape[2]
    return pl.pallas_call(
        _moe_ffn_kernel,
        grid_spec=pltpu.PrefetchScalarGridSpec(
            num_scalar_prefetch=1,
            grid=(rows // tm, ff // tf),
            in_specs=[pl.BlockSpec((tm, d), lambda j, f, te: (j, 0)),
                      pl.BlockSpec((None, d, tf), lambda j, f, te: (te[j], 0, f)),
                      pl.BlockSpec((None, d, tf), lambda j, f, te: (te[j], 0, f)),
                      pl.BlockSpec((None, tf, d), lambda j, f, te: (te[j], f, 0))],
            out_specs=pl.BlockSpec((tm, d), lambda j, f, te: (j, 0)),
            scratch_shapes=[pltpu.VMEM((tm, d), F32)],
        ),
        out_shape=jax.ShapeDtypeStruct((rows, d), BF16),
        compiler_params=_params(("arbitrary", "arbitrary")),
        name="moe_ffn",
    )(texp, hs, wg, wu, wd)


def _moe_combine_kernel(start_ref, delta_ref, x_ref, rt_ref, pos_ref, ys_ref, o_ref, ybuf_ref, sem_ref, *, ts, win):
    s = pl.program_id(0)
    ns = pl.num_programs(0)
    slot = s % 2

    def copy(step, e, sl):
        st = pl.multiple_of(start_ref[step * N_EXPERTS + e], LANES)
        return pltpu.make_async_copy(ys_ref.at[pl.ds(st, win), :], ybuf_ref.at[sl, e], sem_ref.at[sl])

    def start_all(step, sl):
        for e in range(N_EXPERTS):
            copy(step, e, sl).start()

    @pl.when(s == 0)
    def _():
        start_all(s, slot)

    @pl.when(s + 1 < ns)
    def _():
        start_all(s + 1, 1 - slot)

    rt = rt_ref[...]
    pos = pos_ref[...]
    col = lax.broadcasted_iota(jnp.int32, (1, win), 1).astype(F32)
    for e in range(N_EXPERTS):
        copy(s, e, slot).wait()
    y = x_ref[...]
    for e in range(N_EXPERTS):
        rel = pos[:, e:e + 1] + delta_ref[s * N_EXPERTS + e].astype(F32)
        g = jnp.where(rel == col, 1.0, 0.0).astype(BF16)
        w = jnp.where(rt[:, 0:1] == float(e), rt[:, 2:3], 0.0) + jnp.where(rt[:, 1:2] == float(e), rt[:, 3:4], 0.0)
        y = y + w * _dot(g, ybuf_ref[slot, e])
    o_ref[...] = y


def _moe_combine(start, delta, x, rt, pos, ys, *, ts, win):
    t, d = x.shape
    row = lambda n: pl.BlockSpec((ts, n), lambda s, *_: (s, 0))
    return pl.pallas_call(
        functools.partial(_moe_combine_kernel, ts=ts, win=win),
        grid_spec=pltpu.PrefetchScalarGridSpec(
            num_scalar_prefetch=2,
            grid=(t // ts,),
            in_specs=[row(d), row(LANES), row(LANES), pl.BlockSpec(memory_space=pl.ANY)],
            out_specs=row(d),
            scratch_shapes=[pltpu.VMEM((2, N_EXPERTS, win, d), BF16), pltpu.SemaphoreType.DMA((2,))],
        ),
        out_shape=jax.ShapeDtypeStruct((t, d), F32),
        compiler_params=_params(("arbitrary",)),
        name="moe_combine",
    )(start, delta, x, rt, pos, ys)


def _moe(x, h, rt, wg, wu, wd, *, tm, ts, tf):
    t, d = x.shape
    win = ts + LANES
    n_tiles = -(-2 * t // tm) + N_EXPERTS + 1
    pos, post, cnt = _route(rt, ts)
    counts = cnt[:, 0, :N_EXPERTS].astype(jnp.int32)
    cum = jnp.concatenate([jnp.zeros((1, N_EXPERTS), jnp.int32), jnp.cumsum(counts, axis=0)], axis=0)
    tiles_e = -(-cum[-1] // tm)
    tstart = jnp.concatenate([jnp.zeros((1,), jnp.int32), jnp.cumsum(tiles_e)])
    jt = jnp.arange(n_tiles, dtype=jnp.int32)
    texp = jnp.minimum(jnp.sum(jt[:, None] >= tstart[None, 1:], axis=1), N_EXPERTS - 1).astype(jnp.int32)
    used = jt < tstart[-1]
    trank = jnp.where(used, (jt - tstart[texp]) * tm, -2 * tm).astype(jnp.int32)
    lo = cum[:-1].T[texp]
    hi = cum[1:].T[texp]
    hit = used[:, None] & (lo < trank[:, None] + tm) & (hi > trank[:, None])
    slo = jnp.sum(used[:, None] & (hi <= trank[:, None]), axis=1).astype(jnp.int32)
    nsrc = jnp.sum(hit, axis=1).astype(jnp.int32)
    row0 = tstart[:-1][None, :] * tm + cum[:-1]
    start = (row0 // LANES) * LANES
    delta = tstart[:-1][None, :] * tm - start
    hs = _moe_gather((texp, trank, slo, nsrc), post, h, n_tiles=n_tiles, tm=tm, ts=ts)
    ys = _moe_ffn(texp, hs, wg, wu, wd, tm=tm, tf=tf)
    return _moe_combine(start.reshape(-1).astype(jnp.int32), delta.reshape(-1).astype(jnp.int32), x, rt, pos, ys, ts=ts, win=win)


def _head_perm():
    idx = []
    for c in range(4):
        idx += list(range(HEAD_DIM * c, HEAD_DIM * (c + 1))) + list(range(HEAD_DIM * (4 + c), HEAD_DIM * (5 + c)))
    return np.asarray(idx, np.int32)


def _tap_matrix(n_cmp, nc_pad, n_blk, nb_pad):
    r_s = SEL_LEN // CMP_STRIDE
    r_c = CMP_LEN // CMP_STRIDE
    taps = np.convolve(np.ones(r_s), np.ones(r_c)) / r_c
    tm = np.zeros((nc_pad, nb_pad), np.float32)
    for j in range(n_blk):
        for kk, w in enumerate(taps):
            n = j * r_s + kk - (r_c - 1)
            if 0 <= n < n_cmp:
                tm[n, j] = w
    return tm


def _layer_weights(l, w_in, q_gain, k_gain, cmp_pe, cmp_w, w_out):
    perm = _head_perm()
    wl = w_in[l]
    q_end = NSA_WIDTH
    kv_end = q_end + 6 * NSA_GROUPS * HEAD_DIM
    gate_end = kv_end + 3 * NSA_HEADS
    d = wl.shape[0]
    w_pad = jnp.concatenate([wl[:, :q_end][:, perm], wl[:, q_end:kv_end], wl[:, kv_end:gate_end],
                             jnp.zeros((d, LANES - 3 * NSA_HEADS), wl.dtype), wl[:, gate_end:]], axis=1).astype(BF16)
    qg = jnp.tile(q_gain[l], NSA_HEADS)[None, :]
    kg_proj = jnp.stack([jnp.tile(k_gain[l, 1], NSA_GROUPS), jnp.tile(k_gain[l, 2], NSA_GROUPS)])
    kg_cmp = jnp.tile(k_gain[l, 0], NSA_GROUPS)[None, :]
    pairs = CMP_STRIDE // 2
    cw = cmp_w[l].reshape(2, 2, pairs, 2, HEAD_DIM, HEAD_DIM)
    eye = jnp.eye(NSA_GROUPS, dtype=cw.dtype)
    w_cmp = jnp.einsum('krjsde,gh->kjsgdrhe', cw, eye).reshape(2 * pairs * 2 * LANES, 2 * LANES).astype(BF16)
    pe = cmp_pe[l].reshape(2, 2, pairs, 2, 1, HEAD_DIM)
    pe = jnp.broadcast_to(pe, (2, 2, pairs, 2, NSA_GROUPS, HEAD_DIM)).reshape(4 * pairs, 2 * LANES)
    wo = w_out[l]
    wo = jnp.concatenate([wo[:NSA_WIDTH][perm], wo[NSA_WIDTH:]], axis=0).astype(BF16)
    return w_pad, qg, kg_proj, kg_cmp, w_cmp, pe, wo


def _token_minor(a):
    n = a.ndim
    return jnp.transpose(a, tuple(range(n - 4)) + (n - 3, n - 2, n - 1, n - 4))


def _token_major(a):
    n = a.ndim
    return jnp.transpose(a, tuple(range(n - 4)) + (n - 1, n - 4, n - 3, n - 2))


def kernel(x_prompt, x_sample, cache_kv, state_win_kv, state_hgrn, page_table, norm_mix, norm_ffn, w_in, q_gain, k_gain, cmp_pe, cmp_w, hgrn_lb_logits, hgrn_o_gain, w_out, ffn_w_gate, ffn_w_up, ffn_w_down, moe_router, moe_w_gate, moe_w_up, moe_w_down):
    depth = w_in.shape[0]
    batch, seq, d_model = x_prompt.shape
    db, ds, _ = x_sample.shape
    n_pool = cache_kv.shape[1]
    n_pages = page_table.shape[1]
    past = n_pages * PAGE_SIZE
    wb = state_win_kv.shape[2]
    assert wb == WINDOW and seq % 256 == 0 and seq >= WINDOW and ds == 8 and n_pages % 4 == 0

    tq, nkt_p, tc = 128, 256, 128
    tm_p = 256
    tm_f = 512 if (batch * seq) % 512 == 0 else 256
    tm_e = 512
    ts = db * ds
    d_ff = ffn_w_gate.shape[-1]
    tf = d_ff // 2

    sm = jax.nn.softmax(hgrn_lb_logits.astype(F32), axis=0)
    lower = jnp.concatenate([jnp.zeros_like(sm[:1]), jnp.cumsum(sm[1:], axis=0)], axis=0)
    seg = jnp.asarray(np.kron(np.eye(NSA_HEADS), np.ones((HEAD_DIM, HEAD_DIM))), BF16)
    seg1 = seg[:LANES, :LANES]

    nc_p = seq // CMP_STRIDE
    nb_p = -(-seq // SEL_LEN)
    tt_p = jnp.asarray(_tap_matrix(nc_p - 1, nc_p, nb_p, -(-nb_p // 8) * 8).T, BF16)
    nc_s = past // CMP_STRIDE
    nb_s = -(-(past + ds) // SEL_LEN)
    nbp_s = -(-nb_s // LANES) * LANES
    tm_s = jnp.asarray(_tap_matrix(nc_s - 1, nc_s, nb_s, nbp_s), BF16)

    cache_t = _token_minor(cache_kv).reshape(depth * n_pool * KV_COLS, PAGE_SIZE)
    wstate_t = _token_minor(state_win_kv).reshape(depth * db * 2 * LANES, wb)
    hstate = state_hgrn.reshape(depth * db, HGRN_HEADS, HGRN_DK, HGRN_DV)
    zero_state = jnp.zeros((batch, HGRN_HEADS, HGRN_DK, HGRN_DV), F32)

    xp = x_prompt.reshape(batch * seq, d_model)
    xs = x_sample.reshape(ts, d_model)
    kv_p, kv_s, win_p, win_s, hs_p, hs_s = [], [], [], [], [], []
    for l in range(depth):
        w_pad, qg, kg_proj, kg_cmp, w_cmp, pe, wo = _layer_weights(l, w_in, q_gain, k_gain, cmp_pe, cmp_w, w_out)
        lb = lower[l]
        hp = jnp.stack([jnp.log(lb), jnp.log1p(-lb), 1.0 - lb])
        og = hgrn_o_gain[l][None, :]
        g1 = norm_mix[l][None, :]
        g2 = norm_ffn[l][None, :]
        i = l // 2
        if l % 2 == 0:
            router = None
            wg, wu, wd = ffn_w_gate[i].astype(BF16), ffn_w_up[i].astype(BF16), ffn_w_down[i].astype(BF16)
        else:
            router = jnp.pad(moe_router[i], ((0, 0), (0, LANES - N_EXPERTS)))
            wg, wu, wd = moe_w_gate[i].astype(BF16), moe_w_up[i].astype(BF16), moe_w_down[i].astype(BF16)

        def mixer(x, h, rt):
            tm = tm_f if x.shape[0] % tm_f == 0 else x.shape[0]
            if router is None:
                return _ffn(x, h, wg, wu, wd, tm, tf)
            return _moe(x, h, rt, wg, wu, wd, tm=tm_e, ts=min(tm_p, x.shape[0]), tf=tf)

        qt, rows_t, win_t, kb, vt, kc, vc, gt, hq, hlf, hk, hv, hg = _proj_prompt(xp, g1, w_pad, qg, kg_proj, seg, hp, batch=batch, seq=seq, tm=tm_p)
        pp = _compress_prompt(kc, vc, pe, w_cmp, batch=batch, seq=seq)
        o_nsa = _nsa_prompt(qt, kb, vt, pp, gt, kg_cmp, seg1, tt_p, batch=batch, seq=seq, tq=tq, nkt=nkt_p)
        o_hg, s_fin = _hgrn(hq, hlf, hk, hv, hg, og, zero_state, batch=batch, seq=seq, tc=tc, s0_base=0)
        res = _outproj(xp, o_nsa, o_hg, wo, g2, router, tm_p)
        xp = mixer(res[0], res[1], res[2] if router is not None else None)
        kv_p.append(rows_t.reshape(batch, 4, NSA_GROUPS, HEAD_DIM, seq))
        win_p.append(win_t.reshape(batch, 2, NSA_GROUPS, HEAD_DIM, seq)[..., seq - min(WINDOW, seq):])
        hs_p.append(s_fin)

        q, rows, win, gates, hq, hlf, hk, hv, hg = _proj_sample(xs, g1, w_pad, qg, kg_proj, seg, hp)
        ps = _compress_sample(page_table, cache_t, pe, w_cmp, page_base=l * n_pool)
        o_nsa = _nsa_sample(page_table, q, ps, rows, win, wstate_t, gates, kg_cmp, seg1, tm_s, cache_t,
                            tq=ds, pages_per_tile=4, page_base=l * n_pool, wstate_base=l * db)
        padt = lambda a: jnp.pad(a.reshape(db, ds, HGRN_WIDTH), ((0, 0), (0, tc - ds), (0, 0))).reshape(db * tc, HGRN_WIDTH)
        o_hg, s_new = _hgrn(padt(hq), padt(hlf), padt(hk), padt(hv), padt(hg), og, hstate, batch=db, seq=tc, tc=tc, s0_base=l * db)
        o_hg = o_hg.reshape(db, tc, HGRN_WIDTH)[:, :ds].reshape(ts, HGRN_WIDTH)
        res = _outproj(xs, o_nsa.astype(BF16), o_hg, wo, g2, router, ts)
        xs = mixer(res[0], res[1], res[2] if router is not None else None)
        kv_s.append(rows.reshape(db, ds, 4, NSA_GROUPS, HEAD_DIM))
        win_s.append(win.reshape(db, ds, 2, NSA_GROUPS, HEAD_DIM))
        hs_s.append(s_new)

    new_win_sample = jnp.concatenate([state_win_kv[:, :, ds:], jnp.stack(win_s)], axis=2)
    return (xp.reshape(batch, seq, d_model), xs.reshape(db, ds, d_model), _token_major(jnp.stack(kv_p)), jnp.stack(kv_s),
            _token_major(jnp.stack(win_p)), new_win_sample, jnp.stack(hs_p), jnp.stack(hs_s))
```

```python
import functools

import numpy as np
import jax
import jax.numpy as jnp
from jax import lax
from jax.experimental import pallas as pl
from jax.experimental.pallas import tpu as pltpu

F32 = jnp.float32
BF16 = jnp.bfloat16

NSA_HEADS = 8
NSA_GROUPS = 2
HEAD_DIM = 64
NSA_WIDTH = NSA_HEADS * HEAD_DIM
CMP_LEN = 32
CMP_STRIDE = 16
SEL_LEN = 64
N_SEL = 16
N_LOCAL = 2
WINDOW = 512
FORCE_BONUS = 1e4
HGRN_HEADS = 4
HGRN_DK = 128
HGRN_DV = 128
HGRN_WIDTH = HGRN_HEADS * HGRN_DV
N_EXPERTS = 8
EPS = 1e-6
NEG_INF = -1e30
PAGE_SIZE = 128

LANES = 128
KV_COLS = 4 * NSA_GROUPS * HEAD_DIM
CMP_PITCH = CMP_STRIDE + 1
VMEM_LIMIT = 48 * 1024 * 1024


def _log2(n):
    assert n & (n - 1) == 0
    return n.bit_length() - 1


def _dot(a, b):
    return jnp.dot(a, b, preferred_element_type=F32)


def _dot_nt(a, b):
    return lax.dot_general(a, b, (((1,), (1,)), ((), ())), preferred_element_type=F32)


def _dot_tn(a, b):
    return lax.dot_general(a, b, (((0,), (0,)), ((), ())), preferred_element_type=F32)


def _split2(x):
    hi = x.astype(BF16)
    lo = (x - hi.astype(F32)).astype(BF16)
    return hi, lo


def _split3(x):
    hi = x.astype(BF16)
    r = x - hi.astype(F32)
    mid = r.astype(BF16)
    lo = (r - mid.astype(F32)).astype(BF16)
    return hi, mid, lo


def _segsum(x, seg):
    hi, lo = _split2(x)
    return _dot(hi, seg) + _dot(lo, seg)


def _seg_rms(z, gain, seg):
    ss = _segsum(z * z, seg) * (1.0 / HEAD_DIM)
    return z * lax.rsqrt(ss + EPS) * gain


def _sigmoid(x):
    return 1.0 / (1.0 + jnp.exp(-x))


def _params(sem, vmem=VMEM_LIMIT):
    return pltpu.CompilerParams(dimension_semantics=sem, vmem_limit_bytes=vmem)


def _proj_core(x_ref, gain_ref, w_ref, qg_ref, kg_ref, seg_ref, hp_ref, hq_ref, hlf_ref, hk_ref, hv_ref, hg_ref):
    x = x_ref[...]
    ms = jnp.mean(x * x, axis=-1, keepdims=True)
    h = (x * lax.rsqrt(ms + EPS) * gain_ref[...]).astype(BF16)

    def mm(c0, c1):
        return _dot(h, w_ref[:, c0:c1])

    seg1 = seg_ref[0:LANES, 0:LANES]
    qn = _seg_rms(mm(0, NSA_WIDTH), qg_ref[...], seg_ref[...]) * (HEAD_DIM ** -0.5)
    c0 = NSA_WIDTH
    kv = mm(c0, c0 + 6 * LANES)
    kvs = [kv[:, LANES * j:LANES * (j + 1)] for j in range(6)]
    kvs[2] = _seg_rms(kvs[2], kg_ref[0:1, :], seg1)
    kvs[4] = _seg_rms(kvs[4], kg_ref[1:2, :], seg1)
    c0 += 6 * LANES
    gates = _sigmoid(mm(c0, c0 + LANES))
    c0 += LANES
    zq = mm(c0, c0 + HGRN_WIDTH)
    hq_ref[...] = zq * _sigmoid(zq)
    c0 += HGRN_WIDTH
    zf = mm(c0, c0 + HGRN_WIDTH)
    log_sig = jnp.minimum(zf, 0.0) - jnp.log1p(jnp.exp(-jnp.abs(zf)))
    a = hp_ref[0:1, :]
    c = hp_ref[1:2, :] + log_sig
    hlf_ref[...] = jnp.maximum(a, c) + jnp.log1p(jnp.exp(-jnp.abs(a - c)))
    hk_ref[...] = hp_ref[2:3, :] * _sigmoid(-zf)
    c0 += HGRN_WIDTH
    hv_ref[...] = mm(c0, c0 + HGRN_WIDTH)
    c0 += HGRN_WIDTH
    zg = mm(c0, c0 + HGRN_WIDTH)
    hg_ref[...] = zg * _sigmoid(zg)
    return qn, kvs, gates


def _proj_prompt_kernel(x_ref, gain_ref, w_ref, qg_ref, kg_ref, seg_ref, hp_ref,
                        qt_ref, rowst_ref, wint_ref, kb_ref, vt_ref, kc_ref, vc_ref, gt_ref,
                        hq_ref, hlf_ref, hk_ref, hv_ref, hg_ref):
    qn, kvs, gates = _proj_core(x_ref, gain_ref, w_ref, qg_ref, kg_ref, seg_ref, hp_ref, hq_ref, hlf_ref, hk_ref, hv_ref, hg_ref)
    for c in range(4):
        qt_ref[LANES * c:LANES * (c + 1), :] = qn[:, LANES * c:LANES * (c + 1)].T.astype(BF16)
    kvt = [a.T for a in kvs]
    for j in range(4):
        rowst_ref[LANES * j:LANES * (j + 1), :] = kvt[j]
    wint_ref[0:LANES, :] = kvt[4]
    wint_ref[LANES:2 * LANES, :] = kvt[5]
    kb_ref[:, 0:LANES] = kvs[2].astype(BF16)
    kb_ref[:, LANES:2 * LANES] = kvs[4].astype(BF16)
    vt_ref[0:LANES, :] = kvt[3].astype(BF16)
    vt_ref[LANES:2 * LANES, :] = kvt[5].astype(BF16)
    kc_ref[...] = kvs[0]
    vc_ref[...] = kvs[1]
    gt_ref[...] = gates.T


def _proj_sample_kernel(x_ref, gain_ref, w_ref, qg_ref, kg_ref, seg_ref, hp_ref,
                        q_ref, rows_ref, win_ref, gates_ref, hq_ref, hlf_ref, hk_ref, hv_ref, hg_ref):
    qn, kvs, gates = _proj_core(x_ref, gain_ref, w_ref, qg_ref, kg_ref, seg_ref, hp_ref, hq_ref, hlf_ref, hk_ref, hv_ref, hg_ref)
    lane = lax.broadcasted_iota(jnp.int32, (qn.shape[0], LANES), 1)
    for c in range(4):
        blk = qn[:, LANES * c:LANES * (c + 1)]
        q_ref[:, LANES * c:LANES * (c + 1)] = jnp.where(lane < HEAD_DIM, blk, 0.0)
        q_ref[:, LANES * (4 + c):LANES * (5 + c)] = jnp.where(lane >= HEAD_DIM, blk, 0.0)
    for j in range(4):
        rows_ref[:, LANES * j:LANES * (j + 1)] = kvs[j]
    win_ref[:, 0:LANES] = kvs[4]
    win_ref[:, LANES:2 * LANES] = kvs[5]
    gates_ref[...] = gates


def _proj_prompt(x, gain, w, qg, kg, seg, hp, *, batch, seq, tm):
    t, d = x.shape
    nt = seq // tm
    const = lambda shape: pl.BlockSpec(shape, lambda b, i: (0,) * len(shape))
    row = lambda n: pl.BlockSpec((tm, n), lambda b, i: (b * nt + i, 0))
    colt = lambda n: pl.BlockSpec((None, n, tm), lambda b, i: (b, 0, i))
    tshape = lambda n, dt: jax.ShapeDtypeStruct((batch, n, seq), dt)
    rshape = lambda n, dt: jax.ShapeDtypeStruct((t, n), dt)
    return pl.pallas_call(
        _proj_prompt_kernel,
        grid=(batch, nt),
        in_specs=[row(d), const((1, d)), const(w.shape), const(qg.shape), const(kg.shape), const(seg.shape), const(hp.shape)],
        out_specs=[colt(NSA_WIDTH), colt(KV_COLS), colt(2 * LANES), row(2 * LANES), colt(2 * LANES), row(LANES), row(LANES),
                   colt(LANES)] + [row(HGRN_WIDTH)] * 5,
        out_shape=[tshape(NSA_WIDTH, BF16), tshape(KV_COLS, F32), tshape(2 * LANES, F32), rshape(2 * LANES, BF16),
                   tshape(2 * LANES, BF16), rshape(LANES, F32), rshape(LANES, F32), tshape(LANES, F32)]
                  + [rshape(HGRN_WIDTH, F32)] * 5,
        compiler_params=_params(("arbitrary", "arbitrary")),
        name="proj_prompt",
    )(x, gain, w, qg, kg, seg, hp)


def _proj_sample(x, gain, w, qg, kg, seg, hp):
    t, d = x.shape
    const = lambda shape: pl.BlockSpec(shape, lambda i: (0,) * len(shape))
    row = lambda n: pl.BlockSpec((t, n), lambda i: (0, 0))
    outs = [2 * NSA_WIDTH, KV_COLS, 2 * LANES, LANES] + [HGRN_WIDTH] * 5
    return pl.pallas_call(
        _proj_sample_kernel,
        grid=(1,),
        in_specs=[row(d), const((1, d)), const(w.shape), const(qg.shape), const(kg.shape), const(seg.shape), const(hp.shape)],
        out_specs=[row(n) for n in outs],
        out_shape=[jax.ShapeDtypeStruct((t, n), F32) for n in outs],
        compiler_params=_params(("arbitrary",)),
        name="proj_sample",
    )(x, gain, w, qg, kg, seg, hp)


def _compress_rows(k_ref, v_ref, pe_ref, w_ref, out_ref, m, pitch=CMP_STRIDE):
    pairs = CMP_STRIDE // 2
    for kind, ref in ((0, k_ref), (1, v_ref)):
        acc = None
        bias = None
        for j in range(pairs):
            w = w_ref[(kind * pairs + j) * 2 * LANES:(kind * pairs + j + 1) * 2 * LANES, :]
            xa = ref[pl.ds(2 * j, m, stride=pitch), :]
            xb = ref[pl.ds(2 * j + 1, m, stride=pitch), :]
            d = _dot(jnp.concatenate([xa, xb], axis=1).astype(BF16), w)
            acc = d if acc is None else acc + d
            pe0 = pe_ref[(kind * 2) * pairs + j:(kind * 2) * pairs + j + 1, :]
            pe1 = pe_ref[(kind * 2 + 1) * pairs + j:(kind * 2 + 1) * pairs + j + 1, :]
            pel = jnp.concatenate([jnp.broadcast_to(pe0, (8, 2 * LANES)), jnp.broadcast_to(pe1, (8, 2 * LANES))], axis=0)
            pb = _dot(pel.astype(BF16), w)
            bias = pb if bias is None else bias + pb
        out_ref[:, kind * 2 * LANES:kind * 2 * LANES + LANES] = acc[:, 0:LANES] + bias[0:1, 0:LANES]
        out_ref[:, kind * 2 * LANES + LANES:(kind + 1) * 2 * LANES] = acc[:, LANES:2 * LANES] + bias[8:9, LANES:2 * LANES]


def _compress_prompt_kernel(k_ref, v_ref, pe_ref, w_ref, out_ref, *, m):
    _compress_rows(k_ref, v_ref, pe_ref, w_ref, out_ref, m)


def _compress_prompt(kc, vc, pe, w, *, batch, seq):
    m = seq // CMP_STRIDE
    const = lambda shape: pl.BlockSpec(shape, lambda i: (0,) * len(shape))
    return pl.pallas_call(
        functools.partial(_compress_prompt_kernel, m=m),
        grid=(batch,),
        in_specs=[pl.BlockSpec((seq, LANES), lambda i: (i, 0)), pl.BlockSpec((seq, LANES), lambda i: (i, 0)), const(pe.shape), const(w.shape)],
        out_specs=pl.BlockSpec((m, 4 * LANES), lambda i: (i, 0)),
        out_shape=jax.ShapeDtypeStruct((batch * m, 4 * LANES), F32),
        compiler_params=_params(("arbitrary",)),
        name="compress_prompt",
    )(kc, vc, pe, w)


def _page_copy(pt_ref, cache_ref, buf_ref, sem_ref, b, p, slot, *, page_base, kind0):
    page = pt_ref[b, p] + page_base
    return pltpu.make_async_copy(
        cache_ref.at[pl.ds(page * KV_COLS + kind0 * LANES, 2 * LANES), :],
        buf_ref.at[slot, pl.ds(p * 2 * LANES, 2 * LANES), :],
        sem_ref.at[slot])


def _page_pipeline(copy, b, nb, n_pages):
    slot = b % 2

    def start_all(s, sl):
        def body(p, carry):
            copy(s, p, sl).start()
            return carry
        lax.fori_loop(0, n_pages, body, 0)

    @pl.when(b == 0)
    def _():
        start_all(b, slot)

    @pl.when(b + 1 < nb)
    def _():
        start_all(b + 1, 1 - slot)

    def wait_all():
        def body(p, carry):
            copy(b, p, slot).wait()
            return carry
        lax.fori_loop(0, n_pages, body, 0)
    return wait_all


def _compress_sample_kernel(pt_ref, cache_ref, pe_ref, w_ref, out_ref, buf_ref, sem_ref, ktok_ref, vtok_ref, *, n_pages, page_base):
    b = pl.program_id(0)
    slot = b % 2
    copy = functools.partial(_page_copy, pt_ref, cache_ref, buf_ref, sem_ref, page_base=page_base, kind0=0)
    wait_all = _page_pipeline(copy, b, pl.num_programs(0), n_pages)
    wait_all()

    chunks = PAGE_SIZE // CMP_STRIDE

    def tr_body(p, carry):
        r0 = pl.multiple_of(p * 2 * LANES, 2 * LANES)
        t0 = pl.multiple_of(p * chunks * CMP_PITCH, 8)
        kt = buf_ref[slot, pl.ds(r0, LANES), :].T
        vt = buf_ref[slot, pl.ds(r0 + LANES, LANES), :].T
        for c in range(chunks):
            ktok_ref[pl.ds(t0 + c * CMP_PITCH, CMP_STRIDE), :] = kt[c * CMP_STRIDE:(c + 1) * CMP_STRIDE]
            vtok_ref[pl.ds(t0 + c * CMP_PITCH, CMP_STRIDE), :] = vt[c * CMP_STRIDE:(c + 1) * CMP_STRIDE]
        return carry
    lax.fori_loop(0, n_pages, tr_body, 0)
    _compress_rows(ktok_ref, vtok_ref, pe_ref, w_ref, out_ref, n_pages * chunks, pitch=CMP_PITCH)


def _compress_sample(page_table, cache_t, pe, w, *, page_base):
    db, n_pages = page_table.shape
    past = n_pages * PAGE_SIZE
    m = past // CMP_STRIDE
    const = lambda shape: pl.BlockSpec(shape, lambda i, pt: (0,) * len(shape))
    kern = functools.partial(_compress_sample_kernel, n_pages=n_pages, page_base=page_base)
    return pl.pallas_call(
        kern,
        grid_spec=pltpu.PrefetchScalarGridSpec(
            num_scalar_prefetch=1,
            grid=(db,),
            in_specs=[pl.BlockSpec(memory_space=pl.ANY), const(pe.shape), const(w.shape)],
            out_specs=pl.BlockSpec((m, 4 * LANES), lambda i, pt: (i, 0)),
            scratch_shapes=[pltpu.VMEM((2, n_pages * 2 * LANES, LANES), F32), pltpu.SemaphoreType.DMA((2,)),
                            pltpu.VMEM((m * CMP_PITCH, LANES), F32), pltpu.VMEM((m * CMP_PITCH, LANES), F32)],
        ),
        out_shape=jax.ShapeDtypeStruct((db * m, 4 * LANES), F32),
        compiler_params=_params(("arbitrary",)),
        name="compress_sample",
    )(page_table, cache_t, pe, w)


def _compressed_kv(p_all, kg, seg1):
    nc = p_all.shape[0]
    up = lambda a: pltpu.roll(a, nc - 1, 0)
    kc_raw = p_all[:, 0:LANES] + up(p_all[:, LANES:2 * LANES])
    vc = p_all[:, 2 * LANES:3 * LANES] + up(p_all[:, 3 * LANES:4 * LANES])
    return _seg_rms(kc_raw, kg, seg1), vc


def _head_slope(h):
    return 2.0 ** -(h + 1)


def _nsa_prompt_kernel(qt_ref, k_ref, vt_ref, p_ref, gt_ref, kg_ref, seg_ref, tt_ref, o_ref,
                       sc_ref, sel_ref, m_ref, l_ref, acc_ref, *, tq, nkt, n_blk):
    q0 = pl.program_id(1) * tq
    q_last = q0 + tq - 1
    hpg = NSA_HEADS // NSA_GROUPS
    lane_blk = lambda a, h: a[:, h * tq:(h + 1) * tq]

    row = lax.broadcasted_iota(jnp.int32, (LANES, tq), 0)
    cols = [None] * NSA_HEADS
    for c in range(4):
        blk = qt_ref[LANES * c:LANES * (c + 1), :]
        cols[c] = jnp.where(row < HEAD_DIM, blk, jnp.zeros_like(blk))
        cols[4 + c] = jnp.where(row >= HEAD_DIM, blk, jnp.zeros_like(blk))
    qpt = jnp.concatenate(cols, axis=1)

    qpos = q0 + lax.broadcasted_iota(jnp.int32, (1, tq), 1)
    qpos_f = qpos.astype(F32)

    kc, vc = _compressed_kv(p_ref[...], kg_ref[...], seg_ref[...])
    nc = kc.shape[0]
    s_c = _dot(kc.astype(BF16), qpt)
    cstart = lax.broadcasted_iota(jnp.int32, (nc, 1), 0) * CMP_STRIDE
    cdist = jnp.abs(qpos_f - (cstart.astype(F32) + 0.5 * (CMP_LEN - 1)))
    c_ok = (cstart + (CMP_LEN - 1)) <= qpos
    any_ok = (qpos >= CMP_LEN - 1).astype(F32)
    ps = []
    for h in range(NSA_HEADS):
        s = jnp.where(c_ok, lane_blk(s_c, h) - _head_slope(h) * cdist, NEG_INF)
        e = jnp.exp(s - jnp.max(s, axis=0, keepdims=True))
        ps.append(e * (any_ok / jnp.sum(e, axis=0, keepdims=True)))
    o_cmp = _dot(vc.T.astype(BF16), jnp.concatenate(ps, axis=1).astype(BF16))

    imps = []
    for g in range(NSA_GROUPS):
        acc = ps[g * hpg]
        for h in range(1, hpg):
            acc = acc + ps[g * hpg + h]
        imps.append(acc)
    tt = tt_ref[...]
    blk_t = None
    for part in _split3(jnp.concatenate(imps, axis=1)):
        d = _dot(tt, part)
        blk_t = d if blk_t is None else blk_t + d
    nbp = tt.shape[0]
    j_t = lax.broadcasted_iota(jnp.int32, (nbp, 2 * tq), 0)
    qcol = lax.broadcasted_iota(jnp.int32, (nbp, 2 * tq), 1) & (tq - 1)
    back = ((q0 + qcol) >> _log2(SEL_LEN)) - j_t
    forced = (j_t == 0) | ((back >= 0) & (back < N_LOCAL))
    sc_ref[...] = jnp.where(back >= 0, blk_t + jnp.where(forced, FORCE_BONUS, 0.0), -1.0)

    def rank_body(i, cnt):
        r = sc_ref[pl.ds(i, 1), :]
        sc = sc_ref[...]
        return cnt + jnp.where(j_t > i, (r >= sc).astype(F32), (r > sc).astype(F32))
    cnt = lax.fori_loop(0, jnp.minimum(q_last // SEL_LEN + 1, n_blk), rank_body, jnp.zeros((nbp, 2 * tq), F32))
    sel_ref[...] = jnp.where(cnt < N_SEL, 0.0, NEG_INF)

    def flash_init():
        m_ref[...] = jnp.full(m_ref.shape, NEG_INF, F32)
        l_ref[...] = jnp.zeros(l_ref.shape, F32)
        acc_ref[...] = jnp.zeros(acc_ref.shape, F32)

    def flash_tile(s_t, dist, mb, vt):
        m_old = m_ref[...]
        l_old = l_ref[...]
        pbs, ms, ls, als = [], [], [], []
        for h in range(NSA_HEADS):
            t = lane_blk(s_t, h) - _head_slope(h) * dist + mb[h // hpg]
            m_h = jnp.maximum(lane_blk(m_old, h), jnp.max(t, axis=0, keepdims=True))
            al = jnp.exp(lane_blk(m_old, h) - m_h)
            p = jnp.exp(t - m_h)
            ls.append(al * lane_blk(l_old, h) + jnp.sum(p, axis=0, keepdims=True))
            pbs.append(p.astype(BF16))
            ms.append(m_h)
            als.append(al)
        acc_ref[...] = acc_ref[...] * jnp.concatenate(als, axis=1) + _dot(vt, jnp.concatenate(pbs, axis=1))
        m_ref[...] = jnp.concatenate(ms, axis=1)
        l_ref[...] = jnp.concatenate(ls, axis=1)

    flash_init()
    d0 = (lax.broadcasted_iota(jnp.int32, (nkt, tq), 1) - lax.broadcasted_iota(jnp.int32, (nkt, tq), 0)).astype(F32)

    def sel_tile(kt):
        k0 = pl.multiple_of(kt * nkt, nkt)
        dist = d0 + jnp.asarray(q0 - k0, F32)
        causal = jnp.where(dist < 0.0, NEG_INF, 0.0)
        j0 = k0 // SEL_LEN
        rows = [jnp.broadcast_to(sel_ref[pl.ds(j0 + b, 1), :], (SEL_LEN, 2 * tq)) for b in range(nkt // SEL_LEN)]
        mrow = jnp.concatenate(rows, axis=0)
        mb = [mrow[:, 0:tq] + causal, mrow[:, tq:2 * tq] + causal]
        s_t = _dot(k_ref[pl.ds(k0, nkt), 0:LANES], qpt)
        flash_tile(s_t, dist, mb, vt_ref[0:LANES, pl.ds(k0, nkt)])

    def sel_pair(kp, carry):
        sel_tile(2 * kp)
        sel_tile(2 * kp + 1)
        return carry
    n_kt = q_last // nkt + 1
    lax.fori_loop(0, n_kt // 2, sel_pair, 0)

    @pl.when(n_kt % 2 == 1)
    def _():
        sel_tile(n_kt - 1)
    o_sel = acc_ref[...] * (1.0 / l_ref[...])

    flash_init()
    dw0 = (lax.broadcasted_iota(jnp.int32, (tq, tq), 1) - lax.broadcasted_iota(jnp.int32, (tq, tq), 0)).astype(F32)
    for t in range(WINDOW // tq + 1):
        off = WINDOW - t * tq
        k0 = q0 - off

        @pl.when(k0 >= 0)
        def _():
            ks = pl.multiple_of(k0, tq)
            dist = dw0 + float(off)
            wmask = jnp.where((dist >= 0.0) & (dist < float(WINDOW)), 0.0, NEG_INF)
            s_t = _dot(k_ref[pl.ds(ks, tq), LANES:2 * LANES], qpt)
            flash_tile(s_t, dist, [wmask, wmask], vt_ref[LANES:2 * LANES, pl.ds(ks, tq)])
    o_win = acc_ref[...] * (1.0 / l_ref[...])

    gt = gt_ref[...]
    for c in range(4):
        halves = []
        for h, lo in ((c, 0), (4 + c, HEAD_DIM)):
            g = lambda br: gt[br * NSA_HEADS + h:br * NSA_HEADS + h + 1, :]
            pick = lambda a: a[lo:lo + HEAD_DIM, h * tq:(h + 1) * tq]
            halves.append(g(0) * pick(o_cmp) + g(1) * pick(o_sel) + g(2) * pick(o_win))
        o_ref[:, LANES * c:LANES * (c + 1)] = jnp.concatenate(halves, axis=0).T.astype(o_ref.dtype)


def _nsa_prompt(qt, kb, vt, p, gt, kg, seg1, tt, *, batch, seq, tq, nkt):
    nq = seq // tq
    nc = seq // CMP_STRIDE
    n_blk = -(-seq // SEL_LEN)
    r = NSA_HEADS * tq
    nbp = tt.shape[0]
    const = lambda shape: pl.BlockSpec(shape, lambda b, i: (0,) * len(shape))
    kern = functools.partial(_nsa_prompt_kernel, tq=tq, nkt=nkt, n_blk=n_blk)
    return pl.pallas_call(
        kern,
        grid=(batch, nq),
        in_specs=[
            pl.BlockSpec((None, NSA_WIDTH, tq), lambda b, i: (b, 0, i)),
            pl.BlockSpec((seq, 2 * LANES), lambda b, i: (b, 0)),
            pl.BlockSpec((None, 2 * LANES, seq), lambda b, i: (b, 0, 0)),
            pl.BlockSpec((nc, 4 * LANES), lambda b, i: (b, 0)),
            pl.BlockSpec((None, LANES, tq), lambda b, i: (b, 0, i)),
            const(kg.shape), const(seg1.shape), const(tt.shape),
        ],
        out_specs=pl.BlockSpec((tq, NSA_WIDTH), lambda b, i: (b * nq + i, 0)),
        out_shape=jax.ShapeDtypeStruct((batch * seq, NSA_WIDTH), BF16),
        scratch_shapes=[pltpu.VMEM((nbp, 2 * tq), F32), pltpu.VMEM((nbp, 2 * tq), F32),
                        pltpu.VMEM((1, r), F32), pltpu.VMEM((1, r), F32), pltpu.VMEM((LANES, r), F32)],
        compiler_params=_params(("arbitrary", "arbitrary")),
        name="nsa_prompt",
    )(qt, kb, vt, p, gt, kg, seg1, tt)


def _row_meta(tq, q0):
    r = NSA_HEADS * tq
    rid = lax.broadcasted_iota(jnp.int32, (r, 1), 0)
    hh = rid >> _log2(tq)
    ii = rid & (tq - 1)
    slope = lax.bitcast_convert_type((126 - hh) << 23, F32)
    qpos = q0 + ii
    return ii, slope, qpos


def _flash_init(m_ref, l_ref, acc_ref):
    m_ref[...] = jnp.full(m_ref.shape, NEG_INF, F32)
    l_ref[...] = jnp.zeros(l_ref.shape, F32)
    acc_ref[...] = jnp.zeros(acc_ref.shape, F32)


def _flash_update(m_ref, l_ref, acc_ref, s, pv):
    m_old = m_ref[...]
    m_new = jnp.maximum(m_old, jnp.max(s, axis=-1, keepdims=True))
    alpha = jnp.exp(m_old - m_new)
    p = jnp.exp(s - m_new)
    l_ref[...] = alpha * l_ref[...] + jnp.sum(p, axis=-1, keepdims=True)
    acc_ref[...] = alpha * acc_ref[...] + pv(p.astype(BF16))
    m_ref[...] = m_new


def _expand_mask(mask2b, blk0, nkt, tq):
    nbp = mask2b.shape[1]
    j_e = lax.broadcasted_iota(jnp.int32, (nbp, nkt), 0)
    c_e = lax.broadcasted_iota(jnp.int32, (nbp, nkt), 1)
    e = (j_e == blk0 + (c_e >> _log2(SEL_LEN))).astype(BF16)
    me2 = _dot(mask2b, e)
    hpg = NSA_HEADS // NSA_GROUPS
    return jnp.concatenate([me2[0:tq]] * hpg + [me2[tq:2 * tq]] * hpg, axis=0)


def _nsa_sample_kernel(pt_ref, q_ref, p_ref, rows_ref, wnew_ref, wst_ref, gates_ref, kg_ref, seg_ref, tmat_ref, cache_ref,
                       o_ref, buf_ref, sem_ref, m_ref, l_ref, acc_ref, *, tq, pages_per_tile, n_pages, page_base):
    b = pl.program_id(0)
    slot = b % 2
    past = n_pages * PAGE_SIZE
    wb = wst_ref.shape[1]
    hpg = NSA_HEADS // NSA_GROUPS
    copy = functools.partial(_page_copy, pt_ref, cache_ref, buf_ref, sem_ref, page_base=page_base, kind0=2)
    wait_all = _page_pipeline(copy, b, pl.num_programs(0), n_pages)

    r = NSA_HEADS * tq
    qp = jnp.concatenate([q_ref[:, LANES * h:LANES * (h + 1)] for h in range(NSA_HEADS)], axis=0).astype(BF16)
    ii, slope, qpos = _row_meta(tq, past)

    kc, vc = _compressed_kv(p_ref[...], kg_ref[...], seg_ref[...])
    nc = kc.shape[0]
    s = _dot_nt(qp, kc.astype(BF16))
    cstart = lax.broadcasted_iota(jnp.int32, (1, nc), 1) * CMP_STRIDE
    cdist = jnp.abs(qpos.astype(F32) - (cstart.astype(F32) + 0.5 * (CMP_LEN - 1)))
    c_ok = (cstart + (CMP_LEN - 1)) <= qpos
    s = jnp.where(c_ok, s - slope * cdist, NEG_INF)
    e = jnp.exp(s - jnp.max(s, axis=-1, keepdims=True))
    p_cmp = e / jnp.sum(e, axis=-1, keepdims=True) * (qpos >= CMP_LEN - 1).astype(F32)
    o_cmp = _dot(p_cmp.astype(BF16), vc.astype(BF16))

    imps = []
    for g in range(NSA_GROUPS):
        acc = p_cmp[(g * hpg) * tq:(g * hpg + 1) * tq]
        for h in range(1, hpg):
            acc = acc + p_cmp[(g * hpg + h) * tq:(g * hpg + h + 1) * tq]
        imps.append(acc)
    tmat = tmat_ref[...]
    blk = None
    for part in _split3(jnp.concatenate(imps, axis=0)):
        d = _dot(part, tmat)
        blk = d if blk is None else blk + d
    nbp = tmat.shape[1]
    n_blk = -(-(past + tq) // SEL_LEN)
    j_l = lax.broadcasted_iota(jnp.int32, (2 * tq, nbp), 1)
    qrow = lax.broadcasted_iota(jnp.int32, (2 * tq, nbp), 0) & (tq - 1)
    back = ((past + qrow) >> _log2(SEL_LEN)) - j_l
    forced = (j_l == 0) | ((back >= 0) & (back < N_LOCAL))
    score = jnp.where(back >= 0, blk + jnp.where(forced, FORCE_BONUS, 0.0), -1.0)
    cnt = jnp.zeros((2 * tq, nbp), F32)
    for i in range(n_blk):
        col = score[:, i:i + 1]
        cnt = cnt + jnp.where(j_l > i, (col >= score).astype(F32), (col > score).astype(F32))
    mask2 = (cnt < N_SEL).astype(F32)
    mask2b = mask2.astype(BF16)

    wait_all()

    _flash_init(m_ref, l_ref, acc_ref)
    nkt = pages_per_tile * PAGE_SIZE
    dist0 = (past + ii - lax.broadcasted_iota(jnp.int32, (r, nkt), 1)).astype(F32)

    def sel_body(kt, carry):
        k0 = kt * nkt
        tiles = []
        for i in range(pages_per_tile):
            r0 = pl.multiple_of((kt * pages_per_tile + i) * 2 * LANES, 2 * LANES)
            tiles.append((buf_ref[slot, pl.ds(r0, LANES), :].astype(BF16), buf_ref[slot, pl.ds(r0 + LANES, LANES), :].astype(BF16)))
        s = jnp.concatenate([_dot(qp, kt_i) for kt_i, _ in tiles], axis=1)
        me = _expand_mask(mask2b, k0 // SEL_LEN, nkt, tq)
        s = jnp.where(me > 0.5, s - slope * (dist0 - jnp.asarray(k0, F32)), NEG_INF)

        def pv(pb):
            out = None
            for i, (_, vt_i) in enumerate(tiles):
                d = _dot_nt(pb[:, PAGE_SIZE * i:PAGE_SIZE * (i + 1)], vt_i)
                out = d if out is None else out + d
            return out
        _flash_update(m_ref, l_ref, acc_ref, s, pv)
        return carry
    lax.fori_loop(0, n_pages // pages_per_tile, sel_body, 0)

    zpad = jnp.zeros((LANES - tq, LANES), F32)
    dist_new = (ii - lax.broadcasted_iota(jnp.int32, (r, LANES), 1)).astype(F32)
    new_blk = past // SEL_LEN
    me_new = jnp.concatenate([mask2[0:tq, new_blk:new_blk + 1]] * hpg + [mask2[tq:2 * tq, new_blk:new_blk + 1]] * hpg, axis=0)
    k_new = jnp.concatenate([rows_ref[:, 2 * LANES:3 * LANES], zpad], axis=0).astype(BF16)
    v_new = jnp.concatenate([rows_ref[:, 3 * LANES:4 * LANES], zpad], axis=0).astype(BF16)
    ok = (me_new > 0.5) & (dist_new >= 0.0)
    s = jnp.where(ok, _dot_nt(qp, k_new) - slope * dist_new, NEG_INF)
    _flash_update(m_ref, l_ref, acc_ref, s, lambda pb: _dot(pb, v_new))
    o_sel = acc_ref[...] / l_ref[...]

    _flash_init(m_ref, l_ref, acc_ref)
    dist_w = (ii + wb - lax.broadcasted_iota(jnp.int32, (r, wb), 1)).astype(F32)
    ok = (dist_w >= 0.0) & (dist_w < float(WINDOW))
    s = jnp.where(ok, _dot(qp, wst_ref[0:LANES, :].astype(BF16)) - slope * dist_w, NEG_INF)
    vwt = wst_ref[LANES:2 * LANES, :].astype(BF16)
    _flash_update(m_ref, l_ref, acc_ref, s, lambda pb: _dot_nt(pb, vwt))
    wnew = wnew_ref[...]
    kw_new = jnp.concatenate([wnew[:, 0:LANES], zpad], axis=0).astype(BF16)
    vw_new = jnp.concatenate([wnew[:, LANES:2 * LANES], zpad], axis=0).astype(BF16)
    s = jnp.where(dist_new >= 0.0, _dot_nt(qp, kw_new) - slope * dist_new, NEG_INF)
    _flash_update(m_ref, l_ref, acc_ref, s, lambda pb: _dot(pb, vw_new))
    o_win = acc_ref[...] / l_ref[...]

    gates = gates_ref[...]
    lane = lax.broadcasted_iota(jnp.int32, (tq, LANES), 1)
    outs = []
    for h in range(NSA_HEADS):
        rs = slice(h * tq, (h + 1) * tq)
        outs.append(gates[:, h:h + 1] * o_cmp[rs] + gates[:, NSA_HEADS + h:NSA_HEADS + h + 1] * o_sel[rs]
                    + gates[:, 2 * NSA_HEADS + h:2 * NSA_HEADS + h + 1] * o_win[rs])
    for c in range(4):
        o_ref[:, LANES * c:LANES * (c + 1)] = jnp.where(lane < HEAD_DIM, outs[c], outs[4 + c])


def _nsa_sample(page_table, q, p, rows, wnew, wstate_t, gates, kg, seg1, tmat, cache_t, *, tq, pages_per_tile, page_base, wstate_base):
    db, n_pages = page_table.shape
    past = n_pages * PAGE_SIZE
    nc = past // CMP_STRIDE
    wb = wstate_t.shape[1]
    r = NSA_HEADS * tq
    const = lambda shape: pl.BlockSpec(shape, lambda b, pt: (0,) * len(shape))
    kern = functools.partial(_nsa_sample_kernel, tq=tq, pages_per_tile=pages_per_tile, n_pages=n_pages, page_base=page_base)
    return pl.pallas_call(
        kern,
        grid_spec=pltpu.PrefetchScalarGridSpec(
            num_scalar_prefetch=1,
            grid=(db,),
            in_specs=[
                pl.BlockSpec((tq, 2 * NSA_WIDTH), lambda b, pt: (b, 0)),
                pl.BlockSpec((nc, 4 * LANES), lambda b, pt: (b, 0)),
                pl.BlockSpec((tq, KV_COLS), lambda b, pt: (b, 0)),
                pl.BlockSpec((tq, 2 * LANES), lambda b, pt: (b, 0)),
                pl.BlockSpec((2 * LANES, wb), lambda b, pt: (wstate_base + b, 0)),
                pl.BlockSpec((tq, LANES), lambda b, pt: (b, 0)),
                const(kg.shape), const(seg1.shape), const(tmat.shape),
                pl.BlockSpec(memory_space=pl.ANY),
            ],
            out_specs=pl.BlockSpec((tq, NSA_WIDTH), lambda b, pt: (b, 0)),
            scratch_shapes=[pltpu.VMEM((2, n_pages * 2 * LANES, LANES), F32), pltpu.SemaphoreType.DMA((2,)),
                            pltpu.VMEM((r, 1), F32), pltpu.VMEM((r, 1), F32), pltpu.VMEM((r, LANES), F32)],
        ),
        out_shape=jax.ShapeDtypeStruct((db * tq, NSA_WIDTH), F32),
        compiler_params=_params(("arbitrary",)),
        name="nsa_sample",
    )(page_table, q, p, rows, wnew, wstate_t, gates, kg, seg1, tmat, cache_t)


def _hgrn_consts(tc):
    nl = int(np.log2(tc))
    t = np.arange(tc)[:, None]
    u = np.arange(tc)[None, :]
    nl_mxu = min(3, nl)
    blocks = [(u <= t)]
    masks = [np.eye(tc, dtype=bool)]
    for lv in range(nl):
        hs = 1 << lv
        mid = (t // (2 * hs)) * 2 * hs + hs
        ref = mid - 1
        if lv < nl_mxu:
            blocks.append((t >= mid) & (u > ref) & (u <= t))
            blocks.append((t < mid) & (u > t) & (u <= ref))
        masks.append((t // (2 * hs) == u // (2 * hs)) & (t % (2 * hs) >= hs) & (u % (2 * hs) < hs))
    cm = np.concatenate(blocks, axis=0).astype(np.float32)
    mk = np.concatenate(masks, axis=0).astype(np.float32)
    return jnp.asarray(cm, BF16), jnp.asarray(mk, F32), nl, nl_mxu


def _hgrn_kernel(cm_ref, mk_ref, hq_ref, hlf_ref, hk_ref, hv_ref, hg_ref, og_ref, s0_ref, o_ref, sout_ref, st_ref, *, tc, nl, nl_mxu, nb):
    t = pl.program_id(1)
    nt = pl.num_programs(1)

    @pl.when(t == 0)
    def _():
        for i in range(nb):
            for h in range(HGRN_HEADS):
                st_ref[i, h] = s0_ref[i, h].T

    cm = cm_ref[...]
    row_id = lax.broadcasted_iota(jnp.int32, (tc, LANES), 0)
    for i in range(nb):
        for h in range(HGRN_HEADS):
            sl = slice(HGRN_DK * h, HGRN_DK * (h + 1))
            hi, lo = _split2(hlf_ref[i, :, sl])
            ee = _dot(cm, jnp.concatenate([hi, lo], axis=1))
            ee = ee[:, 0:LANES] + ee[:, LANES:2 * LANES]
            b = ee[0:tc]
            q = hq_ref[i, :, sl]
            k = hk_ref[i, :, sl]
            v = hv_ref[i, :, sl].astype(BF16)
            xb = jnp.exp(b)
            xs = jnp.exp(b[tc - 1:tc, :] - b)
            a = mk_ref[0:tc, :] * _dot_nt(q.astype(BF16), k.astype(BF16))
            for lv in range(nl):
                if lv < nl_mxu:
                    xu = jnp.exp(ee[(1 + 2 * lv) * tc:(2 + 2 * lv) * tc])
                    xl = jnp.exp(ee[(2 + 2 * lv) * tc:(3 + 2 * lv) * tc])
                else:
                    hs = 1 << lv
                    bref = jnp.concatenate([jnp.broadcast_to(b[j + hs - 1:j + hs, :], (2 * hs, LANES)) for j in range(0, tc, 2 * hs)], axis=0)
                    upper = ((row_id >> lv) & 1) == 1
                    diff = b - bref
                    xu = jnp.exp(jnp.where(upper, diff, 0.0))
                    xl = jnp.exp(jnp.where(upper, 0.0, -diff))
                a = a + mk_ref[(1 + lv) * tc:(2 + lv) * tc, :] * _dot_nt((q * xu).astype(BF16), (k * xl).astype(BF16))
            st = st_ref[i, h]
            o = _dot(a.astype(BF16), v) + _dot_nt((q * xb).astype(BF16), st.astype(BF16))
            st_new = st * xb[tc - 1:tc, :] + _dot_tn(v, (k * xs).astype(BF16))
            st_ref[i, h] = st_new
            on = o * lax.rsqrt(jnp.mean(o * o, axis=-1, keepdims=True) + EPS) * og_ref[...]
            o_ref[i, :, sl] = (on * hg_ref[i, :, sl]).astype(o_ref.dtype)

    @pl.when(t == nt - 1)
    def _():
        for i in range(nb):
            for h in range(HGRN_HEADS):
                sout_ref[i, h] = st_ref[i, h].T


def _hgrn(hq, hlf, hk, hv, hg, og, s0, *, batch, seq, tc, s0_base):
    cm, mk, nl, nl_mxu = _hgrn_consts(tc)
    nt = seq // tc
    nb = 2 if batch % 2 == 0 and s0_base % 2 == 0 else 1
    const = lambda shape: pl.BlockSpec(shape, lambda b, t: (0,) * len(shape))
    row = pl.BlockSpec((nb, tc, HGRN_WIDTH), lambda b, t: (b, t, 0))
    state = lambda base: pl.BlockSpec((nb, HGRN_HEADS, HGRN_DK, HGRN_DV), lambda b, t: (base + b, 0, 0, 0))
    kern = functools.partial(_hgrn_kernel, tc=tc, nl=nl, nl_mxu=nl_mxu, nb=nb)
    seq3 = lambda a: a.reshape(batch, seq, HGRN_WIDTH)
    o, s_out = pl.pallas_call(
        kern,
        grid=(batch // nb, nt),
        in_specs=[const(cm.shape), const(mk.shape), row, row, row, row, row, const(og.shape), state(s0_base // nb)],
        out_specs=[row, state(0)],
        out_shape=[jax.ShapeDtypeStruct((batch, seq, HGRN_WIDTH), BF16),
                   jax.ShapeDtypeStruct((batch, HGRN_HEADS, HGRN_DK, HGRN_DV), F32)],
        scratch_shapes=[pltpu.VMEM((nb, HGRN_HEADS, HGRN_DV, HGRN_DK), F32)],
        compiler_params=_params(("arbitrary", "arbitrary")),
        name="hgrn",
    )(cm, mk, seq3(hq), seq3(hlf), seq3(hk), seq3(hv), seq3(hg), og, s0)
    return o.reshape(batch * seq, HGRN_WIDTH), s_out


def _outproj_kernel(*refs, moe):
    if moe:
        x_ref, on_ref, oh_ref, w_ref, g_ref, r_ref, xo_ref, h_ref, gate_ref = refs
    else:
        x_ref, on_ref, oh_ref, w_ref, g_ref, xo_ref, h_ref = refs
    xn = x_ref[...] + _dot(on_ref[...], w_ref[0:NSA_WIDTH, :]) + _dot(oh_ref[...], w_ref[NSA_WIDTH:NSA_WIDTH + HGRN_WIDTH, :])
    xo_ref[...] = xn
    h = xn * lax.rsqrt(jnp.mean(xn * xn, axis=-1, keepdims=True) + EPS) * g_ref[...]
    h_ref[...] = h.astype(BF16)
    if moe:
        logits = None
        rparts = _split3(r_ref[...])
        hparts = _split3(h)
        for i in range(3):
            for j in range(3 - i):
                d = _dot(hparts[i], rparts[j])
                logits = d if logits is None else logits + d
        lane = lax.broadcasted_iota(jnp.int32, logits.shape, 1).astype(F32)
        lg = jnp.where(lane < N_EXPERTS, logits, NEG_INF)
        m1 = jnp.max(lg, axis=-1, keepdims=True)
        i1 = jnp.min(jnp.where(lg == m1, lane, float(LANES)), axis=-1, keepdims=True)
        lg2 = jnp.where(lane == i1, NEG_INF, lg)
        m2 = jnp.max(lg2, axis=-1, keepdims=True)
        i2 = jnp.min(jnp.where(lg2 == m2, lane, float(LANES)), axis=-1, keepdims=True)
        e2 = jnp.exp(m2 - m1)
        den = 1.0 + e2
        gate_ref[...] = (jnp.where(lane == 0.0, i1, 0.0) + jnp.where(lane == 1.0, i2, 0.0)
                         + jnp.where(lane == 2.0, 1.0 / den, 0.0) + jnp.where(lane == 3.0, e2 / den, 0.0))


def _outproj(x, o_nsa, o_hg, w, g, router, tm):
    t, d = x.shape
    moe = router is not None
    const = lambda shape: pl.BlockSpec(shape, lambda i: (0,) * len(shape))
    row = lambda n: pl.BlockSpec((tm, n), lambda i: (i, 0))
    in_specs = [row(d), row(NSA_WIDTH), row(HGRN_WIDTH), const(w.shape), const((1, d))]
    args = [x, o_nsa, o_hg, w, g]
    out_specs = [row(d), row(d)]
    out_shape = [jax.ShapeDtypeStruct((t, d), F32), jax.ShapeDtypeStruct((t, d), BF16)]
    if moe:
        in_specs.append(const(router.shape))
        args.append(router)
        out_specs.append(row(LANES))
        out_shape.append(jax.ShapeDtypeStruct((t, LANES), F32))
    return pl.pallas_call(
        functools.partial(_outproj_kernel, moe=moe),
        grid=(t // tm,),
        in_specs=in_specs, out_specs=out_specs, out_shape=out_shape,
        compiler_params=_params(("arbitrary",)),
        name="outproj_moe" if moe else "outproj",
    )(*args)


def _ffn_kernel(x_ref, h_ref, wg_ref, wu_ref, wd_ref, o_ref):
    f = pl.program_id(1)
    h = h_ref[...]
    a = _dot(h, wg_ref[...])
    b = _dot(h, wu_ref[...])
    d = _dot((a * _sigmoid(a) * b).astype(BF16), wd_ref[...])

    @pl.when(f == 0)
    def _():
        o_ref[...] = x_ref[...] + d

    @pl.when(f > 0)
    def _():
        o_ref[...] = o_ref[...] + d


def _ffn(x, h, wg, wu, wd, tm, tf):
    t, d = x.shape
    ff = wg.shape[1]
    return pl.pallas_call(
        _ffn_kernel,
        grid=(t // tm, ff // tf),
        in_specs=[pl.BlockSpec((tm, d), lambda i, f: (i, 0)), pl.BlockSpec((tm, d), lambda i, f: (i, 0)),
                  pl.BlockSpec((d, tf), lambda i, f: (0, f)), pl.BlockSpec((d, tf), lambda i, f: (0, f)),
                  pl.BlockSpec((tf, d), lambda i, f: (f, 0))],
        out_specs=pl.BlockSpec((tm, d), lambda i, f: (i, 0)),
        out_shape=jax.ShapeDtypeStruct((t, d), F32),
        compiler_params=_params(("arbitrary", "arbitrary")),
        name="ffn",
    )(x, h, wg, wu, wd)


def _route_kernel(rt_ref, pos_ref, post_ref, cnt_ref, carry_ref, *, ts):
    s = pl.program_id(0)

    @pl.when(s == 0)
    def _():
        carry_ref[...] = jnp.zeros(carry_ref.shape, F32)

    rt = rt_ref[...]
    lane = lax.broadcasted_iota(jnp.int32, (ts, LANES), 1).astype(F32)
    c = jnp.where((lane == rt[:, 0:1]) | (lane == rt[:, 1:2]), 1.0, 0.0)
    lt = (lax.broadcasted_iota(jnp.int32, (ts, ts), 1) < lax.broadcasted_iota(jnp.int32, (ts, ts), 0)).astype(BF16)
    rank = _dot(lt, c.astype(BF16)) + carry_ref[...]
    pos = jnp.where(c > 0.0, rank, -1.0)
    pos_ref[...] = pos
    post_ref[...] = pos.T[0:8, :]
    n = jnp.sum(c, axis=0, keepdims=True)
    cnt_ref[...] = jnp.broadcast_to(n, cnt_ref.shape)
    carry_ref[...] = carry_ref[...] + n


def _route(rt, ts):
    t = rt.shape[0]
    ns = t // ts
    return pl.pallas_call(
        functools.partial(_route_kernel, ts=ts),
        grid=(ns,),
        in_specs=[pl.BlockSpec((ts, LANES), lambda s: (s, 0))],
        out_specs=[pl.BlockSpec((ts, LANES), lambda s: (s, 0)), pl.BlockSpec((8, ts), lambda s: (0, s)),
                   pl.BlockSpec((None, 8, LANES), lambda s: (s, 0, 0))],
        out_shape=[jax.ShapeDtypeStruct((t, LANES), F32), jax.ShapeDtypeStruct((8, t), F32),
                   jax.ShapeDtypeStruct((ns, 8, LANES), F32)],
        scratch_shapes=[pltpu.VMEM((1, LANES), F32)],
        compiler_params=_params(("arbitrary",)),
        name="moe_route",
    )(rt)


def _moe_gather_kernel(texp_ref, trank_ref, slo_ref, nsrc_ref, post_ref, h_ref, o_ref, hbuf_ref, sem_ref, acc_ref, *, tm, ts):
    j = pl.program_id(0)
    e = texp_ref[j]
    r0 = trank_ref[j]
    s0 = slo_ref[j]
    n = nsrc_ref[j]
    acc_ref[...] = jnp.zeros(acc_ref.shape, F32)
    row = lax.broadcasted_iota(jnp.int32, (tm, 1), 0).astype(F32) + r0.astype(F32)

    def copy(s, slot):
        return pltpu.make_async_copy(h_ref.at[pl.ds(pl.multiple_of(s * ts, ts), ts), :], hbuf_ref.at[slot], sem_ref.at[slot])

    @pl.when(n > 0)
    def _():
        copy(s0, 0).start()

    def body(k, carry):
        slot = k % 2

        @pl.when(k + 1 < n)
        def _():
            copy(s0 + k + 1, 1 - slot).start()
        copy(s0 + k, slot).wait()
        prow = post_ref[pl.ds(e, 1), pl.ds(pl.multiple_of((s0 + k) * ts, ts), ts)]
        onehot = jnp.where(prow == row, 1.0, 0.0).astype(BF16)
        acc_ref[...] = acc_ref[...] + _dot(onehot, hbuf_ref[slot])
        return carry
    lax.fori_loop(0, n, body, 0)
    o_ref[...] = acc_ref[...].astype(o_ref.dtype)


def _moe_gather(tables, post, h, *, n_tiles, tm, ts):
    t, d = h.shape
    return pl.pallas_call(
        functools.partial(_moe_gather_kernel, tm=tm, ts=ts),
        grid_spec=pltpu.PrefetchScalarGridSpec(
            num_scalar_prefetch=4,
            grid=(n_tiles,),
            in_specs=[pl.BlockSpec(post.shape, lambda j, *_: (0, 0)), pl.BlockSpec(memory_space=pl.ANY)],
            out_specs=pl.BlockSpec((tm, d), lambda j, *_: (j, 0)),
            scratch_shapes=[pltpu.VMEM((2, ts, d), BF16), pltpu.SemaphoreType.DMA((2,)), pltpu.VMEM((tm, d), F32)],
        ),
        out_shape=jax.ShapeDtypeStruct((n_tiles * tm, d), BF16),
        compiler_params=_params(("arbitrary",)),
        name="moe_gather",
    )(*tables, post, h)


def _moe_ffn_kernel(texp_ref, h_ref, wg_ref, wu_ref, wd_ref, o_ref, acc_ref):
    f = pl.program_id(1)
    h = h_ref[...]
    a = _dot(h, wg_ref[...])
    b = _dot(h, wu_ref[...])
    d = _dot((a * _sigmoid(a) * b).astype(BF16), wd_ref[...])

    @pl.when(f == 0)
    def _():
        acc_ref[...] = d

    @pl.when(f > 0)
    def _():
        acc_ref[...] = acc_ref[...] + d

    @pl.when(f == pl.num_programs(1) - 1)
    def _():
        o_ref[...] = acc_ref[...].astype(o_ref.dtype)


def _moe_ffn(texp, hs, wg, wu, wd, *, tm, tf):
    rows, d = hs.shape
    ff = wg.shape[2]
    return pl.pallas_call(
        _moe_ffn_kernel,
        grid_spec=pltpu.PrefetchScalarGridSpec(
            num_scalar_prefetch=1,
            grid=(rows // tm, ff // tf),
            in_specs=[pl.BlockSpec((tm, d), lambda j, f, te: (j, 0)),
                      pl.BlockSpec((None, d, tf), lambda j, f, te: (te[j], 0, f)),
                      pl.BlockSpec((None, d, tf), lambda j, f, te: (te[j], 0, f)),
                      pl.BlockSpec((None, tf, d), lambda j, f, te: (te[j], f, 0))],
            out_specs=pl.BlockSpec((tm, d), lambda j, f, te: (j, 0)),
            scratch_shapes=[pltpu.VMEM((tm, d), F32)],
        ),
        out_shape=jax.ShapeDtypeStruct((rows, d), BF16),
        compiler_params=_params(("arbitrary", "arbitrary")),
        name="moe_ffn",
    )(texp, hs, wg, wu, wd)


def _moe_combine_kernel(start_ref, delta_ref, x_ref, rt_ref, pos_ref, ys_ref, o_ref, ybuf_ref, sem_ref, *, ts, win):
    s = pl.program_id(0)
    ns = pl.num_programs(0)
    slot = s % 2

    def copy(step, e, sl):
        st = pl.multiple_of(start_ref[step * N_EXPERTS + e], LANES)
        return pltpu.make_async_copy(ys_ref.at[pl.ds(st, win), :], ybuf_ref.at[sl, e], sem_ref.at[sl])

    def start_all(step, sl):
        for e in range(N_EXPERTS):
            copy(step, e, sl).start()

    @pl.when(s == 0)
    def _():
        start_all(s, slot)

    @pl.when(s + 1 < ns)
    def _():
        start_all(s + 1, 1 - slot)

    rt = rt_ref[...]
    pos = pos_ref[...]
    col = lax.broadcasted_iota(jnp.int32, (1, win), 1).astype(F32)
    for e in range(N_EXPERTS):
        copy(s, e, slot).wait()
    y = x_ref[...]
    for e in range(N_EXPERTS):
        rel = pos[:, e:e + 1] + delta_ref[s * N_EXPERTS + e].astype(F32)
        g = jnp.where(rel == col, 1.0, 0.0).astype(BF16)
        w = jnp.where(rt[:, 0:1] == float(e), rt[:, 2:3], 0.0) + jnp.where(rt[:, 1:2] == float(e), rt[:, 3:4], 0.0)
        y = y + w * _dot(g, ybuf_ref[slot, e])
    o_ref[...] = y


def _moe_combine(start, delta, x, rt, pos, ys, *, ts, win):
    t, d = x.shape
    row = lambda n: pl.BlockSpec((ts, n), lambda s, *_: (s, 0))
    return pl.pallas_call(
        functools.partial(_moe_combine_kernel, ts=ts, win=win),
        grid_spec=pltpu.PrefetchScalarGridSpec(
            num_scalar_prefetch=2,
            grid=(t // ts,),
            in_specs=[row(d), row(LANES), row(LANES), pl.BlockSpec(memory_space=pl.ANY)],
            out_specs=row(d),
            scratch_shapes=[pltpu.VMEM((2, N_EXPERTS, win, d), BF16), pltpu.SemaphoreType.DMA((2,))],
        ),
        out_shape=jax.ShapeDtypeStruct((t, d), F32),
        compiler_params=_params(("arbitrary",)),
        name="moe_combine",
    )(start, delta, x, rt, pos, ys)


def _moe(x, h, rt, wg, wu, wd, *, tm, ts, tf):
    t, d = x.shape
    win = ts + LANES
    n_tiles = -(-2 * t // tm) + N_EXPERTS + 1
    pos, post, cnt = _route(rt, ts)
    counts = cnt[:, 0, :N_EXPERTS].astype(jnp.int32)
    cum = jnp.concatenate([jnp.zeros((1, N_EXPERTS), jnp.int32), jnp.cumsum(counts, axis=0)], axis=0)
    tiles_e = -(-cum[-1] // tm)
    tstart = jnp.concatenate([jnp.zeros((1,), jnp.int32), jnp.cumsum(tiles_e)])
    jt = jnp.arange(n_tiles, dtype=jnp.int32)
    texp = jnp.minimum(jnp.sum(jt[:, None] >= tstart[None, 1:], axis=1), N_EXPERTS - 1).astype(jnp.int32)
    used = jt < tstart[-1]
    trank = jnp.where(used, (jt - tstart[texp]) * tm, -2 * tm).astype(jnp.int32)
    lo = cum[:-1].T[texp]
    hi = cum[1:].T[texp]
    hit = used[:, None] & (lo < trank[:, None] + tm) & (hi > trank[:, None])
    slo = jnp.sum(used[:, None] & (hi <= trank[:, None]), axis=1).astype(jnp.int32)
    nsrc = jnp.sum(hit, axis=1).astype(jnp.int32)
    row0 = tstart[:-1][None, :] * tm + cum[:-1]
    start = (row0 // LANES) * LANES
    delta = tstart[:-1][None, :] * tm - start
    hs = _moe_gather((texp, trank, slo, nsrc), post, h, n_tiles=n_tiles, tm=tm, ts=ts)
    ys = _moe_ffn(texp, hs, wg, wu, wd, tm=tm, tf=tf)
    return _moe_combine(start.reshape(-1).astype(jnp.int32), delta.reshape(-1).astype(jnp.int32), x, rt, pos, ys, ts=ts, win=win)


def _head_perm():
    idx = []
    for c in range(4):
        idx += list(range(HEAD_DIM * c, HEAD_DIM * (c + 1))) + list(range(HEAD_DIM * (4 + c), HEAD_DIM * (5 + c)))
    return np.asarray(idx, np.int32)


def _tap_matrix(n_cmp, nc_pad, n_blk, nb_pad):
    r_s = SEL_LEN // CMP_STRIDE
    r_c = CMP_LEN // CMP_STRIDE
    taps = np.convolve(np.ones(r_s), np.ones(r_c)) / r_c
    tm = np.zeros((nc_pad, nb_pad), np.float32)
    for j in range(n_blk):
        for kk, w in enumerate(taps):
            n = j * r_s + kk - (r_c - 1)
            if 0 <= n < n_cmp:
                tm[n, j] = w
    return tm


def _layer_weights(l, w_in, q_gain, k_gain, cmp_pe, cmp_w, w_out):
    perm = _head_perm()
    wl = w_in[l]
    q_end = NSA_WIDTH
    kv_end = q_end + 6 * NSA_GROUPS * HEAD_DIM
    gate_end = kv_end + 3 * NSA_HEADS
    d = wl.shape[0]
    w_pad = jnp.concatenate([wl[:, :q_end][:, perm], wl[:, q_end:kv_end], wl[:, kv_end:gate_end],
                             jnp.zeros((d, LANES - 3 * NSA_HEADS), wl.dtype), wl[:, gate_end:]], axis=1).astype(BF16)
    qg = jnp.tile(q_gain[l], NSA_HEADS)[None, :]
    kg_proj = jnp.stack([jnp.tile(k_gain[l, 1], NSA_GROUPS), jnp.tile(k_gain[l, 2], NSA_GROUPS)])
    kg_cmp = jnp.tile(k_gain[l, 0], NSA_GROUPS)[None, :]
    pairs = CMP_STRIDE // 2
    cw = cmp_w[l].reshape(2, 2, pairs, 2, HEAD_DIM, HEAD_DIM)
    eye = jnp.eye(NSA_GROUPS, dtype=cw.dtype)
    w_cmp = jnp.einsum('krjsde,gh->kjsgdrhe', cw, eye).reshape(2 * pairs * 2 * LANES, 2 * LANES).astype(BF16)
    pe = cmp_pe[l].reshape(2, 2, pairs, 2, 1, HEAD_DIM)
    pe = jnp.broadcast_to(pe, (2, 2, pairs, 2, NSA_GROUPS, HEAD_DIM)).reshape(4 * pairs, 2 * LANES)
    wo = w_out[l]
    wo = jnp.concatenate([wo[:NSA_WIDTH][perm], wo[NSA_WIDTH:]], axis=0).astype(BF16)
    return w_pad, qg, kg_proj, kg_cmp, w_cmp, pe, wo


def _token_minor(a):
    n = a.ndim
    return jnp.transpose(a, tuple(range(n - 4)) + (n - 3, n - 2, n - 1, n - 4))


def _token_major(a):
    n = a.ndim
    return jnp.transpose(a, tuple(range(n - 4)) + (n - 1, n - 4, n - 3, n - 2))


def kernel(x_prompt, x_sample, cache_kv, state_win_kv, state_hgrn, page_table, norm_mix, norm_ffn, w_in, q_gain, k_gain, cmp_pe, cmp_w, hgrn_lb_logits, hgrn_o_gain, w_out, ffn_w_gate, ffn_w_up, ffn_w_down, moe_router, moe_w_gate, moe_w_up, moe_w_down):
    depth = w_in.shape[0]
    batch, seq, d_model = x_prompt.shape
    db, ds, _ = x_sample.shape
    n_pool = cache_kv.shape[1]
    n_pages = page_table.shape[1]
    past = n_pages * PAGE_SIZE
    wb = state_win_kv.shape[2]
    assert wb == WINDOW and seq % 256 == 0 and seq >= WINDOW and ds == 8 and n_pages % 8 == 0

    tq, nkt_p, tc = 128, 256, 128
    tm_p = 256
    tm_f = 512 if (batch * seq) % 512 == 0 else 256
    tm_e = 512
    ts = db * ds
    d_ff = ffn_w_gate.shape[-1]
    tf = d_ff // 2

    sm = jax.nn.softmax(hgrn_lb_logits.astype(F32), axis=0)
    lower = jnp.concatenate([jnp.zeros_like(sm[:1]), jnp.cumsum(sm[1:], axis=0)], axis=0)
    seg = jnp.asarray(np.kron(np.eye(NSA_HEADS), np.ones((HEAD_DIM, HEAD_DIM))), BF16)
    seg1 = seg[:LANES, :LANES]

    nc_p = seq // CMP_STRIDE
    nb_p = -(-seq // SEL_LEN)
    tt_p = jnp.asarray(_tap_matrix(nc_p - 1, nc_p, nb_p, -(-nb_p // 8) * 8).T, BF16)
    nc_s = past // CMP_STRIDE
    nb_s = -(-(past + ds) // SEL_LEN)
    nbp_s = -(-nb_s // LANES) * LANES
    tm_s = jnp.asarray(_tap_matrix(nc_s - 1, nc_s, nb_s, nbp_s), BF16)

    cache_t = _token_minor(cache_kv).reshape(depth * n_pool * KV_COLS, PAGE_SIZE)
    wstate_t = _token_minor(state_win_kv).reshape(depth * db * 2 * LANES, wb)
    hstate = state_hgrn.reshape(depth * db, HGRN_HEADS, HGRN_DK, HGRN_DV)
    zero_state = jnp.zeros((batch, HGRN_HEADS, HGRN_DK, HGRN_DV), F32)

    xp = x_prompt.reshape(batch * seq, d_model)
    xs = x_sample.reshape(ts, d_model)
    kv_p, kv_s, win_p, win_s, hs_p, hs_s = [], [], [], [], [], []
    for l in range(depth):
        w_pad, qg, kg_proj, kg_cmp, w_cmp, pe, wo = _layer_weights(l, w_in, q_gain, k_gain, cmp_pe, cmp_w, w_out)
        lb = lower[l]
        hp = jnp.stack([jnp.log(lb), jnp.log1p(-lb), 1.0 - lb])
        og = hgrn_o_gain[l][None, :]
        g1 = norm_mix[l][None, :]
        g2 = norm_ffn[l][None, :]
        i = l // 2
        if l % 2 == 0:
            router = None
            wg, wu, wd = ffn_w_gate[i].astype(BF16), ffn_w_up[i].astype(BF16), ffn_w_down[i].astype(BF16)
        else:
            router = jnp.pad(moe_router[i], ((0, 0), (0, LANES - N_EXPERTS)))
            wg, wu, wd = moe_w_gate[i].astype(BF16), moe_w_up[i].astype(BF16), moe_w_down[i].astype(BF16)

        def mixer(x, h, rt):
            tm = tm_f if x.shape[0] % tm_f == 0 else x.shape[0]
            if router is None:
                return _ffn(x, h, wg, wu, wd, tm, tf)
            return _moe(x, h, rt, wg, wu, wd, tm=tm_e, ts=min(tm_p, x.shape[0]), tf=tf)

        qt, rows_t, win_t, kb, vt, kc, vc, gt, hq, hlf, hk, hv, hg = _proj_prompt(xp, g1, w_pad, qg, kg_proj, seg, hp, batch=batch, seq=seq, tm=tm_p)
        pp = _compress_prompt(kc, vc, pe, w_cmp, batch=batch, seq=seq)
        o_nsa = _nsa_prompt(qt, kb, vt, pp, gt, kg_cmp, seg1, tt_p, batch=batch, seq=seq, tq=tq, nkt=nkt_p)
        o_hg, s_fin = _hgrn(hq, hlf, hk, hv, hg, og, zero_state, batch=batch, seq=seq, tc=tc, s0_base=0)
        res = _outproj(xp, o_nsa, o_hg, wo, g2, router, tm_p)
        xp = mixer(res[0], res[1], res[2] if router is not None else None)
        kv_p.append(rows_t.reshape(batch, 4, NSA_GROUPS, HEAD_DIM, seq))
        win_p.append(win_t.reshape(batch, 2, NSA_GROUPS, HEAD_DIM, seq)[..., seq - min(WINDOW, seq):])
        hs_p.append(s_fin)

        q, rows, win, gates, hq, hlf, hk, hv, hg = _proj_sample(xs, g1, w_pad, qg, kg_proj, seg, hp)
        ps = _compress_sample(page_table, cache_t, pe, w_cmp, page_base=l * n_pool)
        o_nsa = _nsa_sample(page_table, q, ps, rows, win, wstate_t, gates, kg_cmp, seg1, tm_s, cache_t,
                            tq=ds, pages_per_tile=8, page_base=l * n_pool, wstate_base=l * db)
        padt = lambda a: jnp.pad(a.reshape(db, ds, HGRN_WIDTH), ((0, 0), (0, tc - ds), (0, 0))).reshape(db * tc, HGRN_WIDTH)
        o_hg, s_new = _hgrn(padt(hq), padt(hlf), padt(hk), padt(hv), padt(hg), og, hstate, batch=db, seq=tc, tc=tc, s0_base=l * db)
        o_hg = o_hg.reshape(db, tc, HGRN_WIDTH)[:, :ds].reshape(ts, HGRN_WIDTH)
        res = _outproj(xs, o_nsa.astype(BF16), o_hg, wo, g2, router, ts)
        xs = mixer(res[0], res[1], res[2] if router is not None else None)
        kv_s.append(rows.reshape(db, ds, 4, NSA_GROUPS, HEAD_DIM))
        win_s.append(win.reshape(db, ds, 2, NSA_GROUPS, HEAD_DIM))
        hs_s.append(s_new)

    new_win_sample = jnp.concatenate([state_win_kv[:, :, ds:], jnp.stack(win_s)], axis=2)
    return (xp.reshape(batch, seq, d_model), xs.reshape(db, ds, d_model), _token_major(jnp.stack(kv_p)), jnp.stack(kv_s),
            _token_major(jnp.stack(win_p)), new_win_sample, jnp.stack(hs_p), jnp.stack(hs_s))
```

```python
import functools

import numpy as np
import jax
import jax.numpy as jnp
from jax import lax
from jax.experimental import pallas as pl
from jax.experimental.pallas import tpu as pltpu

F32 = jnp.float32
BF16 = jnp.bfloat16

NSA_HEADS = 8
NSA_GROUPS = 2
HEAD_DIM = 64
NSA_WIDTH = NSA_HEADS * HEAD_DIM
CMP_LEN = 32
CMP_STRIDE = 16
SEL_LEN = 64
N_SEL = 16
N_LOCAL = 2
WINDOW = 512
FORCE_BONUS = 1e4
HGRN_HEADS = 4
HGRN_DK = 128
HGRN_DV = 128
HGRN_WIDTH = HGRN_HEADS * HGRN_DV
N_EXPERTS = 8
EPS = 1e-6
NEG_INF = -1e30
PAGE_SIZE = 128

LANES = 128
KV_COLS = 4 * NSA_GROUPS * HEAD_DIM
CMP_PITCH = CMP_STRIDE + 1
VMEM_LIMIT = 48 * 1024 * 1024


def _log2(n):
    assert n & (n - 1) == 0
    return n.bit_length() - 1


def _dot(a, b):
    return jnp.dot(a, b, preferred_element_type=F32)


def _dot_nt(a, b):
    return lax.dot_general(a, b, (((1,), (1,)), ((), ())), preferred_element_type=F32)


def _dot_tn(a, b):
    return lax.dot_general(a, b, (((0,), (0,)), ((), ())), preferred_element_type=F32)


def _split2(x):
    hi = x.astype(BF16)
    lo = (x - hi.astype(F32)).astype(BF16)
    return hi, lo


def _split3(x):
    hi = x.astype(BF16)
    r = x - hi.astype(F32)
    mid = r.astype(BF16)
    lo = (r - mid.astype(F32)).astype(BF16)
    return hi, mid, lo


def _segsum(x, seg):
    hi, lo = _split2(x)
    return _dot(hi, seg) + _dot(lo, seg)


def _seg_rms(z, gain, seg):
    ss = _segsum(z * z, seg) * (1.0 / HEAD_DIM)
    return z * lax.rsqrt(ss + EPS) * gain


def _sigmoid(x):
    return 1.0 / (1.0 + jnp.exp(-x))


def _params(sem, vmem=VMEM_LIMIT):
    return pltpu.CompilerParams(dimension_semantics=sem, vmem_limit_bytes=vmem)


def _proj_core(x_ref, gain_ref, w_ref, qg_ref, kg_ref, seg_ref, hp_ref, hq_ref, hlf_ref, hk_ref, hv_ref, hg_ref):
    x = x_ref[...]
    ms = jnp.mean(x * x, axis=-1, keepdims=True)
    h = (x * lax.rsqrt(ms + EPS) * gain_ref[...]).astype(BF16)

    def mm(c0, c1):
        return _dot(h, w_ref[:, c0:c1])

    seg1 = seg_ref[0:LANES, 0:LANES]
    qn = _seg_rms(mm(0, NSA_WIDTH), qg_ref[...], seg_ref[...]) * (HEAD_DIM ** -0.5)
    c0 = NSA_WIDTH
    kv = mm(c0, c0 + 6 * LANES)
    kvs = [kv[:, LANES * j:LANES * (j + 1)] for j in range(6)]
    kvs[2] = _seg_rms(kvs[2], kg_ref[0:1, :], seg1)
    kvs[4] = _seg_rms(kvs[4], kg_ref[1:2, :], seg1)
    c0 += 6 * LANES
    gates = _sigmoid(mm(c0, c0 + LANES))
    c0 += LANES
    zq = mm(c0, c0 + HGRN_WIDTH)
    hq_ref[...] = zq * _sigmoid(zq)
    c0 += HGRN_WIDTH
    zf = mm(c0, c0 + HGRN_WIDTH)
    log_sig = jnp.minimum(zf, 0.0) - jnp.log1p(jnp.exp(-jnp.abs(zf)))
    a = hp_ref[0:1, :]
    c = hp_ref[1:2, :] + log_sig
    hlf_ref[...] = jnp.maximum(a, c) + jnp.log1p(jnp.exp(-jnp.abs(a - c)))
    hk_ref[...] = hp_ref[2:3, :] * _sigmoid(-zf)
    c0 += HGRN_WIDTH
    hv_ref[...] = mm(c0, c0 + HGRN_WIDTH)
    c0 += HGRN_WIDTH
    zg = mm(c0, c0 + HGRN_WIDTH)
    hg_ref[...] = zg * _sigmoid(zg)
    return qn, kvs, gates


def _proj_prompt_kernel(x_ref, gain_ref, w_ref, qg_ref, kg_ref, seg_ref, hp_ref,
                        qt_ref, rowst_ref, wint_ref, kb_ref, vt_ref, kc_ref, vc_ref, gt_ref,
                        hq_ref, hlf_ref, hk_ref, hv_ref, hg_ref):
    qn, kvs, gates = _proj_core(x_ref, gain_ref, w_ref, qg_ref, kg_ref, seg_ref, hp_ref, hq_ref, hlf_ref, hk_ref, hv_ref, hg_ref)
    for c in range(4):
        qt_ref[LANES * c:LANES * (c + 1), :] = qn[:, LANES * c:LANES * (c + 1)].T.astype(BF16)
    kvt = [a.T for a in kvs]
    for j in range(4):
        rowst_ref[LANES * j:LANES * (j + 1), :] = kvt[j]
    wint_ref[0:LANES, :] = kvt[4]
    wint_ref[LANES:2 * LANES, :] = kvt[5]
    kb_ref[:, 0:LANES] = kvs[2].astype(BF16)
    kb_ref[:, LANES:2 * LANES] = kvs[4].astype(BF16)
    vt_ref[0:LANES, :] = kvt[3].astype(BF16)
    vt_ref[LANES:2 * LANES, :] = kvt[5].astype(BF16)
    kc_ref[...] = kvs[0]
    vc_ref[...] = kvs[1]
    gt_ref[...] = gates.T


def _proj_sample_kernel(x_ref, gain_ref, w_ref, qg_ref, kg_ref, seg_ref, hp_ref,
                        q_ref, rows_ref, win_ref, gates_ref, hq_ref, hlf_ref, hk_ref, hv_ref, hg_ref):
    qn, kvs, gates = _proj_core(x_ref, gain_ref, w_ref, qg_ref, kg_ref, seg_ref, hp_ref, hq_ref, hlf_ref, hk_ref, hv_ref, hg_ref)
    lane = lax.broadcasted_iota(jnp.int32, (qn.shape[0], LANES), 1)
    for c in range(4):
        blk = qn[:, LANES * c:LANES * (c + 1)]
        q_ref[:, LANES * c:LANES * (c + 1)] = jnp.where(lane < HEAD_DIM, blk, 0.0)
        q_ref[:, LANES * (4 + c):LANES * (5 + c)] = jnp.where(lane >= HEAD_DIM, blk, 0.0)
    for j in range(4):
        rows_ref[:, LANES * j:LANES * (j + 1)] = kvs[j]
    win_ref[:, 0:LANES] = kvs[4]
    win_ref[:, LANES:2 * LANES] = kvs[5]
    gates_ref[...] = gates


def _proj_prompt(x, gain, w, qg, kg, seg, hp, *, batch, seq, tm):
    t, d = x.shape
    nt = seq // tm
    const = lambda shape: pl.BlockSpec(shape, lambda b, i: (0,) * len(shape))
    row = lambda n: pl.BlockSpec((tm, n), lambda b, i: (b * nt + i, 0))
    colt = lambda n: pl.BlockSpec((None, n, tm), lambda b, i: (b, 0, i))
    tshape = lambda n, dt: jax.ShapeDtypeStruct((batch, n, seq), dt)
    rshape = lambda n, dt: jax.ShapeDtypeStruct((t, n), dt)
    return pl.pallas_call(
        _proj_prompt_kernel,
        grid=(batch, nt),
        in_specs=[row(d), const((1, d)), const(w.shape), const(qg.shape), const(kg.shape), const(seg.shape), const(hp.shape)],
        out_specs=[colt(NSA_WIDTH), colt(KV_COLS), colt(2 * LANES), row(2 * LANES), colt(2 * LANES), row(LANES), row(LANES),
                   colt(LANES)] + [row(HGRN_WIDTH)] * 5,
        out_shape=[tshape(NSA_WIDTH, BF16), tshape(KV_COLS, F32), tshape(2 * LANES, F32), rshape(2 * LANES, BF16),
                   tshape(2 * LANES, BF16), rshape(LANES, F32), rshape(LANES, F32), tshape(LANES, F32)]
                  + [rshape(HGRN_WIDTH, F32)] * 5,
        compiler_params=_params(("arbitrary", "arbitrary")),
        name="proj_prompt",
    )(x, gain, w, qg, kg, seg, hp)


def _proj_sample(x, gain, w, qg, kg, seg, hp):
    t, d = x.shape
    const = lambda shape: pl.BlockSpec(shape, lambda i: (0,) * len(shape))
    row = lambda n: pl.BlockSpec((t, n), lambda i: (0, 0))
    outs = [2 * NSA_WIDTH, KV_COLS, 2 * LANES, LANES] + [HGRN_WIDTH] * 5
    return pl.pallas_call(
        _proj_sample_kernel,
        grid=(1,),
        in_specs=[row(d), const((1, d)), const(w.shape), const(qg.shape), const(kg.shape), const(seg.shape), const(hp.shape)],
        out_specs=[row(n) for n in outs],
        out_shape=[jax.ShapeDtypeStruct((t, n), F32) for n in outs],
        compiler_params=_params(("arbitrary",)),
        name="proj_sample",
    )(x, gain, w, qg, kg, seg, hp)


def _compress_rows(k_ref, v_ref, pe_ref, w_ref, out_ref, m, pitch=CMP_STRIDE):
    pairs = CMP_STRIDE // 2
    for kind, ref in ((0, k_ref), (1, v_ref)):
        acc = None
        bias = None
        for j in range(pairs):
            w = w_ref[(kind * pairs + j) * 2 * LANES:(kind * pairs + j + 1) * 2 * LANES, :]
            xa = ref[pl.ds(2 * j, m, stride=pitch), :]
            xb = ref[pl.ds(2 * j + 1, m, stride=pitch), :]
            d = _dot(jnp.concatenate([xa, xb], axis=1).astype(BF16), w)
            acc = d if acc is None else acc + d
            pe0 = pe_ref[(kind * 2) * pairs + j:(kind * 2) * pairs + j + 1, :]
            pe1 = pe_ref[(kind * 2 + 1) * pairs + j:(kind * 2 + 1) * pairs + j + 1, :]
            pel = jnp.concatenate([jnp.broadcast_to(pe0, (8, 2 * LANES)), jnp.broadcast_to(pe1, (8, 2 * LANES))], axis=0)
            pb = _dot(pel.astype(BF16), w)
            bias = pb if bias is None else bias + pb
        out_ref[:, kind * 2 * LANES:kind * 2 * LANES + LANES] = acc[:, 0:LANES] + bias[0:1, 0:LANES]
        out_ref[:, kind * 2 * LANES + LANES:(kind + 1) * 2 * LANES] = acc[:, LANES:2 * LANES] + bias[8:9, LANES:2 * LANES]


def _compress_prompt_kernel(k_ref, v_ref, pe_ref, w_ref, out_ref, *, m):
    _compress_rows(k_ref, v_ref, pe_ref, w_ref, out_ref, m)


def _compress_prompt(kc, vc, pe, w, *, batch, seq):
    m = seq // CMP_STRIDE
    const = lambda shape: pl.BlockSpec(shape, lambda i: (0,) * len(shape))
    return pl.pallas_call(
        functools.partial(_compress_prompt_kernel, m=m),
        grid=(batch,),
        in_specs=[pl.BlockSpec((seq, LANES), lambda i: (i, 0)), pl.BlockSpec((seq, LANES), lambda i: (i, 0)), const(pe.shape), const(w.shape)],
        out_specs=pl.BlockSpec((m, 4 * LANES), lambda i: (i, 0)),
        out_shape=jax.ShapeDtypeStruct((batch * m, 4 * LANES), F32),
        compiler_params=_params(("arbitrary",)),
        name="compress_prompt",
    )(kc, vc, pe, w)


def _page_copy(pt_ref, cache_ref, buf_ref, sem_ref, b, p, slot, *, page_base, kind0):
    page = pt_ref[b, p] + page_base
    return pltpu.make_async_copy(
        cache_ref.at[pl.ds(page * KV_COLS + kind0 * LANES, 2 * LANES), :],
        buf_ref.at[slot, pl.ds(p * 2 * LANES, 2 * LANES), :],
        sem_ref.at[slot])


def _page_pipeline(copy, b, nb, n_pages):
    slot = b % 2

    def start_all(s, sl):
        def body(p, carry):
            copy(s, p, sl).start()
            return carry
        lax.fori_loop(0, n_pages, body, 0)

    @pl.when(b == 0)
    def _():
        start_all(b, slot)

    @pl.when(b + 1 < nb)
    def _():
        start_all(b + 1, 1 - slot)

    def wait_all():
        def body(p, carry):
            copy(b, p, slot).wait()
            return carry
        lax.fori_loop(0, n_pages, body, 0)
    return wait_all


def _compress_sample_kernel(pt_ref, cache_ref, pe_ref, w_ref, out_ref, buf_ref, sem_ref, ktok_ref, vtok_ref, *, n_pages, page_base):
    b = pl.program_id(0)
    slot = b % 2
    copy = functools.partial(_page_copy, pt_ref, cache_ref, buf_ref, sem_ref, page_base=page_base, kind0=0)
    wait_all = _page_pipeline(copy, b, pl.num_programs(0), n_pages)
    wait_all()

    chunks = PAGE_SIZE // CMP_STRIDE

    def tr_body(p, carry):
        r0 = pl.multiple_of(p * 2 * LANES, 2 * LANES)
        t0 = pl.multiple_of(p * chunks * CMP_PITCH, 8)
        kt = buf_ref[slot, pl.ds(r0, LANES), :].T
        vt = buf_ref[slot, pl.ds(r0 + LANES, LANES), :].T
        for c in range(chunks):
            ktok_ref[pl.ds(t0 + c * CMP_PITCH, CMP_STRIDE), :] = kt[c * CMP_STRIDE:(c + 1) * CMP_STRIDE]
            vtok_ref[pl.ds(t0 + c * CMP_PITCH, CMP_STRIDE), :] = vt[c * CMP_STRIDE:(c + 1) * CMP_STRIDE]
        return carry
    lax.fori_loop(0, n_pages, tr_body, 0)
    _compress_rows(ktok_ref, vtok_ref, pe_ref, w_ref, out_ref, n_pages * chunks, pitch=CMP_PITCH)


def _compress_sample(page_table, cache_t, pe, w, *, page_base):
    db, n_pages = page_table.shape
    past = n_pages * PAGE_SIZE
    m = past // CMP_STRIDE
    const = lambda shape: pl.BlockSpec(shape, lambda i, pt: (0,) * len(shape))
    kern = functools.partial(_compress_sample_kernel, n_pages=n_pages, page_base=page_base)
    return pl.pallas_call(
        kern,
        grid_spec=pltpu.PrefetchScalarGridSpec(
            num_scalar_prefetch=1,
            grid=(db,),
            in_specs=[pl.BlockSpec(memory_space=pl.ANY), const(pe.shape), const(w.shape)],
            out_specs=pl.BlockSpec((m, 4 * LANES), lambda i, pt: (i, 0)),
            scratch_shapes=[pltpu.VMEM((2, n_pages * 2 * LANES, LANES), F32), pltpu.SemaphoreType.DMA((2,)),
                            pltpu.VMEM((m * CMP_PITCH, LANES), F32), pltpu.VMEM((m * CMP_PITCH, LANES), F32)],
        ),
        out_shape=jax.ShapeDtypeStruct((db * m, 4 * LANES), F32),
        compiler_params=_params(("arbitrary",)),
        name="compress_sample",
    )(page_table, cache_t, pe, w)


def _compressed_kv(p_all, kg, seg1):
    nc = p_all.shape[0]
    up = lambda a: pltpu.roll(a, nc - 1, 0)
    kc_raw = p_all[:, 0:LANES] + up(p_all[:, LANES:2 * LANES])
    vc = p_all[:, 2 * LANES:3 * LANES] + up(p_all[:, 3 * LANES:4 * LANES])
    return _seg_rms(kc_raw, kg, seg1), vc


def _head_slope(h):
    return 2.0 ** -(h + 1)


def _nsa_prompt_kernel(qt_ref, k_ref, vt_ref, p_ref, gt_ref, kg_ref, seg_ref, tt_ref, o_ref,
                       sc_ref, sel_ref, m_ref, l_ref, acc_ref, *, tq, nkt, n_blk):
    q0 = pl.program_id(1) * tq
    q_last = q0 + tq - 1
    hpg = NSA_HEADS // NSA_GROUPS
    lane_blk = lambda a, h: a[:, h * tq:(h + 1) * tq]

    row = lax.broadcasted_iota(jnp.int32, (LANES, tq), 0)
    cols = [None] * NSA_HEADS
    for c in range(4):
        blk = qt_ref[LANES * c:LANES * (c + 1), :]
        cols[c] = jnp.where(row < HEAD_DIM, blk, jnp.zeros_like(blk))
        cols[4 + c] = jnp.where(row >= HEAD_DIM, blk, jnp.zeros_like(blk))
    qpt = jnp.concatenate(cols, axis=1)

    qpos = q0 + lax.broadcasted_iota(jnp.int32, (1, tq), 1)
    qpos_f = qpos.astype(F32)

    kc, vc = _compressed_kv(p_ref[...], kg_ref[...], seg_ref[...])
    nc = kc.shape[0]
    s_c = _dot(kc.astype(BF16), qpt)
    cstart = lax.broadcasted_iota(jnp.int32, (nc, 1), 0) * CMP_STRIDE
    cdist = jnp.abs(qpos_f - (cstart.astype(F32) + 0.5 * (CMP_LEN - 1)))
    c_ok = (cstart + (CMP_LEN - 1)) <= qpos
    any_ok = (qpos >= CMP_LEN - 1).astype(F32)
    ps = []
    for h in range(NSA_HEADS):
        s = jnp.where(c_ok, lane_blk(s_c, h) - _head_slope(h) * cdist, NEG_INF)
        e = jnp.exp(s - jnp.max(s, axis=0, keepdims=True))
        ps.append(e * (any_ok / jnp.sum(e, axis=0, keepdims=True)))
    o_cmp = _dot(vc.T.astype(BF16), jnp.concatenate(ps, axis=1).astype(BF16))

    imps = []
    for g in range(NSA_GROUPS):
        acc = ps[g * hpg]
        for h in range(1, hpg):
            acc = acc + ps[g * hpg + h]
        imps.append(acc)
    tt = tt_ref[...]
    blk_t = None
    for part in _split3(jnp.concatenate(imps, axis=1)):
        d = _dot(tt, part)
        blk_t = d if blk_t is None else blk_t + d
    nbp = tt.shape[0]
    j_t = lax.broadcasted_iota(jnp.int32, (nbp, 2 * tq), 0)
    qcol = lax.broadcasted_iota(jnp.int32, (nbp, 2 * tq), 1) & (tq - 1)
    back = ((q0 + qcol) >> _log2(SEL_LEN)) - j_t
    forced = (j_t == 0) | ((back >= 0) & (back < N_LOCAL))
    sc_ref[...] = jnp.where(back >= 0, blk_t + jnp.where(forced, FORCE_BONUS, 0.0), -1.0)

    def rank_body(i, cnt):
        r = sc_ref[pl.ds(i, 1), :]
        sc = sc_ref[...]
        return cnt + jnp.where(j_t > i, (r >= sc).astype(F32), (r > sc).astype(F32))
    cnt = lax.fori_loop(0, jnp.minimum(q_last // SEL_LEN + 1, n_blk), rank_body, jnp.zeros((nbp, 2 * tq), F32))
    sel_ref[...] = jnp.where(cnt < N_SEL, 0.0, NEG_INF)

    def flash_init():
        m_ref[...] = jnp.full(m_ref.shape, NEG_INF, F32)
        l_ref[...] = jnp.zeros(l_ref.shape, F32)
        acc_ref[...] = jnp.zeros(acc_ref.shape, F32)

    def flash_tile(s_t, dist, mb, vt):
        m_old = m_ref[...]
        l_old = l_ref[...]
        pbs, ms, ls, als = [], [], [], []
        for h in range(NSA_HEADS):
            t = lane_blk(s_t, h) - _head_slope(h) * dist + mb[h // hpg]
            m_h = jnp.maximum(lane_blk(m_old, h), jnp.max(t, axis=0, keepdims=True))
            al = jnp.exp(lane_blk(m_old, h) - m_h)
            p = jnp.exp(t - m_h)
            ls.append(al * lane_blk(l_old, h) + jnp.sum(p, axis=0, keepdims=True))
            pbs.append(p.astype(BF16))
            ms.append(m_h)
            als.append(al)
        acc_ref[...] = acc_ref[...] * jnp.concatenate(als, axis=1) + _dot(vt, jnp.concatenate(pbs, axis=1))
        m_ref[...] = jnp.concatenate(ms, axis=1)
        l_ref[...] = jnp.concatenate(ls, axis=1)

    flash_init()
    d0 = (lax.broadcasted_iota(jnp.int32, (nkt, tq), 1) - lax.broadcasted_iota(jnp.int32, (nkt, tq), 0)).astype(F32)

    def sel_tile(kt):
        k0 = pl.multiple_of(kt * nkt, nkt)
        dist = d0 + jnp.asarray(q0 - k0, F32)
        causal = jnp.where(dist < 0.0, NEG_INF, 0.0)
        j0 = k0 // SEL_LEN
        rows = [jnp.broadcast_to(sel_ref[pl.ds(j0 + b, 1), :], (SEL_LEN, 2 * tq)) for b in range(nkt // SEL_LEN)]
        mrow = jnp.concatenate(rows, axis=0)
        mb = [mrow[:, 0:tq] + causal, mrow[:, tq:2 * tq] + causal]
        s_t = _dot(k_ref[pl.ds(k0, nkt), 0:LANES], qpt)
        flash_tile(s_t, dist, mb, vt_ref[0:LANES, pl.ds(k0, nkt)])

    def sel_pair(kp, carry):
        sel_tile(2 * kp)
        sel_tile(2 * kp + 1)
        return carry
    n_kt = q_last // nkt + 1
    lax.fori_loop(0, n_kt // 2, sel_pair, 0)

    @pl.when(n_kt % 2 == 1)
    def _():
        sel_tile(n_kt - 1)
    o_sel = acc_ref[...] * (1.0 / l_ref[...])

    flash_init()
    def win_tile(off, nk):
        ks = pl.multiple_of(q0 - off, tq)
        dist = (lax.broadcasted_iota(jnp.int32, (nk, tq), 1) - lax.broadcasted_iota(jnp.int32, (nk, tq), 0)).astype(F32) + float(off)
        wmask = jnp.where((dist >= 0.0) & (dist < float(WINDOW)), 0.0, NEG_INF)
        s_t = _dot(k_ref[pl.ds(ks, nk), LANES:2 * LANES], qpt)
        flash_tile(s_t, dist, [wmask, wmask], vt_ref[LANES:2 * LANES, pl.ds(ks, nk)])

    @pl.when(q0 >= WINDOW)
    def _():
        for off in range(WINDOW, 0, -2 * tq):
            win_tile(off, 2 * tq)
        win_tile(0, tq)

    @pl.when(q0 < WINDOW)
    def _():
        for off in range(WINDOW, -1, -tq):
            @pl.when(q0 >= off)
            def _():
                win_tile(off, tq)
    o_win = acc_ref[...] * (1.0 / l_ref[...])

    gt = gt_ref[...]
    for c in range(4):
        halves = []
        for h, lo in ((c, 0), (4 + c, HEAD_DIM)):
            g = lambda br: gt[br * NSA_HEADS + h:br * NSA_HEADS + h + 1, :]
            pick = lambda a: a[lo:lo + HEAD_DIM, h * tq:(h + 1) * tq]
            halves.append(g(0) * pick(o_cmp) + g(1) * pick(o_sel) + g(2) * pick(o_win))
        o_ref[:, LANES * c:LANES * (c + 1)] = jnp.concatenate(halves, axis=0).T.astype(o_ref.dtype)


def _nsa_prompt(qt, kb, vt, p, gt, kg, seg1, tt, *, batch, seq, tq, nkt):
    nq = seq // tq
    nc = seq // CMP_STRIDE
    n_blk = -(-seq // SEL_LEN)
    r = NSA_HEADS * tq
    nbp = tt.shape[0]
    const = lambda shape: pl.BlockSpec(shape, lambda b, i: (0,) * len(shape))
    kern = functools.partial(_nsa_prompt_kernel, tq=tq, nkt=nkt, n_blk=n_blk)
    return pl.pallas_call(
        kern,
        grid=(batch, nq),
        in_specs=[
            pl.BlockSpec((None, NSA_WIDTH, tq), lambda b, i: (b, 0, i)),
            pl.BlockSpec((seq, 2 * LANES), lambda b, i: (b, 0)),
            pl.BlockSpec((None, 2 * LANES, seq), lambda b, i: (b, 0, 0)),
            pl.BlockSpec((nc, 4 * LANES), lambda b, i: (b, 0)),
            pl.BlockSpec((None, LANES, tq), lambda b, i: (b, 0, i)),
            const(kg.shape), const(seg1.shape), const(tt.shape),
        ],
        out_specs=pl.BlockSpec((tq, NSA_WIDTH), lambda b, i: (b * nq + i, 0)),
        out_shape=jax.ShapeDtypeStruct((batch * seq, NSA_WIDTH), BF16),
        scratch_shapes=[pltpu.VMEM((nbp, 2 * tq), F32), pltpu.VMEM((nbp, 2 * tq), F32),
                        pltpu.VMEM((1, r), F32), pltpu.VMEM((1, r), F32), pltpu.VMEM((LANES, r), F32)],
        compiler_params=_params(("arbitrary", "arbitrary")),
        name="nsa_prompt",
    )(qt, kb, vt, p, gt, kg, seg1, tt)


def _row_meta(tq, q0):
    r = NSA_HEADS * tq
    rid = lax.broadcasted_iota(jnp.int32, (r, 1), 0)
    hh = rid >> _log2(tq)
    ii = rid & (tq - 1)
    slope = lax.bitcast_convert_type((126 - hh) << 23, F32)
    qpos = q0 + ii
    return ii, slope, qpos


def _flash_init(m_ref, l_ref, acc_ref):
    m_ref[...] = jnp.full(m_ref.shape, NEG_INF, F32)
    l_ref[...] = jnp.zeros(l_ref.shape, F32)
    acc_ref[...] = jnp.zeros(acc_ref.shape, F32)


def _flash_update(m_ref, l_ref, acc_ref, s, pv):
    m_old = m_ref[...]
    m_new = jnp.maximum(m_old, jnp.max(s, axis=-1, keepdims=True))
    alpha = jnp.exp(m_old - m_new)
    p = jnp.exp(s - m_new)
    l_ref[...] = alpha * l_ref[...] + jnp.sum(p, axis=-1, keepdims=True)
    acc_ref[...] = alpha * acc_ref[...] + pv(p.astype(BF16))
    m_ref[...] = m_new


def _expand_mask(mask2b, blk0, nkt, tq):
    nbp = mask2b.shape[1]
    j_e = lax.broadcasted_iota(jnp.int32, (nbp, nkt), 0)
    c_e = lax.broadcasted_iota(jnp.int32, (nbp, nkt), 1)
    e = (j_e == blk0 + (c_e >> _log2(SEL_LEN))).astype(BF16)
    me2 = _dot(mask2b, e)
    hpg = NSA_HEADS // NSA_GROUPS
    return jnp.concatenate([me2[0:tq]] * hpg + [me2[tq:2 * tq]] * hpg, axis=0)


def _nsa_sample_kernel(pt_ref, q_ref, p_ref, rows_ref, wnew_ref, wst_ref, gates_ref, kg_ref, seg_ref, tmat_ref, cache_ref,
                       o_ref, buf_ref, sem_ref, m_ref, l_ref, acc_ref, *, tq, pages_per_tile, n_pages, page_base):
    b = pl.program_id(0)
    slot = b % 2
    past = n_pages * PAGE_SIZE
    wb = wst_ref.shape[1]
    hpg = NSA_HEADS // NSA_GROUPS
    copy = functools.partial(_page_copy, pt_ref, cache_ref, buf_ref, sem_ref, page_base=page_base, kind0=2)
    wait_all = _page_pipeline(copy, b, pl.num_programs(0), n_pages)

    r = NSA_HEADS * tq
    qp = jnp.concatenate([q_ref[:, LANES * h:LANES * (h + 1)] for h in range(NSA_HEADS)], axis=0).astype(BF16)
    ii, slope, qpos = _row_meta(tq, past)

    kc, vc = _compressed_kv(p_ref[...], kg_ref[...], seg_ref[...])
    nc = kc.shape[0]
    s = _dot_nt(qp, kc.astype(BF16))
    cstart = lax.broadcasted_iota(jnp.int32, (1, nc), 1) * CMP_STRIDE
    cdist = jnp.abs(qpos.astype(F32) - (cstart.astype(F32) + 0.5 * (CMP_LEN - 1)))
    c_ok = (cstart + (CMP_LEN - 1)) <= qpos
    s = jnp.where(c_ok, s - slope * cdist, NEG_INF)
    e = jnp.exp(s - jnp.max(s, axis=-1, keepdims=True))
    p_cmp = e / jnp.sum(e, axis=-1, keepdims=True) * (qpos >= CMP_LEN - 1).astype(F32)
    o_cmp = _dot(p_cmp.astype(BF16), vc.astype(BF16))

    imps = []
    for g in range(NSA_GROUPS):
        acc = p_cmp[(g * hpg) * tq:(g * hpg + 1) * tq]
        for h in range(1, hpg):
            acc = acc + p_cmp[(g * hpg + h) * tq:(g * hpg + h + 1) * tq]
        imps.append(acc)
    tmat = tmat_ref[...]
    blk = None
    for part in _split3(jnp.concatenate(imps, axis=0)):
        d = _dot(part, tmat)
        blk = d if blk is None else blk + d
    nbp = tmat.shape[1]
    n_blk = -(-(past + tq) // SEL_LEN)
    j_l = lax.broadcasted_iota(jnp.int32, (2 * tq, nbp), 1)
    qrow = lax.broadcasted_iota(jnp.int32, (2 * tq, nbp), 0) & (tq - 1)
    back = ((past + qrow) >> _log2(SEL_LEN)) - j_l
    forced = (j_l == 0) | ((back >= 0) & (back < N_LOCAL))
    score = jnp.where(back >= 0, blk + jnp.where(forced, FORCE_BONUS, 0.0), -1.0)
    cnt = jnp.zeros((2 * tq, nbp), F32)
    for i in range(n_blk):
        col = score[:, i:i + 1]
        cnt = cnt + jnp.where(j_l > i, (col >= score).astype(F32), (col > score).astype(F32))
    mask2 = (cnt < N_SEL).astype(F32)
    mask2b = mask2.astype(BF16)

    wait_all()

    _flash_init(m_ref, l_ref, acc_ref)
    nkt = pages_per_tile * PAGE_SIZE
    dist0 = (past + ii - lax.broadcasted_iota(jnp.int32, (r, nkt), 1)).astype(F32)

    def sel_body(kt, carry):
        k0 = kt * nkt
        tiles = []
        for i in range(pages_per_tile):
            r0 = pl.multiple_of((kt * pages_per_tile + i) * 2 * LANES, 2 * LANES)
            tiles.append((buf_ref[slot, pl.ds(r0, LANES), :].astype(BF16), buf_ref[slot, pl.ds(r0 + LANES, LANES), :].astype(BF16)))
        s = jnp.concatenate([_dot(qp, kt_i) for kt_i, _ in tiles], axis=1)
        me = _expand_mask(mask2b, k0 // SEL_LEN, nkt, tq)
        s = jnp.where(me > 0.5, s - slope * (dist0 - jnp.asarray(k0, F32)), NEG_INF)

        def pv(pb):
            out = None
            for i, (_, vt_i) in enumerate(tiles):
                d = _dot_nt(pb[:, PAGE_SIZE * i:PAGE_SIZE * (i + 1)], vt_i)
                out = d if out is None else out + d
            return out
        _flash_update(m_ref, l_ref, acc_ref, s, pv)
        return carry
    lax.fori_loop(0, n_pages // pages_per_tile, sel_body, 0)

    zpad = jnp.zeros((LANES - tq, LANES), F32)
    dist_new = (ii - lax.broadcasted_iota(jnp.int32, (r, LANES), 1)).astype(F32)
    new_blk = past // SEL_LEN
    me_new = jnp.concatenate([mask2[0:tq, new_blk:new_blk + 1]] * hpg + [mask2[tq:2 * tq, new_blk:new_blk + 1]] * hpg, axis=0)
    k_new = jnp.concatenate([rows_ref[:, 2 * LANES:3 * LANES], zpad], axis=0).astype(BF16)
    v_new = jnp.concatenate([rows_ref[:, 3 * LANES:4 * LANES], zpad], axis=0).astype(BF16)
    ok = (me_new > 0.5) & (dist_new >= 0.0)
    s = jnp.where(ok, _dot_nt(qp, k_new) - slope * dist_new, NEG_INF)
    _flash_update(m_ref, l_ref, acc_ref, s, lambda pb: _dot(pb, v_new))
    o_sel = acc_ref[...] / l_ref[...]

    _flash_init(m_ref, l_ref, acc_ref)
    dist_w = (ii + wb - lax.broadcasted_iota(jnp.int32, (r, wb), 1)).astype(F32)
    ok = (dist_w >= 0.0) & (dist_w < float(WINDOW))
    s = jnp.where(ok, _dot(qp, wst_ref[0:LANES, :].astype(BF16)) - slope * dist_w, NEG_INF)
    vwt = wst_ref[LANES:2 * LANES, :].astype(BF16)
    _flash_update(m_ref, l_ref, acc_ref, s, lambda pb: _dot_nt(pb, vwt))
    wnew = wnew_ref[...]
    kw_new = jnp.concatenate([wnew[:, 0:LANES], zpad], axis=0).astype(BF16)
    vw_new = jnp.concatenate([wnew[:, LANES:2 * LANES], zpad], axis=0).astype(BF16)
    s = jnp.where(dist_new >= 0.0, _dot_nt(qp, kw_new) - slope * dist_new, NEG_INF)
    _flash_update(m_ref, l_ref, acc_ref, s, lambda pb: _dot(pb, vw_new))
    o_win = acc_ref[...] / l_ref[...]

    gates = gates_ref[...]
    lane = lax.broadcasted_iota(jnp.int32, (tq, LANES), 1)
    outs = []
    for h in range(NSA_HEADS):
        rs = slice(h * tq, (h + 1) * tq)
        outs.append(gates[:, h:h + 1] * o_cmp[rs] + gates[:, NSA_HEADS + h:NSA_HEADS + h + 1] * o_sel[rs]
                    + gates[:, 2 * NSA_HEADS + h:2 * NSA_HEADS + h + 1] * o_win[rs])
    for c in range(4):
        o_ref[:, LANES * c:LANES * (c + 1)] = jnp.where(lane < HEAD_DIM, outs[c], outs[4 + c])


def _nsa_sample(page_table, q, p, rows, wnew, wstate_t, gates, kg, seg1, tmat, cache_t, *, tq, pages_per_tile, page_base, wstate_base):
    db, n_pages = page_table.shape
    past = n_pages * PAGE_SIZE
    nc = past // CMP_STRIDE
    wb = wstate_t.shape[1]
    r = NSA_HEADS * tq
    const = lambda shape: pl.BlockSpec(shape, lambda b, pt: (0,) * len(shape))
    kern = functools.partial(_nsa_sample_kernel, tq=tq, pages_per_tile=pages_per_tile, n_pages=n_pages, page_base=page_base)
    return pl.pallas_call(
        kern,
        grid_spec=pltpu.PrefetchScalarGridSpec(
            num_scalar_prefetch=1,
            grid=(db,),
            in_specs=[
                pl.BlockSpec((tq, 2 * NSA_WIDTH), lambda b, pt: (b, 0)),
                pl.BlockSpec((nc, 4 * LANES), lambda b, pt: (b, 0)),
                pl.BlockSpec((tq, KV_COLS), lambda b, pt: (b, 0)),
                pl.BlockSpec((tq, 2 * LANES), lambda b, pt: (b, 0)),
                pl.BlockSpec((2 * LANES, wb), lambda b, pt: (wstate_base + b, 0)),
                pl.BlockSpec((tq, LANES), lambda b, pt: (b, 0)),
                const(kg.shape), const(seg1.shape), const(tmat.shape),
                pl.BlockSpec(memory_space=pl.ANY),
            ],
            out_specs=pl.BlockSpec((tq, NSA_WIDTH), lambda b, pt: (b, 0)),
            scratch_shapes=[pltpu.VMEM((2, n_pages * 2 * LANES, LANES), F32), pltpu.SemaphoreType.DMA((2,)),
                            pltpu.VMEM((r, 1), F32), pltpu.VMEM((r, 1), F32), pltpu.VMEM((r, LANES), F32)],
        ),
        out_shape=jax.ShapeDtypeStruct((db * tq, NSA_WIDTH), F32),
        compiler_params=_params(("arbitrary",)),
        name="nsa_sample",
    )(page_table, q, p, rows, wnew, wstate_t, gates, kg, seg1, tmat, cache_t)


def _hgrn_consts(tc):
    nl = int(np.log2(tc))
    t = np.arange(tc)[:, None]
    u = np.arange(tc)[None, :]
    nl_mxu = min(3, nl)
    blocks = [(u <= t)]
    masks = [np.eye(tc, dtype=bool)]
    for lv in range(nl):
        hs = 1 << lv
        mid = (t // (2 * hs)) * 2 * hs + hs
        ref = mid - 1
        if lv < nl_mxu:
            blocks.append((t >= mid) & (u > ref) & (u <= t))
            blocks.append((t < mid) & (u > t) & (u <= ref))
        masks.append((t // (2 * hs) == u // (2 * hs)) & (t % (2 * hs) >= hs) & (u % (2 * hs) < hs))
    cm = np.concatenate(blocks, axis=0).astype(np.float32)
    mk = np.concatenate(masks, axis=0).astype(np.float32)
    return jnp.asarray(cm, BF16), jnp.asarray(mk, F32), nl, nl_mxu


def _hgrn_kernel(cm_ref, mk_ref, hq_ref, hlf_ref, hk_ref, hv_ref, hg_ref, og_ref, s0_ref, o_ref, sout_ref, st_ref, *, tc, nl, nl_mxu, nb):
    t = pl.program_id(1)
    nt = pl.num_programs(1)

    @pl.when(t == 0)
    def _():
        for i in range(nb):
            for h in range(HGRN_HEADS):
                st_ref[i, h] = s0_ref[i, h].T

    cm = cm_ref[...]
    row_id = lax.broadcasted_iota(jnp.int32, (tc, LANES), 0)
    for i in range(nb):
        for h in range(HGRN_HEADS):
            sl = slice(HGRN_DK * h, HGRN_DK * (h + 1))
            hi, lo = _split2(hlf_ref[i, :, sl])
            ee = _dot(cm, jnp.concatenate([hi, lo], axis=1))
            ee = ee[:, 0:LANES] + ee[:, LANES:2 * LANES]
            b = ee[0:tc]
            q = hq_ref[i, :, sl]
            k = hk_ref[i, :, sl]
            v = hv_ref[i, :, sl].astype(BF16)
            xb = jnp.exp(b)
            xs = jnp.exp(b[tc - 1:tc, :] - b)
            a = mk_ref[0:tc, :] * _dot_nt(q.astype(BF16), k.astype(BF16))
            for lv in range(nl):
                if lv < nl_mxu:
                    xu = jnp.exp(ee[(1 + 2 * lv) * tc:(2 + 2 * lv) * tc])
                    xl = jnp.exp(ee[(2 + 2 * lv) * tc:(3 + 2 * lv) * tc])
                else:
                    hs = 1 << lv
                    bref = jnp.concatenate([jnp.broadcast_to(b[j + hs - 1:j + hs, :], (2 * hs, LANES)) for j in range(0, tc, 2 * hs)], axis=0)
                    upper = ((row_id >> lv) & 1) == 1
                    diff = b - bref
                    xu = jnp.exp(jnp.where(upper, diff, 0.0))
                    xl = jnp.exp(jnp.where(upper, 0.0, -diff))
                a = a + mk_ref[(1 + lv) * tc:(2 + lv) * tc, :] * _dot_nt((q * xu).astype(BF16), (k * xl).astype(BF16))
            st = st_ref[i, h]
            o = _dot(a.astype(BF16), v) + _dot_nt((q * xb).astype(BF16), st.astype(BF16))
            st_new = st * xb[tc - 1:tc, :] + _dot_tn(v, (k * xs).astype(BF16))
            st_ref[i, h] = st_new
            on = o * lax.rsqrt(jnp.mean(o * o, axis=-1, keepdims=True) + EPS) * og_ref[...]
            o_ref[i, :, sl] = (on * hg_ref[i, :, sl]).astype(o_ref.dtype)

    @pl.when(t == nt - 1)
    def _():
        for i in range(nb):
            for h in range(HGRN_HEADS):
                sout_ref[i, h] = st_ref[i, h].T


def _hgrn(hq, hlf, hk, hv, hg, og, s0, *, batch, seq, tc, s0_base):
    cm, mk, nl, nl_mxu = _hgrn_consts(tc)
    nt = seq // tc
    nb = 2 if batch % 2 == 0 and s0_base % 2 == 0 else 1
    const = lambda shape: pl.BlockSpec(shape, lambda b, t: (0,) * len(shape))
    row = pl.BlockSpec((nb, tc, HGRN_WIDTH), lambda b, t: (b, t, 0))
    state = lambda base: pl.BlockSpec((nb, HGRN_HEADS, HGRN_DK, HGRN_DV), lambda b, t: (base + b, 0, 0, 0))
    kern = functools.partial(_hgrn_kernel, tc=tc, nl=nl, nl_mxu=nl_mxu, nb=nb)
    seq3 = lambda a: a.reshape(batch, seq, HGRN_WIDTH)
    o, s_out = pl.pallas_call(
        kern,
        grid=(batch // nb, nt),
        in_specs=[const(cm.shape), const(mk.shape), row, row, row, row, row, const(og.shape), state(s0_base // nb)],
        out_specs=[row, state(0)],
        out_shape=[jax.ShapeDtypeStruct((batch, seq, HGRN_WIDTH), BF16),
                   jax.ShapeDtypeStruct((batch, HGRN_HEADS, HGRN_DK, HGRN_DV), F32)],
        scratch_shapes=[pltpu.VMEM((nb, HGRN_HEADS, HGRN_DV, HGRN_DK), F32)],
        compiler_params=_params(("arbitrary", "arbitrary")),
        name="hgrn",
    )(cm, mk, seq3(hq), seq3(hlf), seq3(hk), seq3(hv), seq3(hg), og, s0)
    return o.reshape(batch * seq, HGRN_WIDTH), s_out


def _outproj_kernel(*refs, moe):
    if moe:
        x_ref, on_ref, oh_ref, w_ref, g_ref, r_ref, xo_ref, h_ref, gate_ref = refs
    else:
        x_ref, on_ref, oh_ref, w_ref, g_ref, xo_ref, h_ref = refs
    xn = x_ref[...] + _dot(on_ref[...], w_ref[0:NSA_WIDTH, :]) + _dot(oh_ref[...], w_ref[NSA_WIDTH:NSA_WIDTH + HGRN_WIDTH, :])
    xo_ref[...] = xn
    h = xn * lax.rsqrt(jnp.mean(xn * xn, axis=-1, keepdims=True) + EPS) * g_ref[...]
    h_ref[...] = h.astype(BF16)
    if moe:
        logits = None
        rparts = _split3(r_ref[...])
        hparts = _split3(h)
        for i in range(3):
            for j in range(3 - i):
                d = _dot(hparts[i], rparts[j])
                logits = d if logits is None else logits + d
        lane = lax.broadcasted_iota(jnp.int32, logits.shape, 1).astype(F32)
        lg = jnp.where(lane < N_EXPERTS, logits, NEG_INF)
        m1 = jnp.max(lg, axis=-1, keepdims=True)
        i1 = jnp.min(jnp.where(lg == m1, lane, float(LANES)), axis=-1, keepdims=True)
        lg2 = jnp.where(lane == i1, NEG_INF, lg)
        m2 = jnp.max(lg2, axis=-1, keepdims=True)
        i2 = jnp.min(jnp.where(lg2 == m2, lane, float(LANES)), axis=-1, keepdims=True)
        e2 = jnp.exp(m2 - m1)
        den = 1.0 + e2
        gate_ref[...] = (jnp.where(lane == 0.0, i1, 0.0) + jnp.where(lane == 1.0, i2, 0.0)
                         + jnp.where(lane == 2.0, 1.0 / den, 0.0) + jnp.where(lane == 3.0, e2 / den, 0.0))


def _outproj(x, o_nsa, o_hg, w, g, router, tm):
    t, d = x.shape
    moe = router is not None
    const = lambda shape: pl.BlockSpec(shape, lambda i: (0,) * len(shape))
    row = lambda n: pl.BlockSpec((tm, n), lambda i: (i, 0))
    in_specs = [row(d), row(NSA_WIDTH), row(HGRN_WIDTH), const(w.shape), const((1, d))]
    args = [x, o_nsa, o_hg, w, g]
    out_specs = [row(d), row(d)]
    out_shape = [jax.ShapeDtypeStruct((t, d), F32), jax.ShapeDtypeStruct((t, d), BF16)]
    if moe:
        in_specs.append(const(router.shape))
        args.append(router)
        out_specs.append(row(LANES))
        out_shape.append(jax.ShapeDtypeStruct((t, LANES), F32))
    return pl.pallas_call(
        functools.partial(_outproj_kernel, moe=moe),
        grid=(t // tm,),
        in_specs=in_specs, out_specs=out_specs, out_shape=out_shape,
        compiler_params=_params(("arbitrary",)),
        name="outproj_moe" if moe else "outproj",
    )(*args)


def _ffn_kernel(x_ref, h_ref, wg_ref, wu_ref, wd_ref, o_ref):
    f = pl.program_id(1)
    h = h_ref[...]
    a = _dot(h, wg_ref[...])
    b = _dot(h, wu_ref[...])
    d = _dot((a * _sigmoid(a) * b).astype(BF16), wd_ref[...])

    @pl.when(f == 0)
    def _():
        o_ref[...] = x_ref[...] + d

    @pl.when(f > 0)
    def _():
        o_ref[...] = o_ref[...] + d


def _ffn(x, h, wg, wu, wd, tm, tf):
    t, d = x.shape
    ff = wg.shape[1]
    return pl.pallas_call(
        _ffn_kernel,
        grid=(t // tm, ff // tf),
        in_specs=[pl.BlockSpec((tm, d), lambda i, f: (i, 0)), pl.BlockSpec((tm, d), lambda i, f: (i, 0)),
                  pl.BlockSpec((d, tf), lambda i, f: (0, f)), pl.BlockSpec((d, tf), lambda i, f: (0, f)),
                  pl.BlockSpec((tf, d), lambda i, f: (f, 0))],
        out_specs=pl.BlockSpec((tm, d), lambda i, f: (i, 0)),
        out_shape=jax.ShapeDtypeStruct((t, d), F32),
        compiler_params=_params(("arbitrary", "arbitrary")),
        name="ffn",
    )(x, h, wg, wu, wd)


def _route_kernel(rt_ref, pos_ref, post_ref, cnt_ref, carry_ref, *, ts):
    s = pl.program_id(0)

    @pl.when(s == 0)
    def _():
        carry_ref[...] = jnp.zeros(carry_ref.shape, F32)

    rt = rt_ref[...]
    lane = lax.broadcasted_iota(jnp.int32, (ts, LANES), 1).astype(F32)
    c = jnp.where((lane == rt[:, 0:1]) | (lane == rt[:, 1:2]), 1.0, 0.0)
    lt = (lax.broadcasted_iota(jnp.int32, (ts, ts), 1) < lax.broadcasted_iota(jnp.int32, (ts, ts), 0)).astype(BF16)
    rank = _dot(lt, c.astype(BF16)) + carry_ref[...]
    pos = jnp.where(c > 0.0, rank, -1.0)
    pos_ref[...] = pos
    post_ref[...] = pos.T[0:8, :]
    n = jnp.sum(c, axis=0, keepdims=True)
    cnt_ref[...] = jnp.broadcast_to(n, cnt_ref.shape)
    carry_ref[...] = carry_ref[...] + n


def _route(rt, ts):
    t = rt.shape[0]
    ns = t // ts
    return pl.pallas_call(
        functools.partial(_route_kernel, ts=ts),
        grid=(ns,),
        in_specs=[pl.BlockSpec((ts, LANES), lambda s: (s, 0))],
        out_specs=[pl.BlockSpec((ts, LANES), lambda s: (s, 0)), pl.BlockSpec((8, ts), lambda s: (0, s)),
                   pl.BlockSpec((None, 8, LANES), lambda s: (s, 0, 0))],
        out_shape=[jax.ShapeDtypeStruct((t, LANES), F32), jax.ShapeDtypeStruct((8, t), F32),
                   jax.ShapeDtypeStruct((ns, 8, LANES), F32)],
        scratch_shapes=[pltpu.VMEM((1, LANES), F32)],
        compiler_params=_params(("arbitrary",)),
        name="moe_route",
    )(rt)


def _moe_gather_kernel(texp_ref, trank_ref, slo_ref, nsrc_ref, post_ref, h_ref, o_ref, hbuf_ref, sem_ref, acc_ref, *, tm, ts):
    j = pl.program_id(0)
    e = texp_ref[j]
    r0 = trank_ref[j]
    s0 = slo_ref[j]
    n = nsrc_ref[j]
    acc_ref[...] = jnp.zeros(acc_ref.shape, F32)
    row = lax.broadcasted_iota(jnp.int32, (tm, 1), 0).astype(F32) + r0.astype(F32)

    def copy(s, slot):
        return pltpu.make_async_copy(h_ref.at[pl.ds(pl.multiple_of(s * ts, ts), ts), :], hbuf_ref.at[slot], sem_ref.at[slot])

    @pl.when(n > 0)
    def _():
        copy(s0, 0).start()

    def body(k, carry):
        slot = k % 2

        @pl.when(k + 1 < n)
        def _():
            copy(s0 + k + 1, 1 - slot).start()
        copy(s0 + k, slot).wait()
        prow = post_ref[pl.ds(e, 1), pl.ds(pl.multiple_of((s0 + k) * ts, ts), ts)]
        onehot = jnp.where(prow == row, 1.0, 0.0).astype(BF16)
        acc_ref[...] = acc_ref[...] + _dot(onehot, hbuf_ref[slot])
        return carry
    lax.fori_loop(0, n, body, 0)
    o_ref[...] = acc_ref[...].astype(o_ref.dtype)


def _moe_gather(tables, post, h, *, n_tiles, tm, ts):
    t, d = h.shape
    return pl.pallas_call(
        functools.partial(_moe_gather_kernel, tm=tm, ts=ts),
        grid_spec=pltpu.PrefetchScalarGridSpec(
            num_scalar_prefetch=4,
            grid=(n_tiles,),
            in_specs=[pl.BlockSpec(post.shape, lambda j, *_: (0, 0)), pl.BlockSpec(memory_space=pl.ANY)],
            out_specs=pl.BlockSpec((tm, d), lambda j, *_: (j, 0)),
            scratch_shapes=[pltpu.VMEM((2, ts, d), BF16), pltpu.SemaphoreType.DMA((2,)), pltpu.VMEM((tm, d), F32)],
        ),
        out_shape=jax.ShapeDtypeStruct((n_tiles * tm, d), BF16),
        compiler_params=_params(("arbitrary",)),
        name="moe_gather",
    )(*tables, post, h)


def _moe_ffn_kernel(texp_ref, tused_ref, h_ref, wg_ref, wu_ref, wd_ref, o_ref, acc_ref):
    f = pl.program_id(1)
    last = pl.num_programs(1) - 1
    used = tused_ref[pl.program_id(0)] > 0

    @pl.when(used)
    def _():
        h = h_ref[...]
        a = _dot(h, wg_ref[...])
        b = _dot(h, wu_ref[...])
        d = _dot((a * _sigmoid(a) * b).astype(BF16), wd_ref[...])

        @pl.when(f == 0)
        def _():
            acc_ref[...] = d

        @pl.when(f > 0)
        def _():
            acc_ref[...] = acc_ref[...] + d

        @pl.when(f == last)
        def _():
            o_ref[...] = acc_ref[...].astype(o_ref.dtype)

    @pl.when(jnp.logical_not(used) & (f == last))
    def _():
        o_ref[...] = jnp.zeros(o_ref.shape, o_ref.dtype)


def _moe_ffn(texp, tused, hs, wg, wu, wd, *, tm, tf):
    rows, d = hs.shape
    ff = wg.shape[2]
    return pl.pallas_call(
        _moe_ffn_kernel,
        grid_spec=pltpu.PrefetchScalarGridSpec(
            num_scalar_prefetch=2,
            grid=(rows // tm, ff // tf),
            in_specs=[pl.BlockSpec((tm, d), lambda j, f, te, tu: (j, 0)),
                      pl.BlockSpec((None, d, tf), lambda j, f, te, tu: (te[j], 0, f)),
                      pl.BlockSpec((None, d, tf), lambda j, f, te, tu: (te[j], 0, f)),
                      pl.BlockSpec((None, tf, d), lambda j, f, te, tu: (te[j], f, 0))],
            out_specs=pl.BlockSpec((tm, d), lambda j, f, te, tu: (j, 0)),
            scratch_shapes=[pltpu.VMEM((tm, d), F32)],
        ),
        out_shape=jax.ShapeDtypeStruct((rows, d), BF16),
        compiler_params=_params(("arbitrary", "arbitrary")),
        name="moe_ffn",
    )(texp, tused, hs, wg, wu, wd)


def _moe_combine_kernel(start_ref, delta_ref, x_ref, rt_ref, pos_ref, ys_ref, o_ref, ybuf_ref, sem_ref, *, ts, win):
    s = pl.program_id(0)
    ns = pl.num_programs(0)
    slot = s % 2

    def copy(step, e, sl):
        st = pl.multiple_of(start_ref[step * N_EXPERTS + e], LANES)
        return pltpu.make_async_copy(ys_ref.at[pl.ds(st, win), :], ybuf_ref.at[sl, e], sem_ref.at[sl])

    def start_all(step, sl):
        for e in range(N_EXPERTS):
            copy(step, e, sl).start()

    @pl.when(s == 0)
    def _():
        start_all(s, slot)

    @pl.when(s + 1 < ns)
    def _():
        start_all(s + 1, 1 - slot)

    rt = rt_ref[...]
    pos = pos_ref[...]
    col = lax.broadcasted_iota(jnp.int32, (1, win), 1).astype(F32)
    for e in range(N_EXPERTS):
        copy(s, e, slot).wait()
    y = x_ref[...]
    for e in range(N_EXPERTS):
        rel = pos[:, e:e + 1] + delta_ref[s * N_EXPERTS + e].astype(F32)
        g = jnp.where(rel == col, 1.0, 0.0).astype(BF16)
        w = jnp.where(rt[:, 0:1] == float(e), rt[:, 2:3], 0.0) + jnp.where(rt[:, 1:2] == float(e), rt[:, 3:4], 0.0)
        y = y + w * _dot(g, ybuf_ref[slot, e])
    o_ref[...] = y


def _moe_combine(start, delta, x, rt, pos, ys, *, ts, win):
    t, d = x.shape
    row = lambda n: pl.BlockSpec((ts, n), lambda s, *_: (s, 0))
    return pl.pallas_call(
        functools.partial(_moe_combine_kernel, ts=ts, win=win),
        grid_spec=pltpu.PrefetchScalarGridSpec(
            num_scalar_prefetch=2,
            grid=(t // ts,),
            in_specs=[row(d), row(LANES), row(LANES), pl.BlockSpec(memory_space=pl.ANY)],
            out_specs=row(d),
            scratch_shapes=[pltpu.VMEM((2, N_EXPERTS, win, d), BF16), pltpu.SemaphoreType.DMA((2,))],
        ),
        out_shape=jax.ShapeDtypeStruct((t, d), F32),
        compiler_params=_params(("arbitrary",)),
        name="moe_combine",
    )(start, delta, x, rt, pos, ys)


def _moe(x, h, rt, wg, wu, wd, *, tm, ts, tf):
    t, d = x.shape
    win = ts + LANES
    n_tiles = -(-2 * t // tm) + N_EXPERTS + 1
    pos, post, cnt = _route(rt, ts)
    counts = cnt[:, 0, :N_EXPERTS].astype(jnp.int32)
    cum = jnp.concatenate([jnp.zeros((1, N_EXPERTS), jnp.int32), jnp.cumsum(counts, axis=0)], axis=0)
    tiles_e = -(-cum[-1] // tm)
    tstart = jnp.concatenate([jnp.zeros((1,), jnp.int32), jnp.cumsum(tiles_e)])
    jt = jnp.arange(n_tiles, dtype=jnp.int32)
    texp = jnp.minimum(jnp.sum(jt[:, None] >= tstart[None, 1:], axis=1), N_EXPERTS - 1).astype(jnp.int32)
    used = jt < tstart[-1]
    trank = jnp.where(used, (jt - tstart[texp]) * tm, -2 * tm).astype(jnp.int32)
    lo = cum[:-1].T[texp]
    hi = cum[1:].T[texp]
    hit = used[:, None] & (lo < trank[:, None] + tm) & (hi > trank[:, None])
    slo = jnp.sum(used[:, None] & (hi <= trank[:, None]), axis=1).astype(jnp.int32)
    nsrc = jnp.sum(hit, axis=1).astype(jnp.int32)
    row0 = tstart[:-1][None, :] * tm + cum[:-1]
    start = (row0 // LANES) * LANES
    delta = tstart[:-1][None, :] * tm - start
    hs = _moe_gather((texp, trank, slo, nsrc), post, h, n_tiles=n_tiles, tm=tm, ts=ts)
    ys = _moe_ffn(texp, used.astype(jnp.int32), hs, wg, wu, wd, tm=tm, tf=tf)
    return _moe_combine(start.reshape(-1).astype(jnp.int32), delta.reshape(-1).astype(jnp.int32), x, rt, pos, ys, ts=ts, win=win)


def _head_perm():
    idx = []
    for c in range(4):
        idx += list(range(HEAD_DIM * c, HEAD_DIM * (c + 1))) + list(range(HEAD_DIM * (4 + c), HEAD_DIM * (5 + c)))
    return np.asarray(idx, np.int32)


def _tap_matrix(n_cmp, nc_pad, n_blk, nb_pad):
    r_s = SEL_LEN // CMP_STRIDE
    r_c = CMP_LEN // CMP_STRIDE
    taps = np.convolve(np.ones(r_s), np.ones(r_c)) / r_c
    tm = np.zeros((nc_pad, nb_pad), np.float32)
    for j in range(n_blk):
        for kk, w in enumerate(taps):
            n = j * r_s + kk - (r_c - 1)
            if 0 <= n < n_cmp:
                tm[n, j] = w
    return tm


def _layer_weights(l, w_in, q_gain, k_gain, cmp_pe, cmp_w, w_out):
    perm = _head_perm()
    wl = w_in[l]
    q_end = NSA_WIDTH
    kv_end = q_end + 6 * NSA_GROUPS * HEAD_DIM
    gate_end = kv_end + 3 * NSA_HEADS
    d = wl.shape[0]
    w_pad = jnp.concatenate([wl[:, :q_end][:, perm], wl[:, q_end:kv_end], wl[:, kv_end:gate_end],
                             jnp.zeros((d, LANES - 3 * NSA_HEADS), wl.dtype), wl[:, gate_end:]], axis=1).astype(BF16)
    qg = jnp.tile(q_gain[l], NSA_HEADS)[None, :]
    kg_proj = jnp.stack([jnp.tile(k_gain[l, 1], NSA_GROUPS), jnp.tile(k_gain[l, 2], NSA_GROUPS)])
    kg_cmp = jnp.tile(k_gain[l, 0], NSA_GROUPS)[None, :]
    pairs = CMP_STRIDE // 2
    cw = cmp_w[l].reshape(2, 2, pairs, 2, HEAD_DIM, HEAD_DIM)
    eye = jnp.eye(NSA_GROUPS, dtype=cw.dtype)
    w_cmp = jnp.einsum('krjsde,gh->kjsgdrhe', cw, eye).reshape(2 * pairs * 2 * LANES, 2 * LANES).astype(BF16)
    pe = cmp_pe[l].reshape(2, 2, pairs, 2, 1, HEAD_DIM)
    pe = jnp.broadcast_to(pe, (2, 2, pairs, 2, NSA_GROUPS, HEAD_DIM)).reshape(4 * pairs, 2 * LANES)
    wo = w_out[l]
    wo = jnp.concatenate([wo[:NSA_WIDTH][perm], wo[NSA_WIDTH:]], axis=0).astype(BF16)
    return w_pad, qg, kg_proj, kg_cmp, w_cmp, pe, wo


def _token_minor(a):
    n = a.ndim
    return jnp.transpose(a, tuple(range(n - 4)) + (n - 3, n - 2, n - 1, n - 4))


def _token_major(a):
    n = a.ndim
    return jnp.transpose(a, tuple(range(n - 4)) + (n - 1, n - 4, n - 3, n - 2))


def kernel(x_prompt, x_sample, cache_kv, state_win_kv, state_hgrn, page_table, norm_mix, norm_ffn, w_in, q_gain, k_gain, cmp_pe, cmp_w, hgrn_lb_logits, hgrn_o_gain, w_out, ffn_w_gate, ffn_w_up, ffn_w_down, moe_router, moe_w_gate, moe_w_up, moe_w_down):
    depth = w_in.shape[0]
    batch, seq, d_model = x_prompt.shape
    db, ds, _ = x_sample.shape
    n_pool = cache_kv.shape[1]
    n_pages = page_table.shape[1]
    past = n_pages * PAGE_SIZE
    wb = state_win_kv.shape[2]
    assert wb == WINDOW and seq % 256 == 0 and seq >= WINDOW and ds == 8 and n_pages % 8 == 0

    tq, nkt_p, tc = 128, 512, 128
    tm_p = 256
    tm_f = 512 if (batch * seq) % 512 == 0 else 256
    tm_e = 512
    ts = db * ds
    d_ff = ffn_w_gate.shape[-1]
    tf = d_ff // 2

    sm = jax.nn.softmax(hgrn_lb_logits.astype(F32), axis=0)
    lower = jnp.concatenate([jnp.zeros_like(sm[:1]), jnp.cumsum(sm[1:], axis=0)], axis=0)
    seg = jnp.asarray(np.kron(np.eye(NSA_HEADS), np.ones((HEAD_DIM, HEAD_DIM))), BF16)
    seg1 = seg[:LANES, :LANES]

    nc_p = seq // CMP_STRIDE
    nb_p = -(-seq // SEL_LEN)
    tt_p = jnp.asarray(_tap_matrix(nc_p - 1, nc_p, nb_p, -(-nb_p // 8) * 8).T, BF16)
    nc_s = past // CMP_STRIDE
    nb_s = -(-(past + ds) // SEL_LEN)
    nbp_s = -(-nb_s // LANES) * LANES
    tm_s = jnp.asarray(_tap_matrix(nc_s - 1, nc_s, nb_s, nbp_s), BF16)

    cache_t = _token_minor(cache_kv).reshape(depth * n_pool * KV_COLS, PAGE_SIZE)
    wstate_t = _token_minor(state_win_kv).reshape(depth * db * 2 * LANES, wb)
    hstate = state_hgrn.reshape(depth * db, HGRN_HEADS, HGRN_DK, HGRN_DV)
    zero_state = jnp.zeros((batch, HGRN_HEADS, HGRN_DK, HGRN_DV), F32)

    xp = x_prompt.reshape(batch * seq, d_model)
    xs = x_sample.reshape(ts, d_model)
    kv_p, kv_s, win_p, win_s, hs_p, hs_s = [], [], [], [], [], []
    for l in range(depth):
        w_pad, qg, kg_proj, kg_cmp, w_cmp, pe, wo = _layer_weights(l, w_in, q_gain, k_gain, cmp_pe, cmp_w, w_out)
        lb = lower[l]
        hp = jnp.stack([jnp.log(lb), jnp.log1p(-lb), 1.0 - lb])
        og = hgrn_o_gain[l][None, :]
        g1 = norm_mix[l][None, :]
        g2 = norm_ffn[l][None, :]
        i = l // 2
        if l % 2 == 0:
            router = None
            wg, wu, wd = ffn_w_gate[i].astype(BF16), ffn_w_up[i].astype(BF16), ffn_w_down[i].astype(BF16)
        else:
            router = jnp.pad(moe_router[i], ((0, 0), (0, LANES - N_EXPERTS)))
            wg, wu, wd = moe_w_gate[i].astype(BF16), moe_w_up[i].astype(BF16), moe_w_down[i].astype(BF16)

        def mixer(x, h, rt):
            tm = tm_f if x.shape[0] % tm_f == 0 else x.shape[0]
            if router is None:
                return _ffn(x, h, wg, wu, wd, tm, tf)
            return _moe(x, h, rt, wg, wu, wd, tm=tm_e, ts=min(tm_p, x.shape[0]), tf=tf)

        qt, rows_t, win_t, kb, vt, kc, vc, gt, hq, hlf, hk, hv, hg = _proj_prompt(xp, g1, w_pad, qg, kg_proj, seg, hp, batch=batch, seq=seq, tm=tm_p)
        pp = _compress_prompt(kc, vc, pe, w_cmp, batch=batch, seq=seq)
        o_nsa = _nsa_prompt(qt, kb, vt, pp, gt, kg_cmp, seg1, tt_p, batch=batch, seq=seq, tq=tq, nkt=nkt_p)
        o_hg, s_fin = _hgrn(hq, hlf, hk, hv, hg, og, zero_state, batch=batch, seq=seq, tc=tc, s0_base=0)
        res = _outproj(xp, o_nsa, o_hg, wo, g2, router, tm_p)
        xp = mixer(res[0], res[1], res[2] if router is not None else None)
        kv_p.append(rows_t.reshape(batch, 4, NSA_GROUPS, HEAD_DIM, seq))
        win_p.append(win_t.reshape(batch, 2, NSA_GROUPS, HEAD_DIM, seq)[..., seq - min(WINDOW, seq):])
        hs_p.append(s_fin)

        q, rows, win, gates, hq, hlf, hk, hv, hg = _proj_sample(xs, g1, w_pad, qg, kg_proj, seg, hp)
        ps = _compress_sample(page_table, cache_t, pe, w_cmp, page_base=l * n_pool)
        o_nsa = _nsa_sample(page_table, q, ps, rows, win, wstate_t, gates, kg_cmp, seg1, tm_s, cache_t,
                            tq=ds, pages_per_tile=8, page_base=l * n_pool, wstate_base=l * db)
        padt = lambda a: jnp.pad(a.reshape(db, ds, HGRN_WIDTH), ((0, 0), (0, tc - ds), (0, 0))).reshape(db * tc, HGRN_WIDTH)
        o_hg, s_new = _hgrn(padt(hq), padt(hlf), padt(hk), padt(hv), padt(hg), og, hstate, batch=db, seq=tc, tc=tc, s0_base=l * db)
        o_hg = o_hg.reshape(db, tc, HGRN_WIDTH)[:, :ds].reshape(ts, HGRN_WIDTH)
        res = _outproj(xs, o_nsa.astype(BF16), o_hg, wo, g2, router, ts)
        xs = mixer(res[0], res[1], res[2] if router is not None else None)
        kv_s.append(rows.reshape(db, ds, 4, NSA_GROUPS, HEAD_DIM))
        win_s.append(win.reshape(db, ds, 2, NSA_GROUPS, HEAD_DIM))
        hs_s.append(s_new)

    new_win_sample = jnp.concatenate([state_win_kv[:, :, ds:], jnp.stack(win_s)], axis=2)
    return (xp.reshape(batch, seq, d_model), xs.reshape(db, ds, d_model), _token_major(jnp.stack(kv_p)), jnp.stack(kv_s),
            _token_major(jnp.stack(win_p)), new_win_sample, jnp.stack(hs_p), jnp.stack(hs_s))
```

```python
import functools

import numpy as np
import jax
import jax.numpy as jnp
from jax import lax
from jax.experimental import pallas as pl
from jax.experimental.pallas import tpu as pltpu

F32 = jnp.float32
BF16 = jnp.bfloat16

NSA_HEADS = 8
NSA_GROUPS = 2
HEAD_DIM = 64
NSA_WIDTH = NSA_HEADS * HEAD_DIM
CMP_LEN = 32
CMP_STRIDE = 16
SEL_LEN = 64
N_SEL = 16
N_LOCAL = 2
WINDOW = 512
FORCE_BONUS = 1e4
HGRN_HEADS = 4
HGRN_DK = 128
HGRN_DV = 128
HGRN_WIDTH = HGRN_HEADS * HGRN_DV
N_EXPERTS = 8
EPS = 1e-6
NEG_INF = -1e30
PAGE_SIZE = 128

LANES = 128
KV_COLS = 4 * NSA_GROUPS * HEAD_DIM
CMP_PITCH = CMP_STRIDE + 1
TR_PAGES = 4
VMEM_LIMIT = 48 * 1024 * 1024


def _log2(n):
    assert n & (n - 1) == 0
    return n.bit_length() - 1


def _dot(a, b):
    return jnp.dot(a, b, preferred_element_type=F32)


def _dot_nt(a, b):
    return lax.dot_general(a, b, (((1,), (1,)), ((), ())), preferred_element_type=F32)


def _dot_tn(a, b):
    return lax.dot_general(a, b, (((0,), (0,)), ((), ())), preferred_element_type=F32)


def _split2(x):
    hi = x.astype(BF16)
    lo = (x - hi.astype(F32)).astype(BF16)
    return hi, lo


def _split3(x):
    hi = x.astype(BF16)
    r = x - hi.astype(F32)
    mid = r.astype(BF16)
    lo = (r - mid.astype(F32)).astype(BF16)
    return hi, mid, lo


def _segsum(x, seg):
    hi, lo = _split2(x)
    return _dot(hi, seg) + _dot(lo, seg)


def _seg_rms(z, gain, seg):
    ss = _segsum(z * z, seg) * (1.0 / HEAD_DIM)
    return z * lax.rsqrt(ss + EPS) * gain


def _sigmoid(x):
    return 1.0 / (1.0 + jnp.exp(-x))


def _params(sem, vmem=VMEM_LIMIT):
    return pltpu.CompilerParams(dimension_semantics=sem, vmem_limit_bytes=vmem)


def _proj_core(x_ref, gain_ref, w_ref, qg_ref, kg_ref, seg_ref, hp_ref, hq_ref, hlf_ref, hk_ref, hv_ref, hg_ref):
    x = x_ref[...]
    ms = jnp.mean(x * x, axis=-1, keepdims=True)
    h = (x * lax.rsqrt(ms + EPS) * gain_ref[...]).astype(BF16)

    def mm(c0, c1):
        return _dot(h, w_ref[:, c0:c1])

    seg1 = seg_ref[0:LANES, 0:LANES]
    qn = _seg_rms(mm(0, NSA_WIDTH), qg_ref[...], seg_ref[...]) * (HEAD_DIM ** -0.5)
    c0 = NSA_WIDTH
    kv = mm(c0, c0 + 6 * LANES)
    kvs = [kv[:, LANES * j:LANES * (j + 1)] for j in range(6)]
    kvs[2] = _seg_rms(kvs[2], kg_ref[0:1, :], seg1)
    kvs[4] = _seg_rms(kvs[4], kg_ref[1:2, :], seg1)
    c0 += 6 * LANES
    gates = _sigmoid(mm(c0, c0 + LANES))
    c0 += LANES
    zq = mm(c0, c0 + HGRN_WIDTH)
    hq_ref[...] = zq * _sigmoid(zq)
    c0 += HGRN_WIDTH
    zf = mm(c0, c0 + HGRN_WIDTH)
    log_sig = jnp.minimum(zf, 0.0) - jnp.log1p(jnp.exp(-jnp.abs(zf)))
    a = hp_ref[0:1, :]
    c = hp_ref[1:2, :] + log_sig
    hlf_ref[...] = jnp.maximum(a, c) + jnp.log1p(jnp.exp(-jnp.abs(a - c)))
    hk_ref[...] = hp_ref[2:3, :] * _sigmoid(-zf)
    c0 += HGRN_WIDTH
    hv_ref[...] = mm(c0, c0 + HGRN_WIDTH)
    c0 += HGRN_WIDTH
    zg = mm(c0, c0 + HGRN_WIDTH)
    hg_ref[...] = zg * _sigmoid(zg)
    return qn, kvs, gates


def _proj_prompt_kernel(x_ref, gain_ref, w_ref, qg_ref, kg_ref, seg_ref, hp_ref,
                        qt_ref, rowst_ref, wint_ref, kb_ref, vt_ref, kc_ref, vc_ref, gt_ref,
                        hq_ref, hlf_ref, hk_ref, hv_ref, hg_ref):
    qn, kvs, gates = _proj_core(x_ref, gain_ref, w_ref, qg_ref, kg_ref, seg_ref, hp_ref, hq_ref, hlf_ref, hk_ref, hv_ref, hg_ref)
    for c in range(4):
        qt_ref[LANES * c:LANES * (c + 1), :] = qn[:, LANES * c:LANES * (c + 1)].T.astype(BF16)
    kvt = [a.T for a in kvs]
    for j in range(4):
        rowst_ref[LANES * j:LANES * (j + 1), :] = kvt[j]
    wint_ref[0:LANES, :] = kvt[4]
    wint_ref[LANES:2 * LANES, :] = kvt[5]
    kb_ref[:, 0:LANES] = kvs[2].astype(BF16)
    kb_ref[:, LANES:2 * LANES] = kvs[4].astype(BF16)
    vt_ref[0:LANES, :] = kvt[3].astype(BF16)
    vt_ref[LANES:2 * LANES, :] = kvt[5].astype(BF16)
    kc_ref[...] = kvs[0]
    vc_ref[...] = kvs[1]
    gt_ref[...] = gates.T


def _proj_sample_kernel(x_ref, gain_ref, w_ref, qg_ref, kg_ref, seg_ref, hp_ref,
                        q_ref, rows_ref, win_ref, gates_ref, hq_ref, hlf_ref, hk_ref, hv_ref, hg_ref):
    qn, kvs, gates = _proj_core(x_ref, gain_ref, w_ref, qg_ref, kg_ref, seg_ref, hp_ref, hq_ref, hlf_ref, hk_ref, hv_ref, hg_ref)
    lane = lax.broadcasted_iota(jnp.int32, (qn.shape[0], LANES), 1)
    for c in range(4):
        blk = qn[:, LANES * c:LANES * (c + 1)]
        q_ref[:, LANES * c:LANES * (c + 1)] = jnp.where(lane < HEAD_DIM, blk, 0.0)
        q_ref[:, LANES * (4 + c):LANES * (5 + c)] = jnp.where(lane >= HEAD_DIM, blk, 0.0)
    for j in range(4):
        rows_ref[:, LANES * j:LANES * (j + 1)] = kvs[j]
    win_ref[:, 0:LANES] = kvs[4]
    win_ref[:, LANES:2 * LANES] = kvs[5]
    gates_ref[...] = gates


def _proj_prompt(x, gain, w, qg, kg, seg, hp, *, batch, seq, tm):
    t, d = x.shape
    nt = seq // tm
    const = lambda shape: pl.BlockSpec(shape, lambda b, i: (0,) * len(shape))
    row = lambda n: pl.BlockSpec((tm, n), lambda b, i: (b * nt + i, 0))
    colt = lambda n: pl.BlockSpec((None, n, tm), lambda b, i: (b, 0, i))
    tshape = lambda n, dt: jax.ShapeDtypeStruct((batch, n, seq), dt)
    rshape = lambda n, dt: jax.ShapeDtypeStruct((t, n), dt)
    return pl.pallas_call(
        _proj_prompt_kernel,
        grid=(batch, nt),
        in_specs=[row(d), const((1, d)), const(w.shape), const(qg.shape), const(kg.shape), const(seg.shape), const(hp.shape)],
        out_specs=[colt(NSA_WIDTH), colt(KV_COLS), colt(2 * LANES), row(2 * LANES), colt(2 * LANES), row(LANES), row(LANES),
                   colt(LANES)] + [row(HGRN_WIDTH)] * 5,
        out_shape=[tshape(NSA_WIDTH, BF16), tshape(KV_COLS, F32), tshape(2 * LANES, F32), rshape(2 * LANES, BF16),
                   tshape(2 * LANES, BF16), rshape(LANES, F32), rshape(LANES, F32), tshape(LANES, F32)]
                  + [rshape(HGRN_WIDTH, F32)] * 5,
        compiler_params=_params(("arbitrary", "arbitrary")),
        name="proj_prompt",
    )(x, gain, w, qg, kg, seg, hp)


def _proj_sample(x, gain, w, qg, kg, seg, hp):
    t, d = x.shape
    const = lambda shape: pl.BlockSpec(shape, lambda i: (0,) * len(shape))
    row = lambda n: pl.BlockSpec((t, n), lambda i: (0, 0))
    outs = [2 * NSA_WIDTH, KV_COLS, 2 * LANES, LANES] + [HGRN_WIDTH] * 5
    return pl.pallas_call(
        _proj_sample_kernel,
        grid=(1,),
        in_specs=[row(d), const((1, d)), const(w.shape), const(qg.shape), const(kg.shape), const(seg.shape), const(hp.shape)],
        out_specs=[row(n) for n in outs],
        out_shape=[jax.ShapeDtypeStruct((t, n), F32) for n in outs],
        compiler_params=_params(("arbitrary",)),
        name="proj_sample",
    )(x, gain, w, qg, kg, seg, hp)


def _compress_rows(k_ref, v_ref, pe_ref, w_ref, out_ref, m, pitch=CMP_STRIDE):
    pairs = CMP_STRIDE // 2
    for kind, ref in ((0, k_ref), (1, v_ref)):
        acc = None
        bias = None
        for j in range(pairs):
            w = w_ref[(kind * pairs + j) * 2 * LANES:(kind * pairs + j + 1) * 2 * LANES, :]
            xa = ref[pl.ds(2 * j, m, stride=pitch), :]
            xb = ref[pl.ds(2 * j + 1, m, stride=pitch), :]
            d = _dot(jnp.concatenate([xa, xb], axis=1).astype(BF16), w)
            acc = d if acc is None else acc + d
            pe0 = pe_ref[(kind * 2) * pairs + j:(kind * 2) * pairs + j + 1, :]
            pe1 = pe_ref[(kind * 2 + 1) * pairs + j:(kind * 2 + 1) * pairs + j + 1, :]
            pel = jnp.concatenate([jnp.broadcast_to(pe0, (8, 2 * LANES)), jnp.broadcast_to(pe1, (8, 2 * LANES))], axis=0)
            pb = _dot(pel.astype(BF16), w)
            bias = pb if bias is None else bias + pb
        out_ref[:, kind * 2 * LANES:kind * 2 * LANES + LANES] = acc[:, 0:LANES] + bias[0:1, 0:LANES]
        out_ref[:, kind * 2 * LANES + LANES:(kind + 1) * 2 * LANES] = acc[:, LANES:2 * LANES] + bias[8:9, LANES:2 * LANES]


def _compress_prompt_kernel(k_ref, v_ref, pe_ref, w_ref, out_ref, *, m):
    _compress_rows(k_ref, v_ref, pe_ref, w_ref, out_ref, m)


def _compress_prompt(kc, vc, pe, w, *, batch, seq):
    m = seq // CMP_STRIDE
    const = lambda shape: pl.BlockSpec(shape, lambda i: (0,) * len(shape))
    return pl.pallas_call(
        functools.partial(_compress_prompt_kernel, m=m),
        grid=(batch,),
        in_specs=[pl.BlockSpec((seq, LANES), lambda i: (i, 0)), pl.BlockSpec((seq, LANES), lambda i: (i, 0)), const(pe.shape), const(w.shape)],
        out_specs=pl.BlockSpec((m, 4 * LANES), lambda i: (i, 0)),
        out_shape=jax.ShapeDtypeStruct((batch * m, 4 * LANES), F32),
        compiler_params=_params(("arbitrary",)),
        name="compress_prompt",
    )(kc, vc, pe, w)


def _page_copy(pt_ref, cache_ref, buf_ref, sem_ref, b, p, slot, *, page_base, kind0):
    page = pt_ref[b, p] + page_base
    return pltpu.make_async_copy(
        cache_ref.at[pl.ds(page * KV_COLS + kind0 * LANES, 2 * LANES), :],
        buf_ref.at[slot, pl.ds(p * 2 * LANES, 2 * LANES), :],
        sem_ref.at[slot])


def _page_pipeline(copy, b, nb, n_pages):
    slot = b % 2

    def start_all(s, sl):
        def body(p, carry):
            copy(s, p, sl).start()
            return carry
        lax.fori_loop(0, n_pages, body, 0)

    @pl.when(b == 0)
    def _():
        start_all(b, slot)

    @pl.when(b + 1 < nb)
    def _():
        start_all(b + 1, 1 - slot)

    def wait_all():
        def body(p, carry):
            copy(b, p, slot).wait()
            return carry
        lax.fori_loop(0, n_pages, body, 0)
    return wait_all


def _compress_sample_kernel(pt_ref, cache_ref, pe_ref, w_ref, out_ref, buf_ref, sem_ref, ktok_ref, vtok_ref, *, n_pages, page_base):
    b = pl.program_id(0)
    slot = b % 2
    copy = functools.partial(_page_copy, pt_ref, cache_ref, buf_ref, sem_ref, page_base=page_base, kind0=0)
    wait_all = _page_pipeline(copy, b, pl.num_programs(0), n_pages)
    wait_all()

    chunks = PAGE_SIZE // CMP_STRIDE

    def tr_page(p):
        r0 = pl.multiple_of(p * 2 * LANES, 2 * LANES)
        t0 = pl.multiple_of(p * chunks * CMP_PITCH, 8)
        kt = buf_ref[slot, pl.ds(r0, LANES), :].T
        vt = buf_ref[slot, pl.ds(r0 + LANES, LANES), :].T
        for c in range(chunks):
            ktok_ref[pl.ds(t0 + c * CMP_PITCH, CMP_STRIDE), :] = kt[c * CMP_STRIDE:(c + 1) * CMP_STRIDE]
            vtok_ref[pl.ds(t0 + c * CMP_PITCH, CMP_STRIDE), :] = vt[c * CMP_STRIDE:(c + 1) * CMP_STRIDE]

    def tr_body(pp, carry):
        for i in range(TR_PAGES):
            tr_page(pp * TR_PAGES + i)
        return carry
    lax.fori_loop(0, n_pages // TR_PAGES, tr_body, 0)
    _compress_rows(ktok_ref, vtok_ref, pe_ref, w_ref, out_ref, n_pages * chunks, pitch=CMP_PITCH)


def _compress_sample(page_table, cache_t, pe, w, *, page_base):
    db, n_pages = page_table.shape
    past = n_pages * PAGE_SIZE
    m = past // CMP_STRIDE
    const = lambda shape: pl.BlockSpec(shape, lambda i, pt: (0,) * len(shape))
    kern = functools.partial(_compress_sample_kernel, n_pages=n_pages, page_base=page_base)
    return pl.pallas_call(
        kern,
        grid_spec=pltpu.PrefetchScalarGridSpec(
            num_scalar_prefetch=1,
            grid=(db,),
            in_specs=[pl.BlockSpec(memory_space=pl.ANY), const(pe.shape), const(w.shape)],
            out_specs=pl.BlockSpec((m, 4 * LANES), lambda i, pt: (i, 0)),
            scratch_shapes=[pltpu.VMEM((2, n_pages * 2 * LANES, LANES), F32), pltpu.SemaphoreType.DMA((2,)),
                            pltpu.VMEM((m * CMP_PITCH, LANES), F32), pltpu.VMEM((m * CMP_PITCH, LANES), F32)],
        ),
        out_shape=jax.ShapeDtypeStruct((db * m, 4 * LANES), F32),
        compiler_params=_params(("arbitrary",)),
        name="compress_sample",
    )(page_table, cache_t, pe, w)


def _compressed_kv(p_all, kg, seg1):
    nc = p_all.shape[0]
    up = lambda a: pltpu.roll(a, nc - 1, 0)
    kc_raw = p_all[:, 0:LANES] + up(p_all[:, LANES:2 * LANES])
    vc = p_all[:, 2 * LANES:3 * LANES] + up(p_all[:, 3 * LANES:4 * LANES])
    return _seg_rms(kc_raw, kg, seg1), vc


def _head_slope(h):
    return 2.0 ** -(h + 1)


def _nsa_prompt_kernel(qt_ref, k_ref, vt_ref, p_ref, gt_ref, kg_ref, seg_ref, tt_ref, o_ref,
                       sc_ref, sel_ref, m_ref, l_ref, acc_ref, *, tq, nkt, n_blk):
    q0 = pl.program_id(1) * tq
    q_last = q0 + tq - 1
    hpg = NSA_HEADS // NSA_GROUPS
    lane_blk = lambda a, h: a[:, h * tq:(h + 1) * tq]

    row = lax.broadcasted_iota(jnp.int32, (LANES, tq), 0)
    cols = [None] * NSA_HEADS
    for c in range(4):
        blk = qt_ref[LANES * c:LANES * (c + 1), :]
        cols[c] = jnp.where(row < HEAD_DIM, blk, jnp.zeros_like(blk))
        cols[4 + c] = jnp.where(row >= HEAD_DIM, blk, jnp.zeros_like(blk))
    qpt = jnp.concatenate(cols, axis=1)

    qpos = q0 + lax.broadcasted_iota(jnp.int32, (1, tq), 1)
    qpos_f = qpos.astype(F32)

    kc, vc = _compressed_kv(p_ref[...], kg_ref[...], seg_ref[...])
    nc = kc.shape[0]
    s_c = _dot(kc.astype(BF16), qpt)
    cstart = lax.broadcasted_iota(jnp.int32, (nc, 1), 0) * CMP_STRIDE
    cdist = jnp.abs(qpos_f - (cstart.astype(F32) + 0.5 * (CMP_LEN - 1)))
    c_ok = (cstart + (CMP_LEN - 1)) <= qpos
    any_ok = (qpos >= CMP_LEN - 1).astype(F32)
    ps = []
    for h in range(NSA_HEADS):
        s = jnp.where(c_ok, lane_blk(s_c, h) - _head_slope(h) * cdist, NEG_INF)
        e = jnp.exp(s - jnp.max(s, axis=0, keepdims=True))
        ps.append(e * (any_ok / jnp.sum(e, axis=0, keepdims=True)))
    o_cmp = _dot(vc.T.astype(BF16), jnp.concatenate(ps, axis=1).astype(BF16))

    imps = []
    for g in range(NSA_GROUPS):
        acc = ps[g * hpg]
        for h in range(1, hpg):
            acc = acc + ps[g * hpg + h]
        imps.append(acc)
    tt = tt_ref[...]
    blk_t = None
    for part in _split3(jnp.concatenate(imps, axis=1)):
        d = _dot(tt, part)
        blk_t = d if blk_t is None else blk_t + d
    nbp = tt.shape[0]
    j_t = lax.broadcasted_iota(jnp.int32, (nbp, 2 * tq), 0)
    qcol = lax.broadcasted_iota(jnp.int32, (nbp, 2 * tq), 1) & (tq - 1)
    back = ((q0 + qcol) >> _log2(SEL_LEN)) - j_t
    forced = (j_t == 0) | ((back >= 0) & (back < N_LOCAL))
    sc_ref[...] = jnp.where(back >= 0, blk_t + jnp.where(forced, FORCE_BONUS, 0.0), -1.0)

    assert tq % (2 * SEL_LEN) == 0 and n_blk % 2 == 0

    def rank_body(ip, cnt):
        sc = sc_ref[...]
        for i in (2 * ip, 2 * ip + 1):
            r = sc_ref[pl.ds(i, 1), :]
            cnt = cnt + jnp.where(j_t > i, (r >= sc).astype(F32), (r > sc).astype(F32))
        return cnt
    n_causal = jnp.minimum(q_last // SEL_LEN + 1, n_blk)
    cnt = lax.fori_loop(0, n_causal // 2, rank_body, jnp.zeros((nbp, 2 * tq), F32))
    sel_ref[...] = jnp.where(cnt < N_SEL, 0.0, NEG_INF)

    def flash_init():
        m_ref[...] = jnp.full(m_ref.shape, NEG_INF, F32)
        l_ref[...] = jnp.zeros(l_ref.shape, F32)
        acc_ref[...] = jnp.zeros(acc_ref.shape, F32)

    def flash_tile(s_t, dist, mb, vt):
        m_old = m_ref[...]
        l_old = l_ref[...]
        pbs, ms, ls, als = [], [], [], []
        for h in range(NSA_HEADS):
            t = lane_blk(s_t, h) - _head_slope(h) * dist + mb[h // hpg]
            m_h = jnp.maximum(lane_blk(m_old, h), jnp.max(t, axis=0, keepdims=True))
            al = jnp.exp(lane_blk(m_old, h) - m_h)
            p = jnp.exp(t - m_h)
            ls.append(al * lane_blk(l_old, h) + jnp.sum(p, axis=0, keepdims=True))
            pbs.append(p.astype(BF16))
            ms.append(m_h)
            als.append(al)
        acc_ref[...] = acc_ref[...] * jnp.concatenate(als, axis=1) + _dot(vt, jnp.concatenate(pbs, axis=1))
        m_ref[...] = jnp.concatenate(ms, axis=1)
        l_ref[...] = jnp.concatenate(ls, axis=1)

    flash_init()
    d0 = (lax.broadcasted_iota(jnp.int32, (nkt, tq), 1) - lax.broadcasted_iota(jnp.int32, (nkt, tq), 0)).astype(F32)

    def sel_tile(kt):
        k0 = pl.multiple_of(kt * nkt, nkt)
        dist = d0 + jnp.asarray(q0 - k0, F32)
        causal = jnp.where(dist < 0.0, NEG_INF, 0.0)
        j0 = k0 // SEL_LEN
        rows = [jnp.broadcast_to(sel_ref[pl.ds(j0 + b, 1), :], (SEL_LEN, 2 * tq)) for b in range(nkt // SEL_LEN)]
        mrow = jnp.concatenate(rows, axis=0)
        mb = [mrow[:, 0:tq] + causal, mrow[:, tq:2 * tq] + causal]
        s_t = _dot(k_ref[pl.ds(k0, nkt), 0:LANES], qpt)
        flash_tile(s_t, dist, mb, vt_ref[0:LANES, pl.ds(k0, nkt)])

    def sel_pair(kp, carry):
        sel_tile(2 * kp)
        sel_tile(2 * kp + 1)
        return carry
    n_kt = q_last // nkt + 1
    lax.fori_loop(0, n_kt // 2, sel_pair, 0)

    @pl.when(n_kt % 2 == 1)
    def _():
        sel_tile(n_kt - 1)
    o_sel = acc_ref[...] * (1.0 / l_ref[...])

    flash_init()
    def win_tile(off, nk):
        ks = pl.multiple_of(q0 - off, tq)
        dist = (lax.broadcasted_iota(jnp.int32, (nk, tq), 1) - lax.broadcasted_iota(jnp.int32, (nk, tq), 0)).astype(F32) + float(off)
        wmask = jnp.where((dist >= 0.0) & (dist < float(WINDOW)), 0.0, NEG_INF)
        s_t = _dot(k_ref[pl.ds(ks, nk), LANES:2 * LANES], qpt)
        flash_tile(s_t, dist, [wmask, wmask], vt_ref[LANES:2 * LANES, pl.ds(ks, nk)])

    @pl.when(q0 >= WINDOW)
    def _():
        for off in range(WINDOW, 0, -2 * tq):
            win_tile(off, 2 * tq)
        win_tile(0, tq)

    @pl.when(q0 < WINDOW)
    def _():
        for off in range(WINDOW, -1, -tq):
            @pl.when(q0 >= off)
            def _():
                win_tile(off, tq)
    o_win = acc_ref[...] * (1.0 / l_ref[...])

    gt = gt_ref[...]
    for c in range(4):
        halves = []
        for h, lo in ((c, 0), (4 + c, HEAD_DIM)):
            g = lambda br: gt[br * NSA_HEADS + h:br * NSA_HEADS + h + 1, :]
            pick = lambda a: a[lo:lo + HEAD_DIM, h * tq:(h + 1) * tq]
            halves.append(g(0) * pick(o_cmp) + g(1) * pick(o_sel) + g(2) * pick(o_win))
        o_ref[:, LANES * c:LANES * (c + 1)] = jnp.concatenate(halves, axis=0).T.astype(o_ref.dtype)


def _nsa_prompt(qt, kb, vt, p, gt, kg, seg1, tt, *, batch, seq, tq, nkt):
    nq = seq // tq
    nc = seq // CMP_STRIDE
    n_blk = -(-seq // SEL_LEN)
    r = NSA_HEADS * tq
    nbp = tt.shape[0]
    const = lambda shape: pl.BlockSpec(shape, lambda b, i: (0,) * len(shape))
    kern = functools.partial(_nsa_prompt_kernel, tq=tq, nkt=nkt, n_blk=n_blk)
    return pl.pallas_call(
        kern,
        grid=(batch, nq),
        in_specs=[
            pl.BlockSpec((None, NSA_WIDTH, tq), lambda b, i: (b, 0, i)),
            pl.BlockSpec((seq, 2 * LANES), lambda b, i: (b, 0)),
            pl.BlockSpec((None, 2 * LANES, seq), lambda b, i: (b, 0, 0)),
            pl.BlockSpec((nc, 4 * LANES), lambda b, i: (b, 0)),
            pl.BlockSpec((None, LANES, tq), lambda b, i: (b, 0, i)),
            const(kg.shape), const(seg1.shape), const(tt.shape),
        ],
        out_specs=pl.BlockSpec((tq, NSA_WIDTH), lambda b, i: (b * nq + i, 0)),
        out_shape=jax.ShapeDtypeStruct((batch * seq, NSA_WIDTH), BF16),
        scratch_shapes=[pltpu.VMEM((nbp, 2 * tq), F32), pltpu.VMEM((nbp, 2 * tq), F32),
                        pltpu.VMEM((1, r), F32), pltpu.VMEM((1, r), F32), pltpu.VMEM((LANES, r), F32)],
        compiler_params=_params(("arbitrary", "arbitrary")),
        name="nsa_prompt",
    )(qt, kb, vt, p, gt, kg, seg1, tt)


def _row_meta(tq, q0):
    r = NSA_HEADS * tq
    rid = lax.broadcasted_iota(jnp.int32, (r, 1), 0)
    hh = rid >> _log2(tq)
    ii = rid & (tq - 1)
    slope = lax.bitcast_convert_type((126 - hh) << 23, F32)
    qpos = q0 + ii
    return ii, slope, qpos


def _flash_init(m_ref, l_ref, acc_ref):
    m_ref[...] = jnp.full(m_ref.shape, NEG_INF, F32)
    l_ref[...] = jnp.zeros(l_ref.shape, F32)
    acc_ref[...] = jnp.zeros(acc_ref.shape, F32)


def _flash_update(m_ref, l_ref, acc_ref, s, pv):
    m_old = m_ref[...]
    m_new = jnp.maximum(m_old, jnp.max(s, axis=-1, keepdims=True))
    alpha = jnp.exp(m_old - m_new)
    p = jnp.exp(s - m_new)
    l_ref[...] = alpha * l_ref[...] + jnp.sum(p, axis=-1, keepdims=True)
    acc_ref[...] = alpha * acc_ref[...] + pv(p.astype(BF16))
    m_ref[...] = m_new


def _expand_mask(mask2b, blk0, nkt, tq):
    nbp = mask2b.shape[1]
    j_e = lax.broadcasted_iota(jnp.int32, (nbp, nkt), 0)
    c_e = lax.broadcasted_iota(jnp.int32, (nbp, nkt), 1)
    e = (j_e == blk0 + (c_e >> _log2(SEL_LEN))).astype(BF16)
    me2 = _dot(mask2b, e)
    hpg = NSA_HEADS // NSA_GROUPS
    return jnp.concatenate([me2[0:tq]] * hpg + [me2[tq:2 * tq]] * hpg, axis=0)


def _nsa_sample_kernel(pt_ref, q_ref, p_ref, rows_ref, wnew_ref, wst_ref, gates_ref, kg_ref, seg_ref, tmat_ref, cache_ref,
                       o_ref, buf_ref, sem_ref, m_ref, l_ref, acc_ref, *, tq, pages_per_tile, n_pages, page_base):
    b = pl.program_id(0)
    slot = b % 2
    past = n_pages * PAGE_SIZE
    wb = wst_ref.shape[1]
    hpg = NSA_HEADS // NSA_GROUPS
    copy = functools.partial(_page_copy, pt_ref, cache_ref, buf_ref, sem_ref, page_base=page_base, kind0=2)
    wait_all = _page_pipeline(copy, b, pl.num_programs(0), n_pages)

    r = NSA_HEADS * tq
    qp = jnp.concatenate([q_ref[:, LANES * h:LANES * (h + 1)] for h in range(NSA_HEADS)], axis=0).astype(BF16)
    ii, slope, qpos = _row_meta(tq, past)

    kc, vc = _compressed_kv(p_ref[...], kg_ref[...], seg_ref[...])
    nc = kc.shape[0]
    s = _dot_nt(qp, kc.astype(BF16))
    cstart = lax.broadcasted_iota(jnp.int32, (1, nc), 1) * CMP_STRIDE
    cdist = jnp.abs(qpos.astype(F32) - (cstart.astype(F32) + 0.5 * (CMP_LEN - 1)))
    c_ok = (cstart + (CMP_LEN - 1)) <= qpos
    s = jnp.where(c_ok, s - slope * cdist, NEG_INF)
    e = jnp.exp(s - jnp.max(s, axis=-1, keepdims=True))
    p_cmp = e / jnp.sum(e, axis=-1, keepdims=True) * (qpos >= CMP_LEN - 1).astype(F32)
    o_cmp = _dot(p_cmp.astype(BF16), vc.astype(BF16))

    imps = []
    for g in range(NSA_GROUPS):
        acc = p_cmp[(g * hpg) * tq:(g * hpg + 1) * tq]
        for h in range(1, hpg):
            acc = acc + p_cmp[(g * hpg + h) * tq:(g * hpg + h + 1) * tq]
        imps.append(acc)
    tmat = tmat_ref[...]
    blk = None
    for part in _split3(jnp.concatenate(imps, axis=0)):
        d = _dot(part, tmat)
        blk = d if blk is None else blk + d
    nbp = tmat.shape[1]
    n_blk = -(-(past + tq) // SEL_LEN)
    j_l = lax.broadcasted_iota(jnp.int32, (2 * tq, nbp), 1)
    qrow = lax.broadcasted_iota(jnp.int32, (2 * tq, nbp), 0) & (tq - 1)
    back = ((past + qrow) >> _log2(SEL_LEN)) - j_l
    forced = (j_l == 0) | ((back >= 0) & (back < N_LOCAL))
    score = jnp.where(back >= 0, blk + jnp.where(forced, FORCE_BONUS, 0.0), -1.0)
    cnt = jnp.zeros((2 * tq, nbp), F32)
    for i in range(n_blk):
        col = score[:, i:i + 1]
        cnt = cnt + jnp.where(j_l > i, (col >= score).astype(F32), (col > score).astype(F32))
    mask2 = (cnt < N_SEL).astype(F32)
    mask2b = mask2.astype(BF16)

    wait_all()

    _flash_init(m_ref, l_ref, acc_ref)
    nkt = pages_per_tile * PAGE_SIZE
    dist0 = (past + ii - lax.broadcasted_iota(jnp.int32, (r, nkt), 1)).astype(F32)

    def sel_body(kt, carry):
        k0 = kt * nkt
        tiles = []
        for i in range(pages_per_tile):
            r0 = pl.multiple_of((kt * pages_per_tile + i) * 2 * LANES, 2 * LANES)
            tiles.append((buf_ref[slot, pl.ds(r0, LANES), :].astype(BF16), buf_ref[slot, pl.ds(r0 + LANES, LANES), :].astype(BF16)))
        s = jnp.concatenate([_dot(qp, kt_i) for kt_i, _ in tiles], axis=1)
        me = _expand_mask(mask2b, k0 // SEL_LEN, nkt, tq)
        s = jnp.where(me > 0.5, s - slope * (dist0 - jnp.asarray(k0, F32)), NEG_INF)

        def pv(pb):
            out = None
            for i, (_, vt_i) in enumerate(tiles):
                d = _dot_nt(pb[:, PAGE_SIZE * i:PAGE_SIZE * (i + 1)], vt_i)
                out = d if out is None else out + d
            return out
        _flash_update(m_ref, l_ref, acc_ref, s, pv)
        return carry
    lax.fori_loop(0, n_pages // pages_per_tile, sel_body, 0)

    zpad = jnp.zeros((LANES - tq, LANES), F32)
    dist_new = (ii - lax.broadcasted_iota(jnp.int32, (r, LANES), 1)).astype(F32)
    new_blk = past // SEL_LEN
    me_new = jnp.concatenate([mask2[0:tq, new_blk:new_blk + 1]] * hpg + [mask2[tq:2 * tq, new_blk:new_blk + 1]] * hpg, axis=0)
    k_new = jnp.concatenate([rows_ref[:, 2 * LANES:3 * LANES], zpad], axis=0).astype(BF16)
    v_new = jnp.concatenate([rows_ref[:, 3 * LANES:4 * LANES], zpad], axis=0).astype(BF16)
    ok = (me_new > 0.5) & (dist_new >= 0.0)
    s = jnp.where(ok, _dot_nt(qp, k_new) - slope * dist_new, NEG_INF)
    _flash_update(m_ref, l_ref, acc_ref, s, lambda pb: _dot(pb, v_new))
    o_sel = acc_ref[...] / l_ref[...]

    _flash_init(m_ref, l_ref, acc_ref)
    dist_w = (ii + wb - lax.broadcasted_iota(jnp.int32, (r, wb), 1)).astype(F32)
    ok = (dist_w >= 0.0) & (dist_w < float(WINDOW))
    s = jnp.where(ok, _dot(qp, wst_ref[0:LANES, :].astype(BF16)) - slope * dist_w, NEG_INF)
    vwt = wst_ref[LANES:2 * LANES, :].astype(BF16)
    _flash_update(m_ref, l_ref, acc_ref, s, lambda pb: _dot_nt(pb, vwt))
    wnew = wnew_ref[...]
    kw_new = jnp.concatenate([wnew[:, 0:LANES], zpad], axis=0).astype(BF16)
    vw_new = jnp.concatenate([wnew[:, LANES:2 * LANES], zpad], axis=0).astype(BF16)
    s = jnp.where(dist_new >= 0.0, _dot_nt(qp, kw_new) - slope * dist_new, NEG_INF)
    _flash_update(m_ref, l_ref, acc_ref, s, lambda pb: _dot(pb, vw_new))
    o_win = acc_ref[...] / l_ref[...]

    gates = gates_ref[...]
    lane = lax.broadcasted_iota(jnp.int32, (tq, LANES), 1)
    outs = []
    for h in range(NSA_HEADS):
        rs = slice(h * tq, (h + 1) * tq)
        outs.append(gates[:, h:h + 1] * o_cmp[rs] + gates[:, NSA_HEADS + h:NSA_HEADS + h + 1] * o_sel[rs]
                    + gates[:, 2 * NSA_HEADS + h:2 * NSA_HEADS + h + 1] * o_win[rs])
    for c in range(4):
        o_ref[:, LANES * c:LANES * (c + 1)] = jnp.where(lane < HEAD_DIM, outs[c], outs[4 + c])


def _nsa_sample(page_table, q, p, rows, wnew, wstate_t, gates, kg, seg1, tmat, cache_t, *, tq, pages_per_tile, page_base, wstate_base):
    db, n_pages = page_table.shape
    past = n_pages * PAGE_SIZE
    nc = past // CMP_STRIDE
    wb = wstate_t.shape[1]
    r = NSA_HEADS * tq
    const = lambda shape: pl.BlockSpec(shape, lambda b, pt: (0,) * len(shape))
    kern = functools.partial(_nsa_sample_kernel, tq=tq, pages_per_tile=pages_per_tile, n_pages=n_pages, page_base=page_base)
    return pl.pallas_call(
        kern,
        grid_spec=pltpu.PrefetchScalarGridSpec(
            num_scalar_prefetch=1,
            grid=(db,),
            in_specs=[
                pl.BlockSpec((tq, 2 * NSA_WIDTH), lambda b, pt: (b, 0)),
                pl.BlockSpec((nc, 4 * LANES), lambda b, pt: (b, 0)),
                pl.BlockSpec((tq, KV_COLS), lambda b, pt: (b, 0)),
                pl.BlockSpec((tq, 2 * LANES), lambda b, pt: (b, 0)),
                pl.BlockSpec((2 * LANES, wb), lambda b, pt: (wstate_base + b, 0)),
                pl.BlockSpec((tq, LANES), lambda b, pt: (b, 0)),
                const(kg.shape), const(seg1.shape), const(tmat.shape),
                pl.BlockSpec(memory_space=pl.ANY),
            ],
            out_specs=pl.BlockSpec((tq, NSA_WIDTH), lambda b, pt: (b, 0)),
            scratch_shapes=[pltpu.VMEM((2, n_pages * 2 * LANES, LANES), F32), pltpu.SemaphoreType.DMA((2,)),
                            pltpu.VMEM((r, 1), F32), pltpu.VMEM((r, 1), F32), pltpu.VMEM((r, LANES), F32)],
        ),
        out_shape=jax.ShapeDtypeStruct((db * tq, NSA_WIDTH), F32),
        compiler_params=_params(("arbitrary",)),
        name="nsa_sample",
    )(page_table, q, p, rows, wnew, wstate_t, gates, kg, seg1, tmat, cache_t)


def _hgrn_consts(tc):
    nl = int(np.log2(tc))
    t = np.arange(tc)[:, None]
    u = np.arange(tc)[None, :]
    nl_mxu = min(3, nl)
    blocks = [(u <= t)]
    masks = [np.eye(tc, dtype=bool)]
    for lv in range(nl):
        hs = 1 << lv
        mid = (t // (2 * hs)) * 2 * hs + hs
        ref = mid - 1
        if lv < nl_mxu:
            blocks.append((t >= mid) & (u > ref) & (u <= t))
            blocks.append((t < mid) & (u > t) & (u <= ref))
        masks.append((t // (2 * hs) == u // (2 * hs)) & (t % (2 * hs) >= hs) & (u % (2 * hs) < hs))
    cm = np.concatenate(blocks, axis=0).astype(np.float32)
    mk = np.concatenate(masks, axis=0).astype(np.float32)
    return jnp.asarray(cm, BF16), jnp.asarray(mk, F32), nl, nl_mxu


def _hgrn_kernel(cm_ref, mk_ref, hq_ref, hlf_ref, hk_ref, hv_ref, hg_ref, og_ref, s0_ref, o_ref, sout_ref, st_ref, *, tc, nl, nl_mxu, nb):
    t = pl.program_id(1)
    nt = pl.num_programs(1)

    @pl.when(t == 0)
    def _():
        for i in range(nb):
            for h in range(HGRN_HEADS):
                st_ref[i, h] = s0_ref[i, h].T

    cm = cm_ref[...]
    row_id = lax.broadcasted_iota(jnp.int32, (tc, LANES), 0)
    for i in range(nb):
        for h in range(HGRN_HEADS):
            sl = slice(HGRN_DK * h, HGRN_DK * (h + 1))
            hi, lo = _split2(hlf_ref[i, :, sl])
            ee = _dot(cm, jnp.concatenate([hi, lo], axis=1))
            ee = ee[:, 0:LANES] + ee[:, LANES:2 * LANES]
            b = ee[0:tc]
            q = hq_ref[i, :, sl]
            k = hk_ref[i, :, sl]
            v = hv_ref[i, :, sl].astype(BF16)
            xb = jnp.exp(b)
            xs = jnp.exp(b[tc - 1:tc, :] - b)
            a = mk_ref[0:tc, :] * _dot_nt(q.astype(BF16), k.astype(BF16))
            for lv in range(nl):
                if lv < nl_mxu:
                    xu = jnp.exp(ee[(1 + 2 * lv) * tc:(2 + 2 * lv) * tc])
                    xl = jnp.exp(ee[(2 + 2 * lv) * tc:(3 + 2 * lv) * tc])
                else:
                    hs = 1 << lv
                    bref = jnp.concatenate([jnp.broadcast_to(b[j + hs - 1:j + hs, :], (2 * hs, LANES)) for j in range(0, tc, 2 * hs)], axis=0)
                    upper = ((row_id >> lv) & 1) == 1
                    diff = b - bref
                    xu = jnp.exp(jnp.where(upper, diff, 0.0))
                    xl = jnp.exp(jnp.where(upper, 0.0, -diff))
                a = a + mk_ref[(1 + lv) * tc:(2 + lv) * tc, :] * _dot_nt((q * xu).astype(BF16), (k * xl).astype(BF16))
            st = st_ref[i, h]
            o = _dot(a.astype(BF16), v) + _dot_nt((q * xb).astype(BF16), st.astype(BF16))
            st_new = st * xb[tc - 1:tc, :] + _dot_tn(v, (k * xs).astype(BF16))
            st_ref[i, h] = st_new
            on = o * lax.rsqrt(jnp.mean(o * o, axis=-1, keepdims=True) + EPS) * og_ref[...]
            o_ref[i, :, sl] = (on * hg_ref[i, :, sl]).astype(o_ref.dtype)

    @pl.when(t == nt - 1)
    def _():
        for i in range(nb):
            for h in range(HGRN_HEADS):
                sout_ref[i, h] = st_ref[i, h].T


def _hgrn(hq, hlf, hk, hv, hg, og, s0, *, batch, seq, tc, s0_base):
    cm, mk, nl, nl_mxu = _hgrn_consts(tc)
    nt = seq // tc
    nb = 2 if batch % 2 == 0 and s0_base % 2 == 0 else 1
    const = lambda shape: pl.BlockSpec(shape, lambda b, t: (0,) * len(shape))
    row = pl.BlockSpec((nb, tc, HGRN_WIDTH), lambda b, t: (b, t, 0))
    state = lambda base: pl.BlockSpec((nb, HGRN_HEADS, HGRN_DK, HGRN_DV), lambda b, t: (base + b, 0, 0, 0))
    kern = functools.partial(_hgrn_kernel, tc=tc, nl=nl, nl_mxu=nl_mxu, nb=nb)
    seq3 = lambda a: a.reshape(batch, seq, HGRN_WIDTH)
    o, s_out = pl.pallas_call(
        kern,
        grid=(batch // nb, nt),
        in_specs=[const(cm.shape), const(mk.shape), row, row, row, row, row, const(og.shape), state(s0_base // nb)],
        out_specs=[row, state(0)],
        out_shape=[jax.ShapeDtypeStruct((batch, seq, HGRN_WIDTH), BF16),
                   jax.ShapeDtypeStruct((batch, HGRN_HEADS, HGRN_DK, HGRN_DV), F32)],
        scratch_shapes=[pltpu.VMEM((nb, HGRN_HEADS, HGRN_DV, HGRN_DK), F32)],
        compiler_params=_params(("arbitrary", "arbitrary")),
        name="hgrn",
    )(cm, mk, seq3(hq), seq3(hlf), seq3(hk), seq3(hv), seq3(hg), og, s0)
    return o.reshape(batch * seq, HGRN_WIDTH), s_out


def _outproj_kernel(*refs, moe):
    if moe:
        x_ref, on_ref, oh_ref, w_ref, g_ref, r_ref, xo_ref, h_ref, gate_ref = refs
    else:
        x_ref, on_ref, oh_ref, w_ref, g_ref, xo_ref, h_ref = refs
    xn = x_ref[...] + _dot(on_ref[...], w_ref[0:NSA_WIDTH, :]) + _dot(oh_ref[...], w_ref[NSA_WIDTH:NSA_WIDTH + HGRN_WIDTH, :])
    xo_ref[...] = xn
    h = xn * lax.rsqrt(jnp.mean(xn * xn, axis=-1, keepdims=True) + EPS) * g_ref[...]
    h_ref[...] = h.astype(BF16)
    if moe:
        logits = None
        rparts = _split3(r_ref[...])
        hparts = _split3(h)
        for i in range(3):
            for j in range(3 - i):
                d = _dot(hparts[i], rparts[j])
                logits = d if logits is None else logits + d
        lane = lax.broadcasted_iota(jnp.int32, logits.shape, 1).astype(F32)
        lg = jnp.where(lane < N_EXPERTS, logits, NEG_INF)
        m1 = jnp.max(lg, axis=-1, keepdims=True)
        i1 = jnp.min(jnp.where(lg == m1, lane, float(LANES)), axis=-1, keepdims=True)
        lg2 = jnp.where(lane == i1, NEG_INF, lg)
        m2 = jnp.max(lg2, axis=-1, keepdims=True)
        i2 = jnp.min(jnp.where(lg2 == m2, lane, float(LANES)), axis=-1, keepdims=True)
        e2 = jnp.exp(m2 - m1)
        den = 1.0 + e2
        gate_ref[...] = (jnp.where(lane == 0.0, i1, 0.0) + jnp.where(lane == 1.0, i2, 0.0)
                         + jnp.where(lane == 2.0, 1.0 / den, 0.0) + jnp.where(lane == 3.0, e2 / den, 0.0))


def _outproj(x, o_nsa, o_hg, w, g, router, tm):
    t, d = x.shape
    moe = router is not None
    const = lambda shape: pl.BlockSpec(shape, lambda i: (0,) * len(shape))
    row = lambda n: pl.BlockSpec((tm, n), lambda i: (i, 0))
    in_specs = [row(d), row(NSA_WIDTH), row(HGRN_WIDTH), const(w.shape), const((1, d))]
    args = [x, o_nsa, o_hg, w, g]
    out_specs = [row(d), row(d)]
    out_shape = [jax.ShapeDtypeStruct((t, d), F32), jax.ShapeDtypeStruct((t, d), BF16)]
    if moe:
        in_specs.append(const(router.shape))
        args.append(router)
        out_specs.append(row(LANES))
        out_shape.append(jax.ShapeDtypeStruct((t, LANES), F32))
    return pl.pallas_call(
        functools.partial(_outproj_kernel, moe=moe),
        grid=(t // tm,),
        in_specs=in_specs, out_specs=out_specs, out_shape=out_shape,
        compiler_params=_params(("arbitrary",)),
        name="outproj_moe" if moe else "outproj",
    )(*args)


def _ffn_kernel(x_ref, h_ref, wg_ref, wu_ref, wd_ref, o_ref):
    f = pl.program_id(1)
    h = h_ref[...]
    a = _dot(h, wg_ref[...])
    b = _dot(h, wu_ref[...])
    d = _dot((a * _sigmoid(a) * b).astype(BF16), wd_ref[...])

    @pl.when(f == 0)
    def _():
        o_ref[...] = x_ref[...] + d

    @pl.when(f > 0)
    def _():
        o_ref[...] = o_ref[...] + d


def _ffn(x, h, wg, wu, wd, tm, tf):
    t, d = x.shape
    ff = wg.shape[1]
    return pl.pallas_call(
        _ffn_kernel,
        grid=(t // tm, ff // tf),
        in_specs=[pl.BlockSpec((tm, d), lambda i, f: (i, 0)), pl.BlockSpec((tm, d), lambda i, f: (i, 0)),
                  pl.BlockSpec((d, tf), lambda i, f: (0, f)), pl.BlockSpec((d, tf), lambda i, f: (0, f)),
                  pl.BlockSpec((tf, d), lambda i, f: (f, 0))],
        out_specs=pl.BlockSpec((tm, d), lambda i, f: (i, 0)),
        out_shape=jax.ShapeDtypeStruct((t, d), F32),
        compiler_params=_params(("arbitrary", "arbitrary")),
        name="ffn",
    )(x, h, wg, wu, wd)


def _route_kernel(rt_ref, pos_ref, post_ref, cnt_ref, carry_ref, *, ts):
    s = pl.program_id(0)

    @pl.when(s == 0)
    def _():
        carry_ref[...] = jnp.zeros(carry_ref.shape, F32)

    rt = rt_ref[...]
    lane = lax.broadcasted_iota(jnp.int32, (ts, LANES), 1).astype(F32)
    c = jnp.where((lane == rt[:, 0:1]) | (lane == rt[:, 1:2]), 1.0, 0.0)
    lt = (lax.broadcasted_iota(jnp.int32, (ts, ts), 1) < lax.broadcasted_iota(jnp.int32, (ts, ts), 0)).astype(BF16)
    rank = _dot(lt, c.astype(BF16)) + carry_ref[...]
    pos = jnp.where(c > 0.0, rank, -1.0)
    pos_ref[...] = pos
    post_ref[...] = pos.T[0:8, :]
    n = jnp.sum(c, axis=0, keepdims=True)
    cnt_ref[...] = jnp.broadcast_to(n, cnt_ref.shape)
    carry_ref[...] = carry_ref[...] + n


def _route(rt, ts):
    t = rt.shape[0]
    ns = t // ts
    return pl.pallas_call(
        functools.partial(_route_kernel, ts=ts),
        grid=(ns,),
        in_specs=[pl.BlockSpec((ts, LANES), lambda s: (s, 0))],
        out_specs=[pl.BlockSpec((ts, LANES), lambda s: (s, 0)), pl.BlockSpec((8, ts), lambda s: (0, s)),
                   pl.BlockSpec((None, 8, LANES), lambda s: (s, 0, 0))],
        out_shape=[jax.ShapeDtypeStruct((t, LANES), F32), jax.ShapeDtypeStruct((8, t), F32),
                   jax.ShapeDtypeStruct((ns, 8, LANES), F32)],
        scratch_shapes=[pltpu.VMEM((1, LANES), F32)],
        compiler_params=_params(("arbitrary",)),
        name="moe_route",
    )(rt)


def _moe_gather_kernel(texp_ref, trank_ref, slo_ref, nsrc_ref, post_ref, h_ref, o_ref, hbuf_ref, sem_ref, acc_ref, *, tm, ts):
    j = pl.program_id(0)
    e = texp_ref[j]
    r0 = trank_ref[j]
    s0 = slo_ref[j]
    n = nsrc_ref[j]
    acc_ref[...] = jnp.zeros(acc_ref.shape, F32)
    row = lax.broadcasted_iota(jnp.int32, (tm, 1), 0).astype(F32) + r0.astype(F32)

    def copy(s, slot):
        return pltpu.make_async_copy(h_ref.at[pl.ds(pl.multiple_of(s * ts, ts), ts), :], hbuf_ref.at[slot], sem_ref.at[slot])

    @pl.when(n > 0)
    def _():
        copy(s0, 0).start()

    def body(k, carry):
        slot = k % 2

        @pl.when(k + 1 < n)
        def _():
            copy(s0 + k + 1, 1 - slot).start()
        copy(s0 + k, slot).wait()
        prow = post_ref[pl.ds(e, 1), pl.ds(pl.multiple_of((s0 + k) * ts, ts), ts)]
        onehot = jnp.where(prow == row, 1.0, 0.0).astype(BF16)
        acc_ref[...] = acc_ref[...] + _dot(onehot, hbuf_ref[slot])
        return carry
    lax.fori_loop(0, n, body, 0)
    o_ref[...] = acc_ref[...].astype(o_ref.dtype)


def _moe_gather(tables, post, h, *, n_tiles, tm, ts):
    t, d = h.shape
    return pl.pallas_call(
        functools.partial(_moe_gather_kernel, tm=tm, ts=ts),
        grid_spec=pltpu.PrefetchScalarGridSpec(
            num_scalar_prefetch=4,
            grid=(n_tiles,),
            in_specs=[pl.BlockSpec(post.shape, lambda j, *_: (0, 0)), pl.BlockSpec(memory_space=pl.ANY)],
            out_specs=pl.BlockSpec((tm, d), lambda j, *_: (j, 0)),
            scratch_shapes=[pltpu.VMEM((2, ts, d), BF16), pltpu.SemaphoreType.DMA((2,)), pltpu.VMEM((tm, d), F32)],
        ),
        out_shape=jax.ShapeDtypeStruct((n_tiles * tm, d), BF16),
        compiler_params=_params(("arbitrary",)),
        name="moe_gather",
    )(*tables, post, h)


def _moe_ffn_kernel(texp_ref, tused_ref, h_ref, wg_ref, wu_ref, wd_ref, o_ref, acc_ref):
    f = pl.program_id(1)
    last = pl.num_programs(1) - 1
    used = tused_ref[pl.program_id(0)] > 0

    @pl.when(used)
    def _():
        h = h_ref[...]
        a = _dot(h, wg_ref[...])
        b = _dot(h, wu_ref[...])
        d = _dot((a * _sigmoid(a) * b).astype(BF16), wd_ref[...])

        @pl.when(f == 0)
        def _():
            acc_ref[...] = d

        @pl.when(f > 0)
        def _():
            acc_ref[...] = acc_ref[...] + d

        @pl.when(f == last)
        def _():
            o_ref[...] = acc_ref[...].astype(o_ref.dtype)

    @pl.when(jnp.logical_not(used) & (f == last))
    def _():
        o_ref[...] = jnp.zeros(o_ref.shape, o_ref.dtype)


def _moe_ffn(texp, tused, hs, wg, wu, wd, *, tm, tf):
    rows, d = hs.shape
    ff = wg.shape[2]
    return pl.pallas_call(
        _moe_ffn_kernel,
        grid_spec=pltpu.PrefetchScalarGridSpec(
            num_scalar_prefetch=2,
            grid=(rows // tm, ff // tf),
            in_specs=[pl.BlockSpec((tm, d), lambda j, f, te, tu: (j, 0)),
                      pl.BlockSpec((None, d, tf), lambda j, f, te, tu: (te[j], 0, f)),
                      pl.BlockSpec((None, d, tf), lambda j, f, te, tu: (te[j], 0, f)),
                      pl.BlockSpec((None, tf, d), lambda j, f, te, tu: (te[j], f, 0))],
            out_specs=pl.BlockSpec((tm, d), lambda j, f, te, tu: (j, 0)),
            scratch_shapes=[pltpu.VMEM((tm, d), F32)],
        ),
        out_shape=jax.ShapeDtypeStruct((rows, d), BF16),
        compiler_params=_params(("arbitrary", "arbitrary")),
        name="moe_ffn",
    )(texp, tused, hs, wg, wu, wd)


def _moe_combine_kernel(start_ref, delta_ref, x_ref, rt_ref, pos_ref, ys_ref, o_ref, ybuf_ref, sem_ref, *, ts, win):
    s = pl.program_id(0)
    ns = pl.num_programs(0)
    slot = s % 2

    def copy(step, e, sl):
        st = pl.multiple_of(start_ref[step * N_EXPERTS + e], LANES)
        return pltpu.make_async_copy(ys_ref.at[pl.ds(st, win), :], ybuf_ref.at[sl, e], sem_ref.at[sl])

    def start_all(step, sl):
        for e in range(N_EXPERTS):
            copy(step, e, sl).start()

    @pl.when(s == 0)
    def _():
        start_all(s, slot)

    @pl.when(s + 1 < ns)
    def _():
        start_all(s + 1, 1 - slot)

    rt = rt_ref[...]
    pos = pos_ref[...]
    col = lax.broadcasted_iota(jnp.int32, (1, win), 1).astype(F32)
    for e in range(N_EXPERTS):
        copy(s, e, slot).wait()
    y = x_ref[...]
    for e in range(N_EXPERTS):
        rel = pos[:, e:e + 1] + delta_ref[s * N_EXPERTS + e].astype(F32)
        g = jnp.where(rel == col, 1.0, 0.0).astype(BF16)
        w = jnp.where(rt[:, 0:1] == float(e), rt[:, 2:3], 0.0) + jnp.where(rt[:, 1:2] == float(e), rt[:, 3:4], 0.0)
        y = y + w * _dot(g, ybuf_ref[slot, e])
    o_ref[...] = y


def _moe_combine(start, delta, x, rt, pos, ys, *, ts, win):
    t, d = x.shape
    row = lambda n: pl.BlockSpec((ts, n), lambda s, *_: (s, 0))
    return pl.pallas_call(
        functools.partial(_moe_combine_kernel, ts=ts, win=win),
        grid_spec=pltpu.PrefetchScalarGridSpec(
            num_scalar_prefetch=2,
            grid=(t // ts,),
            in_specs=[row(d), row(LANES), row(LANES), pl.BlockSpec(memory_space=pl.ANY)],
            out_specs=row(d),
            scratch_shapes=[pltpu.VMEM((2, N_EXPERTS, win, d), BF16), pltpu.SemaphoreType.DMA((2,))],
        ),
        out_shape=jax.ShapeDtypeStruct((t, d), F32),
        compiler_params=_params(("arbitrary",)),
        name="moe_combine",
    )(start, delta, x, rt, pos, ys)


def _moe(x, h, rt, wg, wu, wd, *, tm, ts, tf):
    t, d = x.shape
    win = ts + LANES
    n_tiles = -(-2 * t // tm) + N_EXPERTS + 1
    pos, post, cnt = _route(rt, ts)
    counts = cnt[:, 0, :N_EXPERTS].astype(jnp.int32)
    cum = jnp.concatenate([jnp.zeros((1, N_EXPERTS), jnp.int32), jnp.cumsum(counts, axis=0)], axis=0)
    tiles_e = -(-cum[-1] // tm)
    tstart = jnp.concatenate([jnp.zeros((1,), jnp.int32), jnp.cumsum(tiles_e)])
    jt = jnp.arange(n_tiles, dtype=jnp.int32)
    texp = jnp.minimum(jnp.sum(jt[:, None] >= tstart[None, 1:], axis=1), N_EXPERTS - 1).astype(jnp.int32)
    used = jt < tstart[-1]
    trank = jnp.where(used, (jt - tstart[texp]) * tm, -2 * tm).astype(jnp.int32)
    lo = cum[:-1].T[texp]
    hi = cum[1:].T[texp]
    hit = used[:, None] & (lo < trank[:, None] + tm) & (hi > trank[:, None])
    slo = jnp.sum(used[:, None] & (hi <= trank[:, None]), axis=1).astype(jnp.int32)
    nsrc = jnp.sum(hit, axis=1).astype(jnp.int32)
    row0 = tstart[:-1][None, :] * tm + cum[:-1]
    start = (row0 // LANES) * LANES
    delta = tstart[:-1][None, :] * tm - start
    hs = _moe_gather((texp, trank, slo, nsrc), post, h, n_tiles=n_tiles, tm=tm, ts=ts)
    ys = _moe_ffn(texp, used.astype(jnp.int32), hs, wg, wu, wd, tm=tm, tf=tf)
    return _moe_combine(start.reshape(-1).astype(jnp.int32), delta.reshape(-1).astype(jnp.int32), x, rt, pos, ys, ts=ts, win=win)


def _head_perm():
    idx = []
    for c in range(4):
        idx += list(range(HEAD_DIM * c, HEAD_DIM * (c + 1))) + list(range(HEAD_DIM * (4 + c), HEAD_DIM * (5 + c)))
    return np.asarray(idx, np.int32)


def _tap_matrix(n_cmp, nc_pad, n_blk, nb_pad):
    r_s = SEL_LEN // CMP_STRIDE
    r_c = CMP_LEN // CMP_STRIDE
    taps = np.convolve(np.ones(r_s), np.ones(r_c)) / r_c
    tm = np.zeros((nc_pad, nb_pad), np.float32)
    for j in range(n_blk):
        for kk, w in enumerate(taps):
            n = j * r_s + kk - (r_c - 1)
            if 0 <= n < n_cmp:
                tm[n, j] = w
    return tm


def _layer_weights(l, w_in, q_gain, k_gain, cmp_pe, cmp_w, w_out):
    perm = _head_perm()
    wl = w_in[l]
    q_end = NSA_WIDTH
    kv_end = q_end + 6 * NSA_GROUPS * HEAD_DIM
    gate_end = kv_end + 3 * NSA_HEADS
    d = wl.shape[0]
    w_pad = jnp.concatenate([wl[:, :q_end][:, perm], wl[:, q_end:kv_end], wl[:, kv_end:gate_end],
                             jnp.zeros((d, LANES - 3 * NSA_HEADS), wl.dtype), wl[:, gate_end:]], axis=1).astype(BF16)
    qg = jnp.tile(q_gain[l], NSA_HEADS)[None, :]
    kg_proj = jnp.stack([jnp.tile(k_gain[l, 1], NSA_GROUPS), jnp.tile(k_gain[l, 2], NSA_GROUPS)])
    kg_cmp = jnp.tile(k_gain[l, 0], NSA_GROUPS)[None, :]
    pairs = CMP_STRIDE // 2
    cw = cmp_w[l].reshape(2, 2, pairs, 2, HEAD_DIM, HEAD_DIM)
    eye = jnp.eye(NSA_GROUPS, dtype=cw.dtype)
    w_cmp = jnp.einsum('krjsde,gh->kjsgdrhe', cw, eye).reshape(2 * pairs * 2 * LANES, 2 * LANES).astype(BF16)
    pe = cmp_pe[l].reshape(2, 2, pairs, 2, 1, HEAD_DIM)
    pe = jnp.broadcast_to(pe, (2, 2, pairs, 2, NSA_GROUPS, HEAD_DIM)).reshape(4 * pairs, 2 * LANES)
    wo = w_out[l]
    wo = jnp.concatenate([wo[:NSA_WIDTH][perm], wo[NSA_WIDTH:]], axis=0).astype(BF16)
    return w_pad, qg, kg_proj, kg_cmp, w_cmp, pe, wo


def _token_minor(a):
    n = a.ndim
    return jnp.transpose(a, tuple(range(n - 4)) + (n - 3, n - 2, n - 1, n - 4))


def _token_major(a):
    n = a.ndim
    return jnp.transpose(a, tuple(range(n - 4)) + (n - 1, n - 4, n - 3, n - 2))


def kernel(x_prompt, x_sample, cache_kv, state_win_kv, state_hgrn, page_table, norm_mix, norm_ffn, w_in, q_gain, k_gain, cmp_pe, cmp_w, hgrn_lb_logits, hgrn_o_gain, w_out, ffn_w_gate, ffn_w_up, ffn_w_down, moe_router, moe_w_gate, moe_w_up, moe_w_down):
    depth = w_in.shape[0]
    batch, seq, d_model = x_prompt.shape
    db, ds, _ = x_sample.shape
    n_pool = cache_kv.shape[1]
    n_pages = page_table.shape[1]
    past = n_pages * PAGE_SIZE
    wb = state_win_kv.shape[2]
    assert wb == WINDOW and seq % 256 == 0 and seq >= WINDOW and ds == 8 and n_pages % 16 == 0

    tq, nkt_p, tc = 128, 512, 128
    tm_p = 256
    tm_f = 512 if (batch * seq) % 512 == 0 else 256
    tm_e = 512
    ts = db * ds
    d_ff = ffn_w_gate.shape[-1]
    tf = d_ff // 2

    sm = jax.nn.softmax(hgrn_lb_logits.astype(F32), axis=0)
    lower = jnp.concatenate([jnp.zeros_like(sm[:1]), jnp.cumsum(sm[1:], axis=0)], axis=0)
    seg = jnp.asarray(np.kron(np.eye(NSA_HEADS), np.ones((HEAD_DIM, HEAD_DIM))), BF16)
    seg1 = seg[:LANES, :LANES]

    nc_p = seq // CMP_STRIDE
    nb_p = -(-seq // SEL_LEN)
    tt_p = jnp.asarray(_tap_matrix(nc_p - 1, nc_p, nb_p, -(-nb_p // 8) * 8).T, BF16)
    nc_s = past // CMP_STRIDE
    nb_s = -(-(past + ds) // SEL_LEN)
    nbp_s = -(-nb_s // LANES) * LANES
    tm_s = jnp.asarray(_tap_matrix(nc_s - 1, nc_s, nb_s, nbp_s), BF16)

    cache_t = _token_minor(cache_kv).reshape(depth * n_pool * KV_COLS, PAGE_SIZE)
    wstate_t = _token_minor(state_win_kv).reshape(depth * db * 2 * LANES, wb)
    hstate = state_hgrn.reshape(depth * db, HGRN_HEADS, HGRN_DK, HGRN_DV)
    zero_state = jnp.zeros((batch, HGRN_HEADS, HGRN_DK, HGRN_DV), F32)

    xp = x_prompt.reshape(batch * seq, d_model)
    xs = x_sample.reshape(ts, d_model)
    kv_p, kv_s, win_p, win_s, hs_p, hs_s = [], [], [], [], [], []
    for l in range(depth):
        w_pad, qg, kg_proj, kg_cmp, w_cmp, pe, wo = _layer_weights(l, w_in, q_gain, k_gain, cmp_pe, cmp_w, w_out)
        lb = lower[l]
        hp = jnp.stack([jnp.log(lb), jnp.log1p(-lb), 1.0 - lb])
        og = hgrn_o_gain[l][None, :]
        g1 = norm_mix[l][None, :]
        g2 = norm_ffn[l][None, :]
        i = l // 2
        if l % 2 == 0:
            router = None
            wg, wu, wd = ffn_w_gate[i].astype(BF16), ffn_w_up[i].astype(BF16), ffn_w_down[i].astype(BF16)
        else:
            router = jnp.pad(moe_router[i], ((0, 0), (0, LANES - N_EXPERTS)))
            wg, wu, wd = moe_w_gate[i].astype(BF16), moe_w_up[i].astype(BF16), moe_w_down[i].astype(BF16)

        def mixer(x, h, rt):
            tm = tm_f if x.shape[0] % tm_f == 0 else x.shape[0]
            if router is None:
                return _ffn(x, h, wg, wu, wd, tm, tf)
            return _moe(x, h, rt, wg, wu, wd, tm=tm_e, ts=min(tm_p, x.shape[0]), tf=tf)

        qt, rows_t, win_t, kb, vt, kc, vc, gt, hq, hlf, hk, hv, hg = _proj_prompt(xp, g1, w_pad, qg, kg_proj, seg, hp, batch=batch, seq=seq, tm=tm_p)
        pp = _compress_prompt(kc, vc, pe, w_cmp, batch=batch, seq=seq)
        o_nsa = _nsa_prompt(qt, kb, vt, pp, gt, kg_cmp, seg1, tt_p, batch=batch, seq=seq, tq=tq, nkt=nkt_p)
        o_hg, s_fin = _hgrn(hq, hlf, hk, hv, hg, og, zero_state, batch=batch, seq=seq, tc=tc, s0_base=0)
        res = _outproj(xp, o_nsa, o_hg, wo, g2, router, tm_p)
        xp = mixer(res[0], res[1], res[2] if router is not None else None)
        kv_p.append(rows_t.reshape(batch, 4, NSA_GROUPS, HEAD_DIM, seq))
        win_p.append(win_t.reshape(batch, 2, NSA_GROUPS, HEAD_DIM, seq)[..., seq - min(WINDOW, seq):])
        hs_p.append(s_fin)

        q, rows, win, gates, hq, hlf, hk, hv, hg = _proj_sample(xs, g1, w_pad, qg, kg_proj, seg, hp)
        ps = _compress_sample(page_table, cache_t, pe, w_cmp, page_base=l * n_pool)
        o_nsa = _nsa_sample(page_table, q, ps, rows, win, wstate_t, gates, kg_cmp, seg1, tm_s, cache_t,
                            tq=ds, pages_per_tile=16, page_base=l * n_pool, wstate_base=l * db)
        padt = lambda a: jnp.pad(a.reshape(db, ds, HGRN_WIDTH), ((0, 0), (0, tc - ds), (0, 0))).reshape(db * tc, HGRN_WIDTH)
        o_hg, s_new = _hgrn(padt(hq), padt(hlf), padt(hk), padt(hv), padt(hg), og, hstate, batch=db, seq=tc, tc=tc, s0_base=l * db)
        o_hg = o_hg.reshape(db, tc, HGRN_WIDTH)[:, :ds].reshape(ts, HGRN_WIDTH)
        res = _outproj(xs, o_nsa.astype(BF16), o_hg, wo, g2, router, ts)
        xs = mixer(res[0], res[1], res[2] if router is not None else None)
        kv_s.append(rows.reshape(db, ds, 4, NSA_GROUPS, HEAD_DIM))
        win_s.append(win.reshape(db, ds, 2, NSA_GROUPS, HEAD_DIM))
        hs_s.append(s_new)

    new_win_sample = jnp.concatenate([state_win_kv[:, :, ds:], jnp.stack(win_s)], axis=2)
    return (xp.reshape(batch, seq, d_model), xs.reshape(db, ds, d_model), _token_major(jnp.stack(kv_p)), jnp.stack(kv_s),
            _token_major(jnp.stack(win_p)), new_win_sample, jnp.stack(hs_p), jnp.stack(hs_s))
```

```python
import functools

import numpy as np
import jax
import jax.numpy as jnp
from jax import lax
from jax.experimental import pallas as pl
from jax.experimental.pallas import tpu as pltpu

F32 = jnp.float32
BF16 = jnp.bfloat16

NSA_HEADS = 8
NSA_GROUPS = 2
HEAD_DIM = 64
NSA_WIDTH = NSA_HEADS * HEAD_DIM
CMP_LEN = 32
CMP_STRIDE = 16
SEL_LEN = 64
N_SEL = 16
N_LOCAL = 2
WINDOW = 512
FORCE_BONUS = 1e4
HGRN_HEADS = 4
HGRN_DK = 128
HGRN_DV = 128
HGRN_WIDTH = HGRN_HEADS * HGRN_DV
N_EXPERTS = 8
EPS = 1e-6
NEG_INF = -1e30
PAGE_SIZE = 128

LANES = 128
KV_COLS = 4 * NSA_GROUPS * HEAD_DIM
CMP_PITCH = CMP_STRIDE + 1
TR_PAGES = 4
VMEM_LIMIT = 48 * 1024 * 1024


def _log2(n):
    assert n & (n - 1) == 0
    return n.bit_length() - 1


def _dot(a, b):
    return jnp.dot(a, b, preferred_element_type=F32)


def _dot_nt(a, b):
    return lax.dot_general(a, b, (((1,), (1,)), ((), ())), preferred_element_type=F32)


def _dot_tn(a, b):
    return lax.dot_general(a, b, (((0,), (0,)), ((), ())), preferred_element_type=F32)


def _split2(x):
    hi = x.astype(BF16)
    lo = (x - hi.astype(F32)).astype(BF16)
    return hi, lo


def _split3(x):
    hi = x.astype(BF16)
    r = x - hi.astype(F32)
    mid = r.astype(BF16)
    lo = (r - mid.astype(F32)).astype(BF16)
    return hi, mid, lo


def _segsum(x, seg):
    hi, lo = _split2(x)
    return _dot(hi, seg) + _dot(lo, seg)


def _seg_rms(z, gain, seg):
    ss = _segsum(z * z, seg) * (1.0 / HEAD_DIM)
    return z * lax.rsqrt(ss + EPS) * gain


def _sigmoid(x):
    return 1.0 / (1.0 + jnp.exp(-x))


def _params(sem, vmem=VMEM_LIMIT):
    return pltpu.CompilerParams(dimension_semantics=sem, vmem_limit_bytes=vmem)


def _proj_core(x_ref, gain_ref, w_ref, qg_ref, kg_ref, seg_ref, hp_ref, hq_ref, hlf_ref, hk_ref, hv_ref, hg_ref):
    x = x_ref[...]
    ms = jnp.mean(x * x, axis=-1, keepdims=True)
    h = (x * lax.rsqrt(ms + EPS) * gain_ref[...]).astype(BF16)

    def mm(c0, c1):
        return _dot(h, w_ref[:, c0:c1])

    seg1 = seg_ref[0:LANES, 0:LANES]
    qn = _seg_rms(mm(0, NSA_WIDTH), qg_ref[...], seg_ref[...]) * (HEAD_DIM ** -0.5)
    c0 = NSA_WIDTH
    kv = mm(c0, c0 + 6 * LANES)
    kvs = [kv[:, LANES * j:LANES * (j + 1)] for j in range(6)]
    kvs[2] = _seg_rms(kvs[2], kg_ref[0:1, :], seg1)
    kvs[4] = _seg_rms(kvs[4], kg_ref[1:2, :], seg1)
    c0 += 6 * LANES
    gates = _sigmoid(mm(c0, c0 + LANES))
    c0 += LANES
    zq = mm(c0, c0 + HGRN_WIDTH)
    hq_ref[...] = zq * _sigmoid(zq)
    c0 += HGRN_WIDTH
    zf = mm(c0, c0 + HGRN_WIDTH)
    log_sig = jnp.minimum(zf, 0.0) - jnp.log1p(jnp.exp(-jnp.abs(zf)))
    a = hp_ref[0:1, :]
    c = hp_ref[1:2, :] + log_sig
    hlf_ref[...] = jnp.maximum(a, c) + jnp.log1p(jnp.exp(-jnp.abs(a - c)))
    hk_ref[...] = hp_ref[2:3, :] * _sigmoid(-zf)
    c0 += HGRN_WIDTH
    hv_ref[...] = mm(c0, c0 + HGRN_WIDTH)
    c0 += HGRN_WIDTH
    zg = mm(c0, c0 + HGRN_WIDTH)
    hg_ref[...] = zg * _sigmoid(zg)
    return qn, kvs, gates


def _proj_prompt_kernel(x_ref, gain_ref, w_ref, qg_ref, kg_ref, seg_ref, hp_ref,
                        qt_ref, rowst_ref, wint_ref, kb_ref, vt_ref, kc_ref, vc_ref, gt_ref,
                        hq_ref, hlf_ref, hk_ref, hv_ref, hg_ref):
    qn, kvs, gates = _proj_core(x_ref, gain_ref, w_ref, qg_ref, kg_ref, seg_ref, hp_ref, hq_ref, hlf_ref, hk_ref, hv_ref, hg_ref)
    for c in range(4):
        qt_ref[LANES * c:LANES * (c + 1), :] = qn[:, LANES * c:LANES * (c + 1)].T.astype(BF16)
    kvt = [a.T for a in kvs]
    for j in range(4):
        rowst_ref[LANES * j:LANES * (j + 1), :] = kvt[j]
    wint_ref[0:LANES, :] = kvt[4]
    wint_ref[LANES:2 * LANES, :] = kvt[5]
    kb_ref[:, 0:LANES] = kvs[2].astype(BF16)
    kb_ref[:, LANES:2 * LANES] = kvs[4].astype(BF16)
    vt_ref[0:LANES, :] = kvt[3].astype(BF16)
    vt_ref[LANES:2 * LANES, :] = kvt[5].astype(BF16)
    kc_ref[...] = kvs[0]
    vc_ref[...] = kvs[1]
    gt_ref[...] = gates.T


def _proj_sample_kernel(x_ref, gain_ref, w_ref, qg_ref, kg_ref, seg_ref, hp_ref,
                        q_ref, rows_ref, win_ref, gates_ref, hq_ref, hlf_ref, hk_ref, hv_ref, hg_ref):
    qn, kvs, gates = _proj_core(x_ref, gain_ref, w_ref, qg_ref, kg_ref, seg_ref, hp_ref, hq_ref, hlf_ref, hk_ref, hv_ref, hg_ref)
    lane = lax.broadcasted_iota(jnp.int32, (qn.shape[0], LANES), 1)
    for c in range(4):
        blk = qn[:, LANES * c:LANES * (c + 1)]
        q_ref[:, LANES * c:LANES * (c + 1)] = jnp.where(lane < HEAD_DIM, blk, 0.0)
        q_ref[:, LANES * (4 + c):LANES * (5 + c)] = jnp.where(lane >= HEAD_DIM, blk, 0.0)
    for j in range(4):
        rows_ref[:, LANES * j:LANES * (j + 1)] = kvs[j]
    win_ref[:, 0:LANES] = kvs[4]
    win_ref[:, LANES:2 * LANES] = kvs[5]
    gates_ref[...] = gates


def _proj_prompt(x, gain, w, qg, kg, seg, hp, *, batch, seq, tm):
    t, d = x.shape
    nt = seq // tm
    const = lambda shape: pl.BlockSpec(shape, lambda b, i: (0,) * len(shape))
    row = lambda n: pl.BlockSpec((tm, n), lambda b, i: (b * nt + i, 0))
    colt = lambda n: pl.BlockSpec((None, n, tm), lambda b, i: (b, 0, i))
    tshape = lambda n, dt: jax.ShapeDtypeStruct((batch, n, seq), dt)
    rshape = lambda n, dt: jax.ShapeDtypeStruct((t, n), dt)
    return pl.pallas_call(
        _proj_prompt_kernel,
        grid=(batch, nt),
        in_specs=[row(d), const((1, d)), const(w.shape), const(qg.shape), const(kg.shape), const(seg.shape), const(hp.shape)],
        out_specs=[colt(NSA_WIDTH), colt(KV_COLS), colt(2 * LANES), row(2 * LANES), colt(2 * LANES), row(LANES), row(LANES),
                   colt(LANES)] + [row(HGRN_WIDTH)] * 5,
        out_shape=[tshape(NSA_WIDTH, BF16), tshape(KV_COLS, F32), tshape(2 * LANES, F32), rshape(2 * LANES, BF16),
                   tshape(2 * LANES, BF16), rshape(LANES, F32), rshape(LANES, F32), tshape(LANES, F32)]
                  + [rshape(HGRN_WIDTH, F32)] * 5,
        compiler_params=_params(("arbitrary", "arbitrary")),
        name="proj_prompt",
    )(x, gain, w, qg, kg, seg, hp)


def _proj_sample(x, gain, w, qg, kg, seg, hp):
    t, d = x.shape
    const = lambda shape: pl.BlockSpec(shape, lambda i: (0,) * len(shape))
    row = lambda n: pl.BlockSpec((t, n), lambda i: (0, 0))
    outs = [2 * NSA_WIDTH, KV_COLS, 2 * LANES, LANES] + [HGRN_WIDTH] * 5
    return pl.pallas_call(
        _proj_sample_kernel,
        grid=(1,),
        in_specs=[row(d), const((1, d)), const(w.shape), const(qg.shape), const(kg.shape), const(seg.shape), const(hp.shape)],
        out_specs=[row(n) for n in outs],
        out_shape=[jax.ShapeDtypeStruct((t, n), F32) for n in outs],
        compiler_params=_params(("arbitrary",)),
        name="proj_sample",
    )(x, gain, w, qg, kg, seg, hp)


def _compress_rows(k_ref, v_ref, pe_ref, w_ref, out_ref, m, pitch=CMP_STRIDE):
    pairs = CMP_STRIDE // 2
    for kind, ref in ((0, k_ref), (1, v_ref)):
        acc = None
        bias = None
        for j in range(pairs):
            w = w_ref[(kind * pairs + j) * 2 * LANES:(kind * pairs + j + 1) * 2 * LANES, :]
            xa = ref[pl.ds(2 * j, m, stride=pitch), :]
            xb = ref[pl.ds(2 * j + 1, m, stride=pitch), :]
            d = _dot(jnp.concatenate([xa, xb], axis=1).astype(BF16), w)
            acc = d if acc is None else acc + d
            pe0 = pe_ref[(kind * 2) * pairs + j:(kind * 2) * pairs + j + 1, :]
            pe1 = pe_ref[(kind * 2 + 1) * pairs + j:(kind * 2 + 1) * pairs + j + 1, :]
            pel = jnp.concatenate([jnp.broadcast_to(pe0, (8, 2 * LANES)), jnp.broadcast_to(pe1, (8, 2 * LANES))], axis=0)
            pb = _dot(pel.astype(BF16), w)
            bias = pb if bias is None else bias + pb
        out_ref[:, kind * 2 * LANES:kind * 2 * LANES + LANES] = acc[:, 0:LANES] + bias[0:1, 0:LANES]
        out_ref[:, kind * 2 * LANES + LANES:(kind + 1) * 2 * LANES] = acc[:, LANES:2 * LANES] + bias[8:9, LANES:2 * LANES]


def _compress_prompt_kernel(k_ref, v_ref, pe_ref, w_ref, out_ref, *, m):
    _compress_rows(k_ref, v_ref, pe_ref, w_ref, out_ref, m)


def _compress_prompt(kc, vc, pe, w, *, batch, seq):
    m = seq // CMP_STRIDE
    const = lambda shape: pl.BlockSpec(shape, lambda i: (0,) * len(shape))
    return pl.pallas_call(
        functools.partial(_compress_prompt_kernel, m=m),
        grid=(batch,),
        in_specs=[pl.BlockSpec((seq, LANES), lambda i: (i, 0)), pl.BlockSpec((seq, LANES), lambda i: (i, 0)), const(pe.shape), const(w.shape)],
        out_specs=pl.BlockSpec((m, 4 * LANES), lambda i: (i, 0)),
        out_shape=jax.ShapeDtypeStruct((batch * m, 4 * LANES), F32),
        compiler_params=_params(("arbitrary",)),
        name="compress_prompt",
    )(kc, vc, pe, w)


def _page_copy(pt_ref, cache_ref, buf_ref, sem_ref, b, p, slot, *, page_base, kind0):
    page = pt_ref[b, p] + page_base
    return pltpu.make_async_copy(
        cache_ref.at[pl.ds(page * KV_COLS + kind0 * LANES, 2 * LANES), :],
        buf_ref.at[slot, pl.ds(p * 2 * LANES, 2 * LANES), :],
        sem_ref.at[slot])


def _page_pipeline(copy, b, nb, n_pages):
    slot = b % 2

    def start_all(s, sl):
        def body(p, carry):
            copy(s, p, sl).start()
            return carry
        lax.fori_loop(0, n_pages, body, 0)

    @pl.when(b == 0)
    def _():
        start_all(b, slot)

    @pl.when(b + 1 < nb)
    def _():
        start_all(b + 1, 1 - slot)

    def wait_all():
        def body(p, carry):
            copy(b, p, slot).wait()
            return carry
        lax.fori_loop(0, n_pages, body, 0)
    return wait_all


def _compress_sample_kernel(pt_ref, cache_ref, pe_ref, w_ref, out_ref, buf_ref, sem_ref, ktok_ref, vtok_ref, *, n_pages, page_base):
    b = pl.program_id(0)
    slot = b % 2
    copy = functools.partial(_page_copy, pt_ref, cache_ref, buf_ref, sem_ref, page_base=page_base, kind0=0)
    wait_all = _page_pipeline(copy, b, pl.num_programs(0), n_pages)
    wait_all()

    chunks = PAGE_SIZE // CMP_STRIDE

    def tr_page(p):
        r0 = pl.multiple_of(p * 2 * LANES, 2 * LANES)
        t0 = pl.multiple_of(p * chunks * CMP_PITCH, 8)
        kt = buf_ref[slot, pl.ds(r0, LANES), :].T
        vt = buf_ref[slot, pl.ds(r0 + LANES, LANES), :].T
        for c in range(chunks):
            ktok_ref[pl.ds(t0 + c * CMP_PITCH, CMP_STRIDE), :] = kt[c * CMP_STRIDE:(c + 1) * CMP_STRIDE]
            vtok_ref[pl.ds(t0 + c * CMP_PITCH, CMP_STRIDE), :] = vt[c * CMP_STRIDE:(c + 1) * CMP_STRIDE]

    def tr_body(pp, carry):
        for i in range(TR_PAGES):
            tr_page(pp * TR_PAGES + i)
        return carry
    lax.fori_loop(0, n_pages // TR_PAGES, tr_body, 0)
    _compress_rows(ktok_ref, vtok_ref, pe_ref, w_ref, out_ref, n_pages * chunks, pitch=CMP_PITCH)


def _compress_sample(page_table, cache_t, pe, w, *, page_base):
    db, n_pages = page_table.shape
    past = n_pages * PAGE_SIZE
    m = past // CMP_STRIDE
    const = lambda shape: pl.BlockSpec(shape, lambda i, pt: (0,) * len(shape))
    kern = functools.partial(_compress_sample_kernel, n_pages=n_pages, page_base=page_base)
    return pl.pallas_call(
        kern,
        grid_spec=pltpu.PrefetchScalarGridSpec(
            num_scalar_prefetch=1,
            grid=(db,),
            in_specs=[pl.BlockSpec(memory_space=pl.ANY), const(pe.shape), const(w.shape)],
            out_specs=pl.BlockSpec((m, 4 * LANES), lambda i, pt: (i, 0)),
            scratch_shapes=[pltpu.VMEM((2, n_pages * 2 * LANES, LANES), F32), pltpu.SemaphoreType.DMA((2,)),
                            pltpu.VMEM((m * CMP_PITCH, LANES), F32), pltpu.VMEM((m * CMP_PITCH, LANES), F32)],
        ),
        out_shape=jax.ShapeDtypeStruct((db * m, 4 * LANES), F32),
        compiler_params=_params(("arbitrary",)),
        name="compress_sample",
    )(page_table, cache_t, pe, w)


def _compressed_kv(p_all, kg, seg1):
    nc = p_all.shape[0]
    up = lambda a: pltpu.roll(a, nc - 1, 0)
    kc_raw = p_all[:, 0:LANES] + up(p_all[:, LANES:2 * LANES])
    vc = p_all[:, 2 * LANES:3 * LANES] + up(p_all[:, 3 * LANES:4 * LANES])
    return _seg_rms(kc_raw, kg, seg1), vc


def _head_slope(h):
    return 2.0 ** -(h + 1)


def _nsa_prompt_kernel(qt_ref, k_ref, vt_ref, p_ref, gt_ref, kg_ref, seg_ref, tt_ref, o_ref,
                       sc_ref, sel_ref, m_ref, l_ref, acc_ref, *, tq, nkt, n_blk):
    q0 = pl.program_id(1) * tq
    q_last = q0 + tq - 1
    hpg = NSA_HEADS // NSA_GROUPS
    lane_blk = lambda a, h: a[:, h * tq:(h + 1) * tq]

    row = lax.broadcasted_iota(jnp.int32, (LANES, tq), 0)
    cols = [None] * NSA_HEADS
    for c in range(4):
        blk = qt_ref[LANES * c:LANES * (c + 1), :]
        cols[c] = jnp.where(row < HEAD_DIM, blk, jnp.zeros_like(blk))
        cols[4 + c] = jnp.where(row >= HEAD_DIM, blk, jnp.zeros_like(blk))
    qpt = jnp.concatenate(cols, axis=1)

    qpos = q0 + lax.broadcasted_iota(jnp.int32, (1, tq), 1)
    qpos_f = qpos.astype(F32)

    kc, vc = _compressed_kv(p_ref[...], kg_ref[...], seg_ref[...])
    nc = kc.shape[0]
    s_c = _dot(kc.astype(BF16), qpt)
    cstart = lax.broadcasted_iota(jnp.int32, (nc, 1), 0) * CMP_STRIDE
    cdist = jnp.abs(qpos_f - (cstart.astype(F32) + 0.5 * (CMP_LEN - 1)))
    c_ok = (cstart + (CMP_LEN - 1)) <= qpos
    any_ok = (qpos >= CMP_LEN - 1).astype(F32)
    ps = []
    for h in range(NSA_HEADS):
        s = jnp.where(c_ok, lane_blk(s_c, h) - _head_slope(h) * cdist, NEG_INF)
        e = jnp.exp(s - jnp.max(s, axis=0, keepdims=True))
        ps.append(e * (any_ok / jnp.sum(e, axis=0, keepdims=True)))
    o_cmp = _dot(vc.T.astype(BF16), jnp.concatenate(ps, axis=1).astype(BF16))

    imps = []
    for g in range(NSA_GROUPS):
        acc = ps[g * hpg]
        for h in range(1, hpg):
            acc = acc + ps[g * hpg + h]
        imps.append(acc)
    tt = tt_ref[...]
    blk_t = None
    for part in _split3(jnp.concatenate(imps, axis=1)):
        d = _dot(tt, part)
        blk_t = d if blk_t is None else blk_t + d
    nbp = tt.shape[0]
    j_t = lax.broadcasted_iota(jnp.int32, (nbp, 2 * tq), 0)
    qcol = lax.broadcasted_iota(jnp.int32, (nbp, 2 * tq), 1) & (tq - 1)
    back = ((q0 + qcol) >> _log2(SEL_LEN)) - j_t
    forced = (j_t == 0) | ((back >= 0) & (back < N_LOCAL))
    sc_ref[...] = jnp.where(back >= 0, blk_t + jnp.where(forced, FORCE_BONUS, 0.0), -1.0)

    assert tq % (2 * SEL_LEN) == 0 and n_blk % 2 == 0

    def rank_body(ip, cnt):
        sc = sc_ref[...]
        for i in (2 * ip, 2 * ip + 1):
            r = sc_ref[pl.ds(i, 1), :]
            cnt = cnt + jnp.where(j_t > i, (r >= sc).astype(F32), (r > sc).astype(F32))
        return cnt
    n_causal = jnp.minimum(q_last // SEL_LEN + 1, n_blk)
    cnt = lax.fori_loop(0, n_causal // 2, rank_body, jnp.zeros((nbp, 2 * tq), F32))
    sel_ref[...] = jnp.where(cnt < N_SEL, 0.0, NEG_INF)

    def flash_init():
        m_ref[...] = jnp.full(m_ref.shape, NEG_INF, F32)
        l_ref[...] = jnp.zeros(l_ref.shape, F32)
        acc_ref[...] = jnp.zeros(acc_ref.shape, F32)

    def flash_tile(s_t, dist, mb, vt):
        m_old = m_ref[...]
        l_old = l_ref[...]
        pbs, ms, ls, als = [], [], [], []
        for h in range(NSA_HEADS):
            t = lane_blk(s_t, h) - _head_slope(h) * dist + mb[h // hpg]
            m_h = jnp.maximum(lane_blk(m_old, h), jnp.max(t, axis=0, keepdims=True))
            al = jnp.exp(lane_blk(m_old, h) - m_h)
            p = jnp.exp(t - m_h)
            ls.append(al * lane_blk(l_old, h) + jnp.sum(p, axis=0, keepdims=True))
            pbs.append(p.astype(BF16))
            ms.append(m_h)
            als.append(al)
        acc_ref[...] = acc_ref[...] * jnp.concatenate(als, axis=1) + _dot(vt, jnp.concatenate(pbs, axis=1))
        m_ref[...] = jnp.concatenate(ms, axis=1)
        l_ref[...] = jnp.concatenate(ls, axis=1)

    flash_init()
    d0 = (lax.broadcasted_iota(jnp.int32, (nkt, tq), 1) - lax.broadcasted_iota(jnp.int32, (nkt, tq), 0)).astype(F32)

    def sel_tile(kt):
        k0 = pl.multiple_of(kt * nkt, nkt)
        dist = d0 + jnp.asarray(q0 - k0, F32)
        causal = jnp.where(dist < 0.0, NEG_INF, 0.0)
        j0 = k0 // SEL_LEN
        rows = [jnp.broadcast_to(sel_ref[pl.ds(j0 + b, 1), :], (SEL_LEN, 2 * tq)) for b in range(nkt // SEL_LEN)]
        mrow = jnp.concatenate(rows, axis=0)
        mb = [mrow[:, 0:tq] + causal, mrow[:, tq:2 * tq] + causal]
        s_t = _dot(k_ref[pl.ds(k0, nkt), 0:LANES], qpt)
        flash_tile(s_t, dist, mb, vt_ref[0:LANES, pl.ds(k0, nkt)])

    def sel_pair(kp, carry):
        sel_tile(2 * kp)
        sel_tile(2 * kp + 1)
        return carry
    n_kt = q_last // nkt + 1
    lax.fori_loop(0, n_kt // 2, sel_pair, 0)

    @pl.when(n_kt % 2 == 1)
    def _():
        sel_tile(n_kt - 1)
    o_sel = acc_ref[...] * (1.0 / l_ref[...])

    flash_init()
    def win_tile(off, nk):
        ks = pl.multiple_of(q0 - off, tq)
        dist = (lax.broadcasted_iota(jnp.int32, (nk, tq), 1) - lax.broadcasted_iota(jnp.int32, (nk, tq), 0)).astype(F32) + float(off)
        wmask = jnp.where((dist >= 0.0) & (dist < float(WINDOW)), 0.0, NEG_INF)
        s_t = _dot(k_ref[pl.ds(ks, nk), LANES:2 * LANES], qpt)
        flash_tile(s_t, dist, [wmask, wmask], vt_ref[LANES:2 * LANES, pl.ds(ks, nk)])

    @pl.when(q0 >= WINDOW)
    def _():
        for off in range(WINDOW, 0, -2 * tq):
            win_tile(off, 2 * tq)
        win_tile(0, tq)

    @pl.when(q0 < WINDOW)
    def _():
        for off in range(WINDOW, -1, -tq):
            @pl.when(q0 >= off)
            def _():
                win_tile(off, tq)
    o_win = acc_ref[...] * (1.0 / l_ref[...])

    gt = gt_ref[...]
    for c in range(4):
        halves = []
        for h, lo in ((c, 0), (4 + c, HEAD_DIM)):
            g = lambda br: gt[br * NSA_HEADS + h:br * NSA_HEADS + h + 1, :]
            pick = lambda a: a[lo:lo + HEAD_DIM, h * tq:(h + 1) * tq]
            halves.append(g(0) * pick(o_cmp) + g(1) * pick(o_sel) + g(2) * pick(o_win))
        o_ref[:, LANES * c:LANES * (c + 1)] = jnp.concatenate(halves, axis=0).T.astype(o_ref.dtype)


def _nsa_prompt(qt, kb, vt, p, gt, kg, seg1, tt, *, batch, seq, tq, nkt):
    nq = seq // tq
    nc = seq // CMP_STRIDE
    n_blk = -(-seq // SEL_LEN)
    r = NSA_HEADS * tq
    nbp = tt.shape[0]
    const = lambda shape: pl.BlockSpec(shape, lambda b, i: (0,) * len(shape))
    kern = functools.partial(_nsa_prompt_kernel, tq=tq, nkt=nkt, n_blk=n_blk)
    return pl.pallas_call(
        kern,
        grid=(batch, nq),
        in_specs=[
            pl.BlockSpec((None, NSA_WIDTH, tq), lambda b, i: (b, 0, i)),
            pl.BlockSpec((seq, 2 * LANES), lambda b, i: (b, 0)),
            pl.BlockSpec((None, 2 * LANES, seq), lambda b, i: (b, 0, 0)),
            pl.BlockSpec((nc, 4 * LANES), lambda b, i: (b, 0)),
            pl.BlockSpec((None, LANES, tq), lambda b, i: (b, 0, i)),
            const(kg.shape), const(seg1.shape), const(tt.shape),
        ],
        out_specs=pl.BlockSpec((tq, NSA_WIDTH), lambda b, i: (b * nq + i, 0)),
        out_shape=jax.ShapeDtypeStruct((batch * seq, NSA_WIDTH), BF16),
        scratch_shapes=[pltpu.VMEM((nbp, 2 * tq), F32), pltpu.VMEM((nbp, 2 * tq), F32),
                        pltpu.VMEM((1, r), F32), pltpu.VMEM((1, r), F32), pltpu.VMEM((LANES, r), F32)],
        compiler_params=_params(("arbitrary", "arbitrary")),
        name="nsa_prompt",
    )(qt, kb, vt, p, gt, kg, seg1, tt)


def _row_meta(tq, q0):
    r = NSA_HEADS * tq
    rid = lax.broadcasted_iota(jnp.int32, (r, 1), 0)
    hh = rid >> _log2(tq)
    ii = rid & (tq - 1)
    slope = lax.bitcast_convert_type((126 - hh) << 23, F32)
    qpos = q0 + ii
    return ii, slope, qpos


def _flash_init(m_ref, l_ref, acc_ref):
    m_ref[...] = jnp.full(m_ref.shape, NEG_INF, F32)
    l_ref[...] = jnp.zeros(l_ref.shape, F32)
    acc_ref[...] = jnp.zeros(acc_ref.shape, F32)


def _flash_update(m_ref, l_ref, acc_ref, s, pv):
    m_old = m_ref[...]
    m_new = jnp.maximum(m_old, jnp.max(s, axis=-1, keepdims=True))
    alpha = jnp.exp(m_old - m_new)
    p = jnp.exp(s - m_new)
    l_ref[...] = alpha * l_ref[...] + jnp.sum(p, axis=-1, keepdims=True)
    acc_ref[...] = alpha * acc_ref[...] + pv(p.astype(BF16))
    m_ref[...] = m_new


def _expand_mask(mask2b, blk0, nkt, tq):
    nbp = mask2b.shape[1]
    j_e = lax.broadcasted_iota(jnp.int32, (nbp, nkt), 0)
    c_e = lax.broadcasted_iota(jnp.int32, (nbp, nkt), 1)
    e = (j_e == blk0 + (c_e >> _log2(SEL_LEN))).astype(BF16)
    me2 = _dot(mask2b, e)
    hpg = NSA_HEADS // NSA_GROUPS
    return jnp.concatenate([me2[0:tq]] * hpg + [me2[tq:2 * tq]] * hpg, axis=0)


def _nsa_sample_kernel(pt_ref, q_ref, p_ref, rows_ref, wnew_ref, wst_ref, gates_ref, kg_ref, seg_ref, tmat_ref, cache_ref,
                       o_ref, buf_ref, sem_ref, m_ref, l_ref, acc_ref, *, tq, pages_per_tile, n_pages, page_base):
    b = pl.program_id(0)
    slot = b % 2
    past = n_pages * PAGE_SIZE
    wb = wst_ref.shape[1]
    hpg = NSA_HEADS // NSA_GROUPS
    copy = functools.partial(_page_copy, pt_ref, cache_ref, buf_ref, sem_ref, page_base=page_base, kind0=2)
    wait_all = _page_pipeline(copy, b, pl.num_programs(0), n_pages)

    r = NSA_HEADS * tq
    qp = jnp.concatenate([q_ref[:, LANES * h:LANES * (h + 1)] for h in range(NSA_HEADS)], axis=0).astype(BF16)
    ii, slope, qpos = _row_meta(tq, past)

    kc, vc = _compressed_kv(p_ref[...], kg_ref[...], seg_ref[...])
    nc = kc.shape[0]
    s = _dot_nt(qp, kc.astype(BF16))
    cstart = lax.broadcasted_iota(jnp.int32, (1, nc), 1) * CMP_STRIDE
    cdist = jnp.abs(qpos.astype(F32) - (cstart.astype(F32) + 0.5 * (CMP_LEN - 1)))
    c_ok = (cstart + (CMP_LEN - 1)) <= qpos
    s = jnp.where(c_ok, s - slope * cdist, NEG_INF)
    e = jnp.exp(s - jnp.max(s, axis=-1, keepdims=True))
    p_cmp = e / jnp.sum(e, axis=-1, keepdims=True) * (qpos >= CMP_LEN - 1).astype(F32)
    o_cmp = _dot(p_cmp.astype(BF16), vc.astype(BF16))

    imps = []
    for g in range(NSA_GROUPS):
        acc = p_cmp[(g * hpg) * tq:(g * hpg + 1) * tq]
        for h in range(1, hpg):
            acc = acc + p_cmp[(g * hpg + h) * tq:(g * hpg + h + 1) * tq]
        imps.append(acc)
    tmat = tmat_ref[...]
    blk = None
    for part in _split3(jnp.concatenate(imps, axis=0)):
        d = _dot(part, tmat)
        blk = d if blk is None else blk + d
    nbp = tmat.shape[1]
    n_blk = -(-(past + tq) // SEL_LEN)
    j_l = lax.broadcasted_iota(jnp.int32, (2 * tq, nbp), 1)
    qrow = lax.broadcasted_iota(jnp.int32, (2 * tq, nbp), 0) & (tq - 1)
    back = ((past + qrow) >> _log2(SEL_LEN)) - j_l
    forced = (j_l == 0) | ((back >= 0) & (back < N_LOCAL))
    score = jnp.where(back >= 0, blk + jnp.where(forced, FORCE_BONUS, 0.0), -1.0)
    cnt = jnp.zeros((2 * tq, nbp), F32)
    for i in range(n_blk):
        col = score[:, i:i + 1]
        cnt = cnt + jnp.where(j_l > i, (col >= score).astype(F32), (col > score).astype(F32))
    mask2 = (cnt < N_SEL).astype(F32)
    mask2b = mask2.astype(BF16)

    wait_all()

    _flash_init(m_ref, l_ref, acc_ref)
    nkt = pages_per_tile * PAGE_SIZE
    dist0 = (past + ii - lax.broadcasted_iota(jnp.int32, (r, nkt), 1)).astype(F32)

    def sel_body(kt, carry):
        k0 = kt * nkt
        tiles = []
        for i in range(pages_per_tile):
            r0 = pl.multiple_of((kt * pages_per_tile + i) * 2 * LANES, 2 * LANES)
            tiles.append((buf_ref[slot, pl.ds(r0, LANES), :].astype(BF16), buf_ref[slot, pl.ds(r0 + LANES, LANES), :].astype(BF16)))
        s = jnp.concatenate([_dot(qp, kt_i) for kt_i, _ in tiles], axis=1)
        me = _expand_mask(mask2b, k0 // SEL_LEN, nkt, tq)
        s = jnp.where(me > 0.5, s - slope * (dist0 - jnp.asarray(k0, F32)), NEG_INF)

        def pv(pb):
            out = None
            for i, (_, vt_i) in enumerate(tiles):
                d = _dot_nt(pb[:, PAGE_SIZE * i:PAGE_SIZE * (i + 1)], vt_i)
                out = d if out is None else out + d
            return out
        _flash_update(m_ref, l_ref, acc_ref, s, pv)
        return carry
    lax.fori_loop(0, n_pages // pages_per_tile, sel_body, 0)

    zpad = jnp.zeros((LANES - tq, LANES), F32)
    dist_new = (ii - lax.broadcasted_iota(jnp.int32, (r, LANES), 1)).astype(F32)
    new_blk = past // SEL_LEN
    me_new = jnp.concatenate([mask2[0:tq, new_blk:new_blk + 1]] * hpg + [mask2[tq:2 * tq, new_blk:new_blk + 1]] * hpg, axis=0)
    k_new = jnp.concatenate([rows_ref[:, 2 * LANES:3 * LANES], zpad], axis=0).astype(BF16)
    v_new = jnp.concatenate([rows_ref[:, 3 * LANES:4 * LANES], zpad], axis=0).astype(BF16)
    ok = (me_new > 0.5) & (dist_new >= 0.0)
    s = jnp.where(ok, _dot_nt(qp, k_new) - slope * dist_new, NEG_INF)
    _flash_update(m_ref, l_ref, acc_ref, s, lambda pb: _dot(pb, v_new))
    o_sel = acc_ref[...] / l_ref[...]

    _flash_init(m_ref, l_ref, acc_ref)
    dist_w = (ii + wb - lax.broadcasted_iota(jnp.int32, (r, wb), 1)).astype(F32)
    ok = (dist_w >= 0.0) & (dist_w < float(WINDOW))
    s = jnp.where(ok, _dot(qp, wst_ref[0:LANES, :].astype(BF16)) - slope * dist_w, NEG_INF)
    vwt = wst_ref[LANES:2 * LANES, :].astype(BF16)
    _flash_update(m_ref, l_ref, acc_ref, s, lambda pb: _dot_nt(pb, vwt))
    wnew = wnew_ref[...]
    kw_new = jnp.concatenate([wnew[:, 0:LANES], zpad], axis=0).astype(BF16)
    vw_new = jnp.concatenate([wnew[:, LANES:2 * LANES], zpad], axis=0).astype(BF16)
    s = jnp.where(dist_new >= 0.0, _dot_nt(qp, kw_new) - slope * dist_new, NEG_INF)
    _flash_update(m_ref, l_ref, acc_ref, s, lambda pb: _dot(pb, vw_new))
    o_win = acc_ref[...] / l_ref[...]

    gates = gates_ref[...]
    lane = lax.broadcasted_iota(jnp.int32, (tq, LANES), 1)
    outs = []
    for h in range(NSA_HEADS):
        rs = slice(h * tq, (h + 1) * tq)
        outs.append(gates[:, h:h + 1] * o_cmp[rs] + gates[:, NSA_HEADS + h:NSA_HEADS + h + 1] * o_sel[rs]
                    + gates[:, 2 * NSA_HEADS + h:2 * NSA_HEADS + h + 1] * o_win[rs])
    for c in range(4):
        o_ref[:, LANES * c:LANES * (c + 1)] = jnp.where(lane < HEAD_DIM, outs[c], outs[4 + c])


def _nsa_sample(page_table, q, p, rows, wnew, wstate_t, gates, kg, seg1, tmat, cache_t, *, tq, pages_per_tile, page_base, wstate_base):
    db, n_pages = page_table.shape
    past = n_pages * PAGE_SIZE
    nc = past // CMP_STRIDE
    wb = wstate_t.shape[1]
    r = NSA_HEADS * tq
    const = lambda shape: pl.BlockSpec(shape, lambda b, pt: (0,) * len(shape))
    kern = functools.partial(_nsa_sample_kernel, tq=tq, pages_per_tile=pages_per_tile, n_pages=n_pages, page_base=page_base)
    return pl.pallas_call(
        kern,
        grid_spec=pltpu.PrefetchScalarGridSpec(
            num_scalar_prefetch=1,
            grid=(db,),
            in_specs=[
                pl.BlockSpec((tq, 2 * NSA_WIDTH), lambda b, pt: (b, 0)),
                pl.BlockSpec((nc, 4 * LANES), lambda b, pt: (b, 0)),
                pl.BlockSpec((tq, KV_COLS), lambda b, pt: (b, 0)),
                pl.BlockSpec((tq, 2 * LANES), lambda b, pt: (b, 0)),
                pl.BlockSpec((2 * LANES, wb), lambda b, pt: (wstate_base + b, 0)),
                pl.BlockSpec((tq, LANES), lambda b, pt: (b, 0)),
                const(kg.shape), const(seg1.shape), const(tmat.shape),
                pl.BlockSpec(memory_space=pl.ANY),
            ],
            out_specs=pl.BlockSpec((tq, NSA_WIDTH), lambda b, pt: (b, 0)),
            scratch_shapes=[pltpu.VMEM((2, n_pages * 2 * LANES, LANES), F32), pltpu.SemaphoreType.DMA((2,)),
                            pltpu.VMEM((r, 1), F32), pltpu.VMEM((r, 1), F32), pltpu.VMEM((r, LANES), F32)],
        ),
        out_shape=jax.ShapeDtypeStruct((db * tq, NSA_WIDTH), F32),
        compiler_params=_params(("arbitrary",)),
        name="nsa_sample",
    )(page_table, q, p, rows, wnew, wstate_t, gates, kg, seg1, tmat, cache_t)


def _hgrn_consts(tc):
    nl = int(np.log2(tc))
    t = np.arange(tc)[:, None]
    u = np.arange(tc)[None, :]
    nl_mxu = min(3, nl)
    blocks = [(u <= t)]
    masks = [np.eye(tc, dtype=bool)]
    for lv in range(nl):
        hs = 1 << lv
        mid = (t // (2 * hs)) * 2 * hs + hs
        ref = mid - 1
        if lv < nl_mxu:
            blocks.append((t >= mid) & (u > ref) & (u <= t))
            blocks.append((t < mid) & (u > t) & (u <= ref))
        masks.append((t // (2 * hs) == u // (2 * hs)) & (t % (2 * hs) >= hs) & (u % (2 * hs) < hs))
    cm = np.concatenate(blocks, axis=0).astype(np.float32)
    mk = np.concatenate(masks, axis=0).astype(np.float32)
    return jnp.asarray(cm, BF16), jnp.asarray(mk, F32), nl, nl_mxu


def _hgrn_kernel(cm_ref, mk_ref, hq_ref, hlf_ref, hk_ref, hv_ref, hg_ref, og_ref, s0_ref, o_ref, sout_ref, st_ref, *, tc, nl, nl_mxu, nb):
    t = pl.program_id(1)
    nt = pl.num_programs(1)

    @pl.when(t == 0)
    def _():
        for i in range(nb):
            for h in range(HGRN_HEADS):
                st_ref[i, h] = s0_ref[i, h].T

    cm = cm_ref[...]
    row_id = lax.broadcasted_iota(jnp.int32, (tc, LANES), 0)
    for i in range(nb):
        for h in range(HGRN_HEADS):
            sl = slice(HGRN_DK * h, HGRN_DK * (h + 1))
            hi, lo = _split2(hlf_ref[i, :, sl])
            ee = _dot(cm, jnp.concatenate([hi, lo], axis=1))
            ee = ee[:, 0:LANES] + ee[:, LANES:2 * LANES]
            b = ee[0:tc]
            q = hq_ref[i, :, sl]
            k = hk_ref[i, :, sl]
            v = hv_ref[i, :, sl].astype(BF16)
            xb = jnp.exp(b)
            xs = jnp.exp(b[tc - 1:tc, :] - b)
            a = mk_ref[0:tc, :] * _dot_nt(q.astype(BF16), k.astype(BF16))
            for lv in range(nl):
                if lv < nl_mxu:
                    xu = jnp.exp(ee[(1 + 2 * lv) * tc:(2 + 2 * lv) * tc])
                    xl = jnp.exp(ee[(2 + 2 * lv) * tc:(3 + 2 * lv) * tc])
                else:
                    hs = 1 << lv
                    bref = jnp.concatenate([jnp.broadcast_to(b[j + hs - 1:j + hs, :], (2 * hs, LANES)) for j in range(0, tc, 2 * hs)], axis=0)
                    upper = ((row_id >> lv) & 1) == 1
                    diff = b - bref
                    xu = jnp.exp(jnp.where(upper, diff, 0.0))
                    xl = jnp.exp(jnp.where(upper, 0.0, -diff))
                a = a + mk_ref[(1 + lv) * tc:(2 + lv) * tc, :] * _dot_nt((q * xu).astype(BF16), (k * xl).astype(BF16))
            st = st_ref[i, h]
            o = _dot(a.astype(BF16), v) + _dot_nt((q * xb).astype(BF16), st.astype(BF16))
            st_new = st * xb[tc - 1:tc, :] + _dot_tn(v, (k * xs).astype(BF16))
            st_ref[i, h] = st_new
            on = o * lax.rsqrt(jnp.mean(o * o, axis=-1, keepdims=True) + EPS) * og_ref[...]
            o_ref[i, :, sl] = (on * hg_ref[i, :, sl]).astype(o_ref.dtype)

    @pl.when(t == nt - 1)
    def _():
        for i in range(nb):
            for h in range(HGRN_HEADS):
                sout_ref[i, h] = st_ref[i, h].T


def _hgrn(hq, hlf, hk, hv, hg, og, s0, *, batch, seq, tc, s0_base):
    cm, mk, nl, nl_mxu = _hgrn_consts(tc)
    nt = seq // tc
    nb = 2 if batch % 2 == 0 and s0_base % 2 == 0 else 1
    const = lambda shape: pl.BlockSpec(shape, lambda b, t: (0,) * len(shape))
    row = pl.BlockSpec((nb, tc, HGRN_WIDTH), lambda b, t: (b, t, 0))
    state = lambda base: pl.BlockSpec((nb, HGRN_HEADS, HGRN_DK, HGRN_DV), lambda b, t: (base + b, 0, 0, 0))
    kern = functools.partial(_hgrn_kernel, tc=tc, nl=nl, nl_mxu=nl_mxu, nb=nb)
    seq3 = lambda a: a.reshape(batch, seq, HGRN_WIDTH)
    o, s_out = pl.pallas_call(
        kern,
        grid=(batch // nb, nt),
        in_specs=[const(cm.shape), const(mk.shape), row, row, row, row, row, const(og.shape), state(s0_base // nb)],
        out_specs=[row, state(0)],
        out_shape=[jax.ShapeDtypeStruct((batch, seq, HGRN_WIDTH), BF16),
                   jax.ShapeDtypeStruct((batch, HGRN_HEADS, HGRN_DK, HGRN_DV), F32)],
        scratch_shapes=[pltpu.VMEM((nb, HGRN_HEADS, HGRN_DV, HGRN_DK), F32)],
        compiler_params=_params(("arbitrary", "arbitrary")),
        name="hgrn",
    )(cm, mk, seq3(hq), seq3(hlf), seq3(hk), seq3(hv), seq3(hg), og, s0)
    return o.reshape(batch * seq, HGRN_WIDTH), s_out


def _outproj_kernel(*refs, moe):
    if moe:
        x_ref, on_ref, oh_ref, w_ref, g_ref, r_ref, xo_ref, h_ref, gate_ref = refs
    else:
        x_ref, on_ref, oh_ref, w_ref, g_ref, xo_ref, h_ref = refs
    xn = x_ref[...] + _dot(on_ref[...], w_ref[0:NSA_WIDTH, :]) + _dot(oh_ref[...], w_ref[NSA_WIDTH:NSA_WIDTH + HGRN_WIDTH, :])
    xo_ref[...] = xn
    h = xn * lax.rsqrt(jnp.mean(xn * xn, axis=-1, keepdims=True) + EPS) * g_ref[...]
    h_ref[...] = h.astype(BF16)
    if moe:
        logits = None
        rparts = _split3(r_ref[...])
        hparts = _split3(h)
        for i in range(3):
            for j in range(3 - i):
                d = _dot(hparts[i], rparts[j])
                logits = d if logits is None else logits + d
        lane = lax.broadcasted_iota(jnp.int32, logits.shape, 1).astype(F32)
        lg = jnp.where(lane < N_EXPERTS, logits, NEG_INF)
        m1 = jnp.max(lg, axis=-1, keepdims=True)
        i1 = jnp.min(jnp.where(lg == m1, lane, float(LANES)), axis=-1, keepdims=True)
        lg2 = jnp.where(lane == i1, NEG_INF, lg)
        m2 = jnp.max(lg2, axis=-1, keepdims=True)
        i2 = jnp.min(jnp.where(lg2 == m2, lane, float(LANES)), axis=-1, keepdims=True)
        e2 = jnp.exp(m2 - m1)
        den = 1.0 + e2
        gate_ref[...] = (jnp.where(lane == 0.0, i1, 0.0) + jnp.where(lane == 1.0, i2, 0.0)
                         + jnp.where(lane == 2.0, 1.0 / den, 0.0) + jnp.where(lane == 3.0, e2 / den, 0.0))


def _outproj(x, o_nsa, o_hg, w, g, router, tm):
    t, d = x.shape
    moe = router is not None
    const = lambda shape: pl.BlockSpec(shape, lambda i: (0,) * len(shape))
    row = lambda n: pl.BlockSpec((tm, n), lambda i: (i, 0))
    in_specs = [row(d), row(NSA_WIDTH), row(HGRN_WIDTH), const(w.shape), const((1, d))]
    args = [x, o_nsa, o_hg, w, g]
    out_specs = [row(d), row(d)]
    out_shape = [jax.ShapeDtypeStruct((t, d), F32), jax.ShapeDtypeStruct((t, d), BF16)]
    if moe:
        in_specs.append(const(router.shape))
        args.append(router)
        out_specs.append(row(LANES))
        out_shape.append(jax.ShapeDtypeStruct((t, LANES), F32))
    return pl.pallas_call(
        functools.partial(_outproj_kernel, moe=moe),
        grid=(t // tm,),
        in_specs=in_specs, out_specs=out_specs, out_shape=out_shape,
        compiler_params=_params(("arbitrary",)),
        name="outproj_moe" if moe else "outproj",
    )(*args)


def _ffn_kernel(x_ref, h_ref, wg_ref, wu_ref, wd_ref, o_ref):
    f = pl.program_id(1)
    h = h_ref[...]
    a = _dot(h, wg_ref[...])
    b = _dot(h, wu_ref[...])
    d = _dot((a * _sigmoid(a) * b).astype(BF16), wd_ref[...])

    @pl.when(f == 0)
    def _():
        o_ref[...] = x_ref[...] + d

    @pl.when(f > 0)
    def _():
        o_ref[...] = o_ref[...] + d


def _ffn(x, h, wg, wu, wd, tm, tf):
    t, d = x.shape
    ff = wg.shape[1]
    return pl.pallas_call(
        _ffn_kernel,
        grid=(t // tm, ff // tf),
        in_specs=[pl.BlockSpec((tm, d), lambda i, f: (i, 0)), pl.BlockSpec((tm, d), lambda i, f: (i, 0)),
                  pl.BlockSpec((d, tf), lambda i, f: (0, f)), pl.BlockSpec((d, tf), lambda i, f: (0, f)),
                  pl.BlockSpec((tf, d), lambda i, f: (f, 0))],
        out_specs=pl.BlockSpec((tm, d), lambda i, f: (i, 0)),
        out_shape=jax.ShapeDtypeStruct((t, d), F32),
        compiler_params=_params(("arbitrary", "arbitrary")),
        name="ffn",
    )(x, h, wg, wu, wd)


def _route_kernel(rt_ref, pos_ref, post_ref, cnt_ref, carry_ref, *, ts):
    s = pl.program_id(0)

    @pl.when(s == 0)
    def _():
        carry_ref[...] = jnp.zeros(carry_ref.shape, F32)

    rt = rt_ref[...]
    lane = lax.broadcasted_iota(jnp.int32, (ts, LANES), 1).astype(F32)
    c = jnp.where((lane == rt[:, 0:1]) | (lane == rt[:, 1:2]), 1.0, 0.0)
    lt = (lax.broadcasted_iota(jnp.int32, (ts, ts), 1) < lax.broadcasted_iota(jnp.int32, (ts, ts), 0)).astype(BF16)
    rank = _dot(lt, c.astype(BF16)) + carry_ref[...]
    pos = jnp.where(c > 0.0, rank, -1.0)
    pos_ref[...] = pos
    post_ref[...] = pos.T[0:8, :]
    n = jnp.sum(c, axis=0, keepdims=True)
    cnt_ref[...] = jnp.broadcast_to(n, cnt_ref.shape)
    carry_ref[...] = carry_ref[...] + n


def _route(rt, ts):
    t = rt.shape[0]
    ns = t // ts
    return pl.pallas_call(
        functools.partial(_route_kernel, ts=ts),
        grid=(ns,),
        in_specs=[pl.BlockSpec((ts, LANES), lambda s: (s, 0))],
        out_specs=[pl.BlockSpec((ts, LANES), lambda s: (s, 0)), pl.BlockSpec((8, ts), lambda s: (0, s)),
                   pl.BlockSpec((None, 8, LANES), lambda s: (s, 0, 0))],
        out_shape=[jax.ShapeDtypeStruct((t, LANES), F32), jax.ShapeDtypeStruct((8, t), F32),
                   jax.ShapeDtypeStruct((ns, 8, LANES), F32)],
        scratch_shapes=[pltpu.VMEM((1, LANES), F32)],
        compiler_params=_params(("arbitrary",)),
        name="moe_route",
    )(rt)


def _moe_gather_kernel(texp_ref, trank_ref, slo_ref, nsrc_ref, post_ref, h_ref, o_ref, hbuf_ref, sem_ref, acc_ref, *, tm, ts):
    j = pl.program_id(0)
    e = texp_ref[j]
    r0 = trank_ref[j]
    s0 = slo_ref[j]
    n = nsrc_ref[j]
    acc_ref[...] = jnp.zeros(acc_ref.shape, F32)
    row = lax.broadcasted_iota(jnp.int32, (tm, 1), 0).astype(F32) + r0.astype(F32)

    def copy(s, slot):
        return pltpu.make_async_copy(h_ref.at[pl.ds(pl.multiple_of(s * ts, ts), ts), :], hbuf_ref.at[slot], sem_ref.at[slot])

    @pl.when(n > 0)
    def _():
        copy(s0, 0).start()

    def body(k, carry):
        slot = k % 2

        @pl.when(k + 1 < n)
        def _():
            copy(s0 + k + 1, 1 - slot).start()
        copy(s0 + k, slot).wait()
        prow = post_ref[pl.ds(e, 1), pl.ds(pl.multiple_of((s0 + k) * ts, ts), ts)]
        onehot = jnp.where(prow == row, 1.0, 0.0).astype(BF16)
        acc_ref[...] = acc_ref[...] + _dot(onehot, hbuf_ref[slot])
        return carry
    lax.fori_loop(0, n, body, 0)
    o_ref[...] = acc_ref[...].astype(o_ref.dtype)


def _moe_gather(tables, post, h, *, n_tiles, tm, ts):
    t, d = h.shape
    return pl.pallas_call(
        functools.partial(_moe_gather_kernel, tm=tm, ts=ts),
        grid_spec=pltpu.PrefetchScalarGridSpec(
            num_scalar_prefetch=4,
            grid=(n_tiles,),
            in_specs=[pl.BlockSpec(post.shape, lambda j, *_: (0, 0)), pl.BlockSpec(memory_space=pl.ANY)],
            out_specs=pl.BlockSpec((tm, d), lambda j, *_: (j, 0)),
            scratch_shapes=[pltpu.VMEM((2, ts, d), BF16), pltpu.SemaphoreType.DMA((2,)), pltpu.VMEM((tm, d), F32)],
        ),
        out_shape=jax.ShapeDtypeStruct((n_tiles * tm, d), BF16),
        compiler_params=_params(("arbitrary",)),
        name="moe_gather",
    )(*tables, post, h)


def _moe_ffn_kernel(texp_ref, tused_ref, h_ref, wg_ref, wu_ref, wd_ref, o_ref, acc_ref):
    f = pl.program_id(1)
    last = pl.num_programs(1) - 1
    used = tused_ref[pl.program_id(0)] > 0

    @pl.when(used)
    def _():
        h = h_ref[...]
        a = _dot(h, wg_ref[...])
        b = _dot(h, wu_ref[...])
        d = _dot((a * _sigmoid(a) * b).astype(BF16), wd_ref[...])

        @pl.when(f == 0)
        def _():
            acc_ref[...] = d

        @pl.when(f > 0)
        def _():
            acc_ref[...] = acc_ref[...] + d

        @pl.when(f == last)
        def _():
            o_ref[...] = acc_ref[...].astype(o_ref.dtype)

    @pl.when(jnp.logical_not(used) & (f == last))
    def _():
        o_ref[...] = jnp.zeros(o_ref.shape, o_ref.dtype)


def _moe_ffn(texp, tused, hs, wg, wu, wd, *, tm, tf):
    rows, d = hs.shape
    ff = wg.shape[2]
    return pl.pallas_call(
        _moe_ffn_kernel,
        grid_spec=pltpu.PrefetchScalarGridSpec(
            num_scalar_prefetch=2,
            grid=(rows // tm, ff // tf),
            in_specs=[pl.BlockSpec((tm, d), lambda j, f, te, tu: (j, 0)),
                      pl.BlockSpec((None, d, tf), lambda j, f, te, tu: (te[j], 0, f)),
                      pl.BlockSpec((None, d, tf), lambda j, f, te, tu: (te[j], 0, f)),
                      pl.BlockSpec((None, tf, d), lambda j, f, te, tu: (te[j], f, 0))],
            out_specs=pl.BlockSpec((tm, d), lambda j, f, te, tu: (j, 0)),
            scratch_shapes=[pltpu.VMEM((tm, d), F32)],
        ),
        out_shape=jax.ShapeDtypeStruct((rows, d), BF16),
        compiler_params=_params(("arbitrary", "arbitrary")),
        name="moe_ffn",
    )(texp, tused, hs, wg, wu, wd)


def _moe_combine_kernel(start_ref, delta_ref, x_ref, rt_ref, pos_ref, ys_ref, o_ref, ybuf_ref, sem_ref, *, ts, win):
    s = pl.program_id(0)
    ns = pl.num_programs(0)
    slot = s % 2

    def copy(step, e, sl):
        st = pl.multiple_of(start_ref[step * N_EXPERTS + e], LANES)
        return pltpu.make_async_copy(ys_ref.at[pl.ds(st, win), :], ybuf_ref.at[sl, e], sem_ref.at[sl])

    def start_all(step, sl):
        for e in range(N_EXPERTS):
            copy(step, e, sl).start()

    @pl.when(s == 0)
    def _():
        start_all(s, slot)

    @pl.when(s + 1 < ns)
    def _():
        start_all(s + 1, 1 - slot)

    rt = rt_ref[...]
    pos = pos_ref[...]
    col = lax.broadcasted_iota(jnp.int32, (1, win), 1).astype(F32)
    for e in range(N_EXPERTS):
        copy(s, e, slot).wait()
    y = x_ref[...]
    for e in range(N_EXPERTS):
        rel = pos[:, e:e + 1] + delta_ref[s * N_EXPERTS + e].astype(F32)
        g = jnp.where(rel == col, 1.0, 0.0).astype(BF16)
        w = jnp.where(rt[:, 0:1] == float(e), rt[:, 2:3], 0.0) + jnp.where(rt[:, 1:2] == float(e), rt[:, 3:4], 0.0)
        y = y + w * _dot(g, ybuf_ref[slot, e])
    o_ref[...] = y


def _moe_combine(start, delta, x, rt, pos, ys, *, ts, win):
    t, d = x.shape
    row = lambda n: pl.BlockSpec((ts, n), lambda s, *_: (s, 0))
    return pl.pallas_call(
        functools.partial(_moe_combine_kernel, ts=ts, win=win),
        grid_spec=pltpu.PrefetchScalarGridSpec(
            num_scalar_prefetch=2,
            grid=(t // ts,),
            in_specs=[row(d), row(LANES), row(LANES), pl.BlockSpec(memory_space=pl.ANY)],
            out_specs=row(d),
            scratch_shapes=[pltpu.VMEM((2, N_EXPERTS, win, d), BF16), pltpu.SemaphoreType.DMA((2,))],
        ),
        out_shape=jax.ShapeDtypeStruct((t, d), F32),
        compiler_params=_params(("arbitrary",)),
        name="moe_combine",
    )(start, delta, x, rt, pos, ys)


def _moe(x, h, rt, wg, wu, wd, *, tm, ts, tf):
    t, d = x.shape
    win = ts + LANES
    tg = tm // 2
    n_tiles = -(-2 * t // tm) + N_EXPERTS + 1
    pos, post, cnt = _route(rt, ts)
    counts = cnt[:, 0, :N_EXPERTS].astype(jnp.int32)
    cum = jnp.concatenate([jnp.zeros((1, N_EXPERTS), jnp.int32), jnp.cumsum(counts, axis=0)], axis=0)
    tiles_e = -(-cum[-1] // tm)
    tstart = jnp.concatenate([jnp.zeros((1,), jnp.int32), jnp.cumsum(tiles_e)])
    jt = jnp.arange(n_tiles, dtype=jnp.int32)
    texp = jnp.minimum(jnp.sum(jt[:, None] >= tstart[None, 1:], axis=1), N_EXPERTS - 1).astype(jnp.int32)
    used = jt < tstart[-1]
    sub = tm // tg
    spread = lambda a: jnp.broadcast_to(a[:, None], (n_tiles, sub)).reshape(-1)
    texp_g, used_g = spread(texp), spread(used)
    trank = ((jt - tstart[texp]) * tm)[:, None] + (jnp.arange(sub, dtype=jnp.int32) * tg)[None, :]
    trank = jnp.where(used_g, trank.reshape(-1), -2 * tm).astype(jnp.int32)
    lo = cum[:-1].T[texp_g]
    hi = cum[1:].T[texp_g]
    hit = used_g[:, None] & (lo < trank[:, None] + tg) & (hi > trank[:, None])
    slo = jnp.sum(used_g[:, None] & (hi <= trank[:, None]), axis=1).astype(jnp.int32)
    nsrc = jnp.sum(hit, axis=1).astype(jnp.int32)
    row0 = tstart[:-1][None, :] * tm + cum[:-1]
    start = (row0 // LANES) * LANES
    delta = tstart[:-1][None, :] * tm - start
    hs = _moe_gather((texp_g, trank, slo, nsrc), post, h, n_tiles=n_tiles * sub, tm=tg, ts=ts)
    ys = _moe_ffn(texp, used.astype(jnp.int32), hs, wg, wu, wd, tm=tm, tf=tf)
    return _moe_combine(start.reshape(-1).astype(jnp.int32), delta.reshape(-1).astype(jnp.int32), x, rt, pos, ys, ts=ts, win=win)


def _head_perm():
    idx = []
    for c in range(4):
        idx += list(range(HEAD_DIM * c, HEAD_DIM * (c + 1))) + list(range(HEAD_DIM * (4 + c), HEAD_DIM * (5 + c)))
    return np.asarray(idx, np.int32)


def _tap_matrix(n_cmp, nc_pad, n_blk, nb_pad):
    r_s = SEL_LEN // CMP_STRIDE
    r_c = CMP_LEN // CMP_STRIDE
    taps = np.convolve(np.ones(r_s), np.ones(r_c)) / r_c
    tm = np.zeros((nc_pad, nb_pad), np.float32)
    for j in range(n_blk):
        for kk, w in enumerate(taps):
            n = j * r_s + kk - (r_c - 1)
            if 0 <= n < n_cmp:
                tm[n, j] = w
    return tm


def _layer_weights(l, w_in, q_gain, k_gain, cmp_pe, cmp_w, w_out):
    perm = _head_perm()
    wl = w_in[l]
    q_end = NSA_WIDTH
    kv_end = q_end + 6 * NSA_GROUPS * HEAD_DIM
    gate_end = kv_end + 3 * NSA_HEADS
    d = wl.shape[0]
    w_pad = jnp.concatenate([wl[:, :q_end][:, perm], wl[:, q_end:kv_end], wl[:, kv_end:gate_end],
                             jnp.zeros((d, LANES - 3 * NSA_HEADS), wl.dtype), wl[:, gate_end:]], axis=1).astype(BF16)
    qg = jnp.tile(q_gain[l], NSA_HEADS)[None, :]
    kg_proj = jnp.stack([jnp.tile(k_gain[l, 1], NSA_GROUPS), jnp.tile(k_gain[l, 2], NSA_GROUPS)])
    kg_cmp = jnp.tile(k_gain[l, 0], NSA_GROUPS)[None, :]
    pairs = CMP_STRIDE // 2
    cw = cmp_w[l].reshape(2, 2, pairs, 2, HEAD_DIM, HEAD_DIM)
    eye = jnp.eye(NSA_GROUPS, dtype=cw.dtype)
    w_cmp = jnp.einsum('krjsde,gh->kjsgdrhe', cw, eye).reshape(2 * pairs * 2 * LANES, 2 * LANES).astype(BF16)
    pe = cmp_pe[l].reshape(2, 2, pairs, 2, 1, HEAD_DIM)
    pe = jnp.broadcast_to(pe, (2, 2, pairs, 2, NSA_GROUPS, HEAD_DIM)).reshape(4 * pairs, 2 * LANES)
    wo = w_out[l]
    wo = jnp.concatenate([wo[:NSA_WIDTH][perm], wo[NSA_WIDTH:]], axis=0).astype(BF16)
    return w_pad, qg, kg_proj, kg_cmp, w_cmp, pe, wo


def _token_minor(a):
    n = a.ndim
    return jnp.transpose(a, tuple(range(n - 4)) + (n - 3, n - 2, n - 1, n - 4))


def _token_major(a):
    n = a.ndim
    return jnp.transpose(a, tuple(range(n - 4)) + (n - 1, n - 4, n - 3, n - 2))


def kernel(x_prompt, x_sample, cache_kv, state_win_kv, state_hgrn, page_table, norm_mix, norm_ffn, w_in, q_gain, k_gain, cmp_pe, cmp_w, hgrn_lb_logits, hgrn_o_gain, w_out, ffn_w_gate, ffn_w_up, ffn_w_down, moe_router, moe_w_gate, moe_w_up, moe_w_down):
    depth = w_in.shape[0]
    batch, seq, d_model = x_prompt.shape
    db, ds, _ = x_sample.shape
    n_pool = cache_kv.shape[1]
    n_pages = page_table.shape[1]
    past = n_pages * PAGE_SIZE
    wb = state_win_kv.shape[2]
    assert wb == WINDOW and seq % 256 == 0 and seq >= WINDOW and ds == 8 and n_pages % 16 == 0

    tq, nkt_p, tc = 128, 512, 128
    tm_p = 256
    tm_f = 512 if (batch * seq) % 512 == 0 else 256
    tm_e = 512
    ts = db * ds
    d_ff = ffn_w_gate.shape[-1]
    tf = d_ff // 2

    sm = jax.nn.softmax(hgrn_lb_logits.astype(F32), axis=0)
    lower = jnp.concatenate([jnp.zeros_like(sm[:1]), jnp.cumsum(sm[1:], axis=0)], axis=0)
    seg = jnp.asarray(np.kron(np.eye(NSA_HEADS), np.ones((HEAD_DIM, HEAD_DIM))), BF16)
    seg1 = seg[:LANES, :LANES]

    nc_p = seq // CMP_STRIDE
    nb_p = -(-seq // SEL_LEN)
    tt_p = jnp.asarray(_tap_matrix(nc_p - 1, nc_p, nb_p, -(-nb_p // 8) * 8).T, BF16)
    nc_s = past // CMP_STRIDE
    nb_s = -(-(past + ds) // SEL_LEN)
    nbp_s = -(-nb_s // LANES) * LANES
    tm_s = jnp.asarray(_tap_matrix(nc_s - 1, nc_s, nb_s, nbp_s), BF16)

    cache_t = _token_minor(cache_kv).reshape(depth * n_pool * KV_COLS, PAGE_SIZE)
    wstate_t = _token_minor(state_win_kv).reshape(depth * db * 2 * LANES, wb)
    hstate = state_hgrn.reshape(depth * db, HGRN_HEADS, HGRN_DK, HGRN_DV)
    zero_state = jnp.zeros((batch, HGRN_HEADS, HGRN_DK, HGRN_DV), F32)

    xp = x_prompt.reshape(batch * seq, d_model)
    xs = x_sample.reshape(ts, d_model)
    kv_p, kv_s, win_p, win_s, hs_p, hs_s = [], [], [], [], [], []
    for l in range(depth):
        w_pad, qg, kg_proj, kg_cmp, w_cmp, pe, wo = _layer_weights(l, w_in, q_gain, k_gain, cmp_pe, cmp_w, w_out)
        lb = lower[l]
        hp = jnp.stack([jnp.log(lb), jnp.log1p(-lb), 1.0 - lb])
        og = hgrn_o_gain[l][None, :]
        g1 = norm_mix[l][None, :]
        g2 = norm_ffn[l][None, :]
        i = l // 2
        if l % 2 == 0:
            router = None
            wg, wu, wd = ffn_w_gate[i].astype(BF16), ffn_w_up[i].astype(BF16), ffn_w_down[i].astype(BF16)
        else:
            router = jnp.pad(moe_router[i], ((0, 0), (0, LANES - N_EXPERTS)))
            wg, wu, wd = moe_w_gate[i].astype(BF16), moe_w_up[i].astype(BF16), moe_w_down[i].astype(BF16)

        def mixer(x, h, rt):
            tm = tm_f if x.shape[0] % tm_f == 0 else x.shape[0]
            if router is None:
                return _ffn(x, h, wg, wu, wd, tm, tf)
            return _moe(x, h, rt, wg, wu, wd, tm=tm_e, ts=min(tm_p, x.shape[0]), tf=tf)

        qt, rows_t, win_t, kb, vt, kc, vc, gt, hq, hlf, hk, hv, hg = _proj_prompt(xp, g1, w_pad, qg, kg_proj, seg, hp, batch=batch, seq=seq, tm=tm_p)
        pp = _compress_prompt(kc, vc, pe, w_cmp, batch=batch, seq=seq)
        o_nsa = _nsa_prompt(qt, kb, vt, pp, gt, kg_cmp, seg1, tt_p, batch=batch, seq=seq, tq=tq, nkt=nkt_p)
        o_hg, s_fin = _hgrn(hq, hlf, hk, hv, hg, og, zero_state, batch=batch, seq=seq, tc=tc, s0_base=0)
        res = _outproj(xp, o_nsa, o_hg, wo, g2, router, tm_p)
        xp = mixer(res[0], res[1], res[2] if router is not None else None)
        kv_p.append(rows_t.reshape(batch, 4, NSA_GROUPS, HEAD_DIM, seq))
        win_p.append(win_t.reshape(batch, 2, NSA_GROUPS, HEAD_DIM, seq)[..., seq - min(WINDOW, seq):])
        hs_p.append(s_fin)

        q, rows, win, gates, hq, hlf, hk, hv, hg = _proj_sample(xs, g1, w_pad, qg, kg_proj, seg, hp)
        ps = _compress_sample(page_table, cache_t, pe, w_cmp, page_base=l * n_pool)
        o_nsa = _nsa_sample(page_table, q, ps, rows, win, wstate_t, gates, kg_cmp, seg1, tm_s, cache_t,
                            tq=ds, pages_per_tile=16, page_base=l * n_pool, wstate_base=l * db)
        padt = lambda a: jnp.pad(a.reshape(db, ds, HGRN_WIDTH), ((0, 0), (0, tc - ds), (0, 0))).reshape(db * tc, HGRN_WIDTH)
        o_hg, s_new = _hgrn(padt(hq), padt(hlf), padt(hk), padt(hv), padt(hg), og, hstate, batch=db, seq=tc, tc=tc, s0_base=l * db)
        o_hg = o_hg.reshape(db, tc, HGRN_WIDTH)[:, :ds].reshape(ts, HGRN_WIDTH)
        res = _outproj(xs, o_nsa.astype(BF16), o_hg, wo, g2, router, ts)
        xs = mixer(res[0], res[1], res[2] if router is not None else None)
        kv_s.append(rows.reshape(db, ds, 4, NSA_GROUPS, HEAD_DIM))
        win_s.append(win.reshape(db, ds, 2, NSA_GROUPS, HEAD_DIM))
        hs_s.append(s_new)

    new_win_sample = jnp.concatenate([state_win_kv[:, :, ds:], jnp.stack(win_s)], axis=2)
    return (xp.reshape(batch, seq, d_model), xs.reshape(db, ds, d_model), _token_major(jnp.stack(kv_p)), jnp.stack(kv_s),
            _token_major(jnp.stack(win_p)), new_win_sample, jnp.stack(hs_p), jnp.stack(hs_s))
```

```python
import functools

import numpy as np
import jax
import jax.numpy as jnp
from jax import lax
from jax.experimental import pallas as pl
from jax.experimental.pallas import tpu as pltpu

F32 = jnp.float32
BF16 = jnp.bfloat16

NSA_HEADS = 8
NSA_GROUPS = 2
HEAD_DIM = 64
NSA_WIDTH = NSA_HEADS * HEAD_DIM
CMP_LEN = 32
CMP_STRIDE = 16
SEL_LEN = 64
N_SEL = 16
N_LOCAL = 2
WINDOW = 512
FORCE_BONUS = 1e4
HGRN_HEADS = 4
HGRN_DK = 128
HGRN_DV = 128
HGRN_WIDTH = HGRN_HEADS * HGRN_DV
N_EXPERTS = 8
EPS = 1e-6
NEG_INF = -1e30
PAGE_SIZE = 128

LANES = 128
KV_COLS = 4 * NSA_GROUPS * HEAD_DIM
CMP_PITCH = CMP_STRIDE + 1
TR_PAGES = 4
GATHER_BUFS = 4
VMEM_LIMIT = 48 * 1024 * 1024


def _log2(n):
    assert n & (n - 1) == 0
    return n.bit_length() - 1


def _dot(a, b):
    return jnp.dot(a, b, preferred_element_type=F32)


def _dot_nt(a, b):
    return lax.dot_general(a, b, (((1,), (1,)), ((), ())), preferred_element_type=F32)


def _dot_tn(a, b):
    return lax.dot_general(a, b, (((0,), (0,)), ((), ())), preferred_element_type=F32)


def _split2(x):
    hi = x.astype(BF16)
    lo = (x - hi.astype(F32)).astype(BF16)
    return hi, lo


def _split3(x):
    hi = x.astype(BF16)
    r = x - hi.astype(F32)
    mid = r.astype(BF16)
    lo = (r - mid.astype(F32)).astype(BF16)
    return hi, mid, lo


def _segsum(x, seg):
    hi, lo = _split2(x)
    return _dot(hi, seg) + _dot(lo, seg)


def _seg_rms(z, gain, seg):
    ss = _segsum(z * z, seg) * (1.0 / HEAD_DIM)
    return z * lax.rsqrt(ss + EPS) * gain


def _sigmoid(x):
    return 1.0 / (1.0 + jnp.exp(-x))


def _params(sem, vmem=VMEM_LIMIT):
    return pltpu.CompilerParams(dimension_semantics=sem, vmem_limit_bytes=vmem)


def _proj_core(x_ref, gain_ref, w_ref, qg_ref, kg_ref, seg_ref, hp_ref, hq_ref, hlf_ref, hk_ref, hv_ref, hg_ref):
    x = x_ref[...]
    ms = jnp.mean(x * x, axis=-1, keepdims=True)
    h = (x * lax.rsqrt(ms + EPS) * gain_ref[...]).astype(BF16)

    def mm(c0, c1):
        return _dot(h, w_ref[:, c0:c1])

    seg1 = seg_ref[0:LANES, 0:LANES]
    qn = _seg_rms(mm(0, NSA_WIDTH), qg_ref[...], seg_ref[...]) * (HEAD_DIM ** -0.5)
    c0 = NSA_WIDTH
    kv = mm(c0, c0 + 6 * LANES)
    kvs = [kv[:, LANES * j:LANES * (j + 1)] for j in range(6)]
    kvs[2] = _seg_rms(kvs[2], kg_ref[0:1, :], seg1)
    kvs[4] = _seg_rms(kvs[4], kg_ref[1:2, :], seg1)
    c0 += 6 * LANES
    gates = _sigmoid(mm(c0, c0 + LANES))
    c0 += LANES
    zq = mm(c0, c0 + HGRN_WIDTH)
    hq_ref[...] = zq * _sigmoid(zq)
    c0 += HGRN_WIDTH
    zf = mm(c0, c0 + HGRN_WIDTH)
    log_sig = jnp.minimum(zf, 0.0) - jnp.log1p(jnp.exp(-jnp.abs(zf)))
    a = hp_ref[0:1, :]
    c = hp_ref[1:2, :] + log_sig
    hlf_ref[...] = jnp.maximum(a, c) + jnp.log1p(jnp.exp(-jnp.abs(a - c)))
    hk_ref[...] = hp_ref[2:3, :] * _sigmoid(-zf)
    c0 += HGRN_WIDTH
    hv_ref[...] = mm(c0, c0 + HGRN_WIDTH)
    c0 += HGRN_WIDTH
    zg = mm(c0, c0 + HGRN_WIDTH)
    hg_ref[...] = zg * _sigmoid(zg)
    return qn, kvs, gates


def _proj_prompt_kernel(x_ref, gain_ref, w_ref, qg_ref, kg_ref, seg_ref, hp_ref,
                        qt_ref, rowst_ref, wint_ref, kb_ref, vt_ref, kc_ref, vc_ref, gt_ref,
                        hq_ref, hlf_ref, hk_ref, hv_ref, hg_ref):
    qn, kvs, gates = _proj_core(x_ref, gain_ref, w_ref, qg_ref, kg_ref, seg_ref, hp_ref, hq_ref, hlf_ref, hk_ref, hv_ref, hg_ref)
    for c in range(4):
        qt_ref[LANES * c:LANES * (c + 1), :] = qn[:, LANES * c:LANES * (c + 1)].T.astype(BF16)
    kvt = [a.T for a in kvs]
    for j in range(4):
        rowst_ref[LANES * j:LANES * (j + 1), :] = kvt[j]
    wint_ref[0:LANES, :] = kvt[4]
    wint_ref[LANES:2 * LANES, :] = kvt[5]
    kb_ref[:, 0:LANES] = kvs[2].astype(BF16)
    kb_ref[:, LANES:2 * LANES] = kvs[4].astype(BF16)
    vt_ref[0:LANES, :] = kvt[3].astype(BF16)
    vt_ref[LANES:2 * LANES, :] = kvt[5].astype(BF16)
    kc_ref[...] = kvs[0]
    vc_ref[...] = kvs[1]
    gt_ref[...] = gates.T


def _proj_sample_kernel(x_ref, gain_ref, w_ref, qg_ref, kg_ref, seg_ref, hp_ref,
                        q_ref, rows_ref, win_ref, gates_ref, hq_ref, hlf_ref, hk_ref, hv_ref, hg_ref):
    qn, kvs, gates = _proj_core(x_ref, gain_ref, w_ref, qg_ref, kg_ref, seg_ref, hp_ref, hq_ref, hlf_ref, hk_ref, hv_ref, hg_ref)
    lane = lax.broadcasted_iota(jnp.int32, (qn.shape[0], LANES), 1)
    for c in range(4):
        blk = qn[:, LANES * c:LANES * (c + 1)]
        q_ref[:, LANES * c:LANES * (c + 1)] = jnp.where(lane < HEAD_DIM, blk, 0.0)
        q_ref[:, LANES * (4 + c):LANES * (5 + c)] = jnp.where(lane >= HEAD_DIM, blk, 0.0)
    for j in range(4):
        rows_ref[:, LANES * j:LANES * (j + 1)] = kvs[j]
    win_ref[:, 0:LANES] = kvs[4]
    win_ref[:, LANES:2 * LANES] = kvs[5]
    gates_ref[...] = gates


def _proj_prompt(x, gain, w, qg, kg, seg, hp, *, batch, seq, tm):
    t, d = x.shape
    nt = seq // tm
    const = lambda shape: pl.BlockSpec(shape, lambda b, i: (0,) * len(shape))
    row = lambda n: pl.BlockSpec((tm, n), lambda b, i: (b * nt + i, 0))
    colt = lambda n: pl.BlockSpec((None, n, tm), lambda b, i: (b, 0, i))
    tshape = lambda n, dt: jax.ShapeDtypeStruct((batch, n, seq), dt)
    rshape = lambda n, dt: jax.ShapeDtypeStruct((t, n), dt)
    return pl.pallas_call(
        _proj_prompt_kernel,
        grid=(batch, nt),
        in_specs=[row(d), const((1, d)), const(w.shape), const(qg.shape), const(kg.shape), const(seg.shape), const(hp.shape)],
        out_specs=[colt(NSA_WIDTH), colt(KV_COLS), colt(2 * LANES), row(2 * LANES), colt(2 * LANES), row(LANES), row(LANES),
                   colt(LANES)] + [row(HGRN_WIDTH)] * 5,
        out_shape=[tshape(NSA_WIDTH, BF16), tshape(KV_COLS, F32), tshape(2 * LANES, F32), rshape(2 * LANES, BF16),
                   tshape(2 * LANES, BF16), rshape(LANES, F32), rshape(LANES, F32), tshape(LANES, F32)]
                  + [rshape(HGRN_WIDTH, F32)] * 5,
        compiler_params=_params(("arbitrary", "arbitrary")),
        name="proj_prompt",
    )(x, gain, w, qg, kg, seg, hp)


def _proj_sample(x, gain, w, qg, kg, seg, hp):
    t, d = x.shape
    const = lambda shape: pl.BlockSpec(shape, lambda i: (0,) * len(shape))
    row = lambda n: pl.BlockSpec((t, n), lambda i: (0, 0))
    outs = [2 * NSA_WIDTH, KV_COLS, 2 * LANES, LANES] + [HGRN_WIDTH] * 5
    return pl.pallas_call(
        _proj_sample_kernel,
        grid=(1,),
        in_specs=[row(d), const((1, d)), const(w.shape), const(qg.shape), const(kg.shape), const(seg.shape), const(hp.shape)],
        out_specs=[row(n) for n in outs],
        out_shape=[jax.ShapeDtypeStruct((t, n), F32) for n in outs],
        compiler_params=_params(("arbitrary",)),
        name="proj_sample",
    )(x, gain, w, qg, kg, seg, hp)


def _compress_rows(k_ref, v_ref, pe_ref, w_ref, out_ref, m, pitch=CMP_STRIDE):
    pairs = CMP_STRIDE // 2
    for kind, ref in ((0, k_ref), (1, v_ref)):
        acc = None
        bias = None
        for j in range(pairs):
            w = w_ref[(kind * pairs + j) * 2 * LANES:(kind * pairs + j + 1) * 2 * LANES, :]
            xa = ref[pl.ds(2 * j, m, stride=pitch), :]
            xb = ref[pl.ds(2 * j + 1, m, stride=pitch), :]
            d = _dot(jnp.concatenate([xa, xb], axis=1).astype(BF16), w)
            acc = d if acc is None else acc + d
            pe0 = pe_ref[(kind * 2) * pairs + j:(kind * 2) * pairs + j + 1, :]
            pe1 = pe_ref[(kind * 2 + 1) * pairs + j:(kind * 2 + 1) * pairs + j + 1, :]
            pel = jnp.concatenate([jnp.broadcast_to(pe0, (8, 2 * LANES)), jnp.broadcast_to(pe1, (8, 2 * LANES))], axis=0)
            pb = _dot(pel.astype(BF16), w)
            bias = pb if bias is None else bias + pb
        out_ref[:, kind * 2 * LANES:kind * 2 * LANES + LANES] = acc[:, 0:LANES] + bias[0:1, 0:LANES]
        out_ref[:, kind * 2 * LANES + LANES:(kind + 1) * 2 * LANES] = acc[:, LANES:2 * LANES] + bias[8:9, LANES:2 * LANES]


def _compress_prompt_kernel(k_ref, v_ref, pe_ref, w_ref, out_ref, *, m):
    _compress_rows(k_ref, v_ref, pe_ref, w_ref, out_ref, m)


def _compress_prompt(kc, vc, pe, w, *, batch, seq):
    m = seq // CMP_STRIDE
    const = lambda shape: pl.BlockSpec(shape, lambda i: (0,) * len(shape))
    return pl.pallas_call(
        functools.partial(_compress_prompt_kernel, m=m),
        grid=(batch,),
        in_specs=[pl.BlockSpec((seq, LANES), lambda i: (i, 0)), pl.BlockSpec((seq, LANES), lambda i: (i, 0)), const(pe.shape), const(w.shape)],
        out_specs=pl.BlockSpec((m, 4 * LANES), lambda i: (i, 0)),
        out_shape=jax.ShapeDtypeStruct((batch * m, 4 * LANES), F32),
        compiler_params=_params(("arbitrary",)),
        name="compress_prompt",
    )(kc, vc, pe, w)


def _page_copy(pt_ref, cache_ref, buf_ref, sem_ref, b, p, slot, *, page_base, kind0):
    page = pt_ref[b, p] + page_base
    return pltpu.make_async_copy(
        cache_ref.at[pl.ds(page * KV_COLS + kind0 * LANES, 2 * LANES), :],
        buf_ref.at[slot, pl.ds(p * 2 * LANES, 2 * LANES), :],
        sem_ref.at[slot])


def _page_pipeline(copy, b, nb, n_pages):
    slot = b % 2

    def start_all(s, sl):
        def body(p, carry):
            copy(s, p, sl).start()
            return carry
        lax.fori_loop(0, n_pages, body, 0)

    @pl.when(b == 0)
    def _():
        start_all(b, slot)

    @pl.when(b + 1 < nb)
    def _():
        start_all(b + 1, 1 - slot)

    def wait_all():
        def body(p, carry):
            copy(b, p, slot).wait()
            return carry
        lax.fori_loop(0, n_pages, body, 0)
    return wait_all


def _compress_sample_kernel(pt_ref, cache_ref, pe_ref, w_ref, out_ref, buf_ref, sem_ref, ktok_ref, vtok_ref, *, n_pages, page_base):
    b = pl.program_id(0)
    slot = b % 2
    copy = functools.partial(_page_copy, pt_ref, cache_ref, buf_ref, sem_ref, page_base=page_base, kind0=0)
    wait_all = _page_pipeline(copy, b, pl.num_programs(0), n_pages)
    wait_all()

    chunks = PAGE_SIZE // CMP_STRIDE

    def tr_page(p):
        r0 = pl.multiple_of(p * 2 * LANES, 2 * LANES)
        t0 = pl.multiple_of(p * chunks * CMP_PITCH, 8)
        kt = buf_ref[slot, pl.ds(r0, LANES), :].T
        vt = buf_ref[slot, pl.ds(r0 + LANES, LANES), :].T
        for c in range(chunks):
            ktok_ref[pl.ds(t0 + c * CMP_PITCH, CMP_STRIDE), :] = kt[c * CMP_STRIDE:(c + 1) * CMP_STRIDE]
            vtok_ref[pl.ds(t0 + c * CMP_PITCH, CMP_STRIDE), :] = vt[c * CMP_STRIDE:(c + 1) * CMP_STRIDE]

    def tr_body(pp, carry):
        for i in range(TR_PAGES):
            tr_page(pp * TR_PAGES + i)
        return carry
    lax.fori_loop(0, n_pages // TR_PAGES, tr_body, 0)
    _compress_rows(ktok_ref, vtok_ref, pe_ref, w_ref, out_ref, n_pages * chunks, pitch=CMP_PITCH)


def _compress_sample(page_table, cache_t, pe, w, *, page_base):
    db, n_pages = page_table.shape
    past = n_pages * PAGE_SIZE
    m = past // CMP_STRIDE
    const = lambda shape: pl.BlockSpec(shape, lambda i, pt: (0,) * len(shape))
    kern = functools.partial(_compress_sample_kernel, n_pages=n_pages, page_base=page_base)
    return pl.pallas_call(
        kern,
        grid_spec=pltpu.PrefetchScalarGridSpec(
            num_scalar_prefetch=1,
            grid=(db,),
            in_specs=[pl.BlockSpec(memory_space=pl.ANY), const(pe.shape), const(w.shape)],
            out_specs=pl.BlockSpec((m, 4 * LANES), lambda i, pt: (i, 0)),
            scratch_shapes=[pltpu.VMEM((2, n_pages * 2 * LANES, LANES), F32), pltpu.SemaphoreType.DMA((2,)),
                            pltpu.VMEM((m * CMP_PITCH, LANES), F32), pltpu.VMEM((m * CMP_PITCH, LANES), F32)],
        ),
        out_shape=jax.ShapeDtypeStruct((db * m, 4 * LANES), F32),
        compiler_params=_params(("arbitrary",)),
        name="compress_sample",
    )(page_table, cache_t, pe, w)


def _compressed_kv(p_all, kg, seg1):
    nc = p_all.shape[0]
    up = lambda a: pltpu.roll(a, nc - 1, 0)
    kc_raw = p_all[:, 0:LANES] + up(p_all[:, LANES:2 * LANES])
    vc = p_all[:, 2 * LANES:3 * LANES] + up(p_all[:, 3 * LANES:4 * LANES])
    return _seg_rms(kc_raw, kg, seg1), vc


def _head_slope(h):
    return 2.0 ** -(h + 1)


def _nsa_prompt_kernel(qt_ref, k_ref, vt_ref, p_ref, gt_ref, kg_ref, seg_ref, tt_ref, o_ref,
                       sc_ref, sel_ref, m_ref, l_ref, acc_ref, *, tq, nkt, n_blk):
    q0 = pl.program_id(1) * tq
    q_last = q0 + tq - 1
    hpg = NSA_HEADS // NSA_GROUPS
    lane_blk = lambda a, h: a[:, h * tq:(h + 1) * tq]

    row = lax.broadcasted_iota(jnp.int32, (LANES, tq), 0)
    cols = [None] * NSA_HEADS
    for c in range(4):
        blk = qt_ref[LANES * c:LANES * (c + 1), :]
        cols[c] = jnp.where(row < HEAD_DIM, blk, jnp.zeros_like(blk))
        cols[4 + c] = jnp.where(row >= HEAD_DIM, blk, jnp.zeros_like(blk))
    qpt = jnp.concatenate(cols, axis=1)

    qpos = q0 + lax.broadcasted_iota(jnp.int32, (1, tq), 1)
    qpos_f = qpos.astype(F32)

    kc, vc = _compressed_kv(p_ref[...], kg_ref[...], seg_ref[...])
    nc = kc.shape[0]
    s_c = _dot(kc.astype(BF16), qpt)
    cstart = lax.broadcasted_iota(jnp.int32, (nc, 1), 0) * CMP_STRIDE
    cdist = jnp.abs(qpos_f - (cstart.astype(F32) + 0.5 * (CMP_LEN - 1)))
    c_ok = (cstart + (CMP_LEN - 1)) <= qpos
    any_ok = (qpos >= CMP_LEN - 1).astype(F32)
    ps = []
    for h in range(NSA_HEADS):
        s = jnp.where(c_ok, lane_blk(s_c, h) - _head_slope(h) * cdist, NEG_INF)
        e = jnp.exp(s - jnp.max(s, axis=0, keepdims=True))
        ps.append(e * (any_ok / jnp.sum(e, axis=0, keepdims=True)))
    o_cmp = _dot(vc.T.astype(BF16), jnp.concatenate(ps, axis=1).astype(BF16))

    imps = []
    for g in range(NSA_GROUPS):
        acc = ps[g * hpg]
        for h in range(1, hpg):
            acc = acc + ps[g * hpg + h]
        imps.append(acc)
    tt = tt_ref[...]
    blk_t = None
    for part in _split3(jnp.concatenate(imps, axis=1)):
        d = _dot(tt, part)
        blk_t = d if blk_t is None else blk_t + d
    nbp = tt.shape[0]
    j_t = lax.broadcasted_iota(jnp.int32, (nbp, 2 * tq), 0)
    qcol = lax.broadcasted_iota(jnp.int32, (nbp, 2 * tq), 1) & (tq - 1)
    back = ((q0 + qcol) >> _log2(SEL_LEN)) - j_t
    forced = (j_t == 0) | ((back >= 0) & (back < N_LOCAL))
    sc_ref[...] = jnp.where(back >= 0, blk_t + jnp.where(forced, FORCE_BONUS, 0.0), -1.0)

    assert tq % (2 * SEL_LEN) == 0 and n_blk % 2 == 0

    def rank_body(ip, cnt):
        sc = sc_ref[...]
        for i in (2 * ip, 2 * ip + 1):
            r = sc_ref[pl.ds(i, 1), :]
            cnt = cnt + jnp.where(j_t > i, (r >= sc).astype(F32), (r > sc).astype(F32))
        return cnt
    n_causal = jnp.minimum(q_last // SEL_LEN + 1, n_blk)
    cnt = lax.fori_loop(0, n_causal // 2, rank_body, jnp.zeros((nbp, 2 * tq), F32))
    sel_ref[...] = jnp.where(cnt < N_SEL, 0.0, NEG_INF)

    def flash_init():
        m_ref[...] = jnp.full(m_ref.shape, NEG_INF, F32)
        l_ref[...] = jnp.zeros(l_ref.shape, F32)
        acc_ref[...] = jnp.zeros(acc_ref.shape, F32)

    def flash_tile(s_t, dist, mb, vt):
        m_old = m_ref[...]
        l_old = l_ref[...]
        pbs, ms, ls, als = [], [], [], []
        for h in range(NSA_HEADS):
            t = lane_blk(s_t, h) - _head_slope(h) * dist + mb[h // hpg]
            m_h = jnp.maximum(lane_blk(m_old, h), jnp.max(t, axis=0, keepdims=True))
            al = jnp.exp(lane_blk(m_old, h) - m_h)
            p = jnp.exp(t - m_h)
            ls.append(al * lane_blk(l_old, h) + jnp.sum(p, axis=0, keepdims=True))
            pbs.append(p.astype(BF16))
            ms.append(m_h)
            als.append(al)
        acc_ref[...] = acc_ref[...] * jnp.concatenate(als, axis=1) + _dot(vt, jnp.concatenate(pbs, axis=1))
        m_ref[...] = jnp.concatenate(ms, axis=1)
        l_ref[...] = jnp.concatenate(ls, axis=1)

    flash_init()
    d0 = (lax.broadcasted_iota(jnp.int32, (nkt, tq), 1) - lax.broadcasted_iota(jnp.int32, (nkt, tq), 0)).astype(F32)

    def sel_tile(kt):
        k0 = pl.multiple_of(kt * nkt, nkt)
        dist = d0 + jnp.asarray(q0 - k0, F32)
        causal = jnp.where(dist < 0.0, NEG_INF, 0.0)
        j0 = k0 // SEL_LEN
        rows = [jnp.broadcast_to(sel_ref[pl.ds(j0 + b, 1), :], (SEL_LEN, 2 * tq)) for b in range(nkt // SEL_LEN)]
        mrow = jnp.concatenate(rows, axis=0)
        mb = [mrow[:, 0:tq] + causal, mrow[:, tq:2 * tq] + causal]
        s_t = _dot(k_ref[pl.ds(k0, nkt), 0:LANES], qpt)
        flash_tile(s_t, dist, mb, vt_ref[0:LANES, pl.ds(k0, nkt)])

    def sel_pair(kp, carry):
        sel_tile(2 * kp)
        sel_tile(2 * kp + 1)
        return carry
    n_kt = q_last // nkt + 1
    lax.fori_loop(0, n_kt // 2, sel_pair, 0)

    @pl.when(n_kt % 2 == 1)
    def _():
        sel_tile(n_kt - 1)
    o_sel = acc_ref[...] * (1.0 / l_ref[...])

    flash_init()
    def win_tile(off, nk):
        ks = pl.multiple_of(q0 - off, tq)
        dist = (lax.broadcasted_iota(jnp.int32, (nk, tq), 1) - lax.broadcasted_iota(jnp.int32, (nk, tq), 0)).astype(F32) + float(off)
        wmask = jnp.where((dist >= 0.0) & (dist < float(WINDOW)), 0.0, NEG_INF)
        s_t = _dot(k_ref[pl.ds(ks, nk), LANES:2 * LANES], qpt)
        flash_tile(s_t, dist, [wmask, wmask], vt_ref[LANES:2 * LANES, pl.ds(ks, nk)])

    @pl.when(q0 >= WINDOW)
    def _():
        for off in range(WINDOW, 0, -2 * tq):
            win_tile(off, 2 * tq)
        win_tile(0, tq)

    @pl.when(q0 < WINDOW)
    def _():
        for off in range(WINDOW, -1, -tq):
            @pl.when(q0 >= off)
            def _():
                win_tile(off, tq)
    o_win = acc_ref[...] * (1.0 / l_ref[...])

    gt = gt_ref[...]
    for c in range(4):
        halves = []
        for h, lo in ((c, 0), (4 + c, HEAD_DIM)):
            g = lambda br: gt[br * NSA_HEADS + h:br * NSA_HEADS + h + 1, :]
            pick = lambda a: a[lo:lo + HEAD_DIM, h * tq:(h + 1) * tq]
            halves.append(g(0) * pick(o_cmp) + g(1) * pick(o_sel) + g(2) * pick(o_win))
        o_ref[:, LANES * c:LANES * (c + 1)] = jnp.concatenate(halves, axis=0).T.astype(o_ref.dtype)


def _nsa_prompt(qt, kb, vt, p, gt, kg, seg1, tt, *, batch, seq, tq, nkt):
    nq = seq // tq
    nc = seq // CMP_STRIDE
    n_blk = -(-seq // SEL_LEN)
    r = NSA_HEADS * tq
    nbp = tt.shape[0]
    const = lambda shape: pl.BlockSpec(shape, lambda b, i: (0,) * len(shape))
    kern = functools.partial(_nsa_prompt_kernel, tq=tq, nkt=nkt, n_blk=n_blk)
    return pl.pallas_call(
        kern,
        grid=(batch, nq),
        in_specs=[
            pl.BlockSpec((None, NSA_WIDTH, tq), lambda b, i: (b, 0, i)),
            pl.BlockSpec((seq, 2 * LANES), lambda b, i: (b, 0)),
            pl.BlockSpec((None, 2 * LANES, seq), lambda b, i: (b, 0, 0)),
            pl.BlockSpec((nc, 4 * LANES), lambda b, i: (b, 0)),
            pl.BlockSpec((None, LANES, tq), lambda b, i: (b, 0, i)),
            const(kg.shape), const(seg1.shape), const(tt.shape),
        ],
        out_specs=pl.BlockSpec((tq, NSA_WIDTH), lambda b, i: (b * nq + i, 0)),
        out_shape=jax.ShapeDtypeStruct((batch * seq, NSA_WIDTH), BF16),
        scratch_shapes=[pltpu.VMEM((nbp, 2 * tq), F32), pltpu.VMEM((nbp, 2 * tq), F32),
                        pltpu.VMEM((1, r), F32), pltpu.VMEM((1, r), F32), pltpu.VMEM((LANES, r), F32)],
        compiler_params=_params(("arbitrary", "arbitrary")),
        name="nsa_prompt",
    )(qt, kb, vt, p, gt, kg, seg1, tt)


def _row_meta(tq, q0):
    r = NSA_HEADS * tq
    rid = lax.broadcasted_iota(jnp.int32, (r, 1), 0)
    hh = rid >> _log2(tq)
    ii = rid & (tq - 1)
    slope = lax.bitcast_convert_type((126 - hh) << 23, F32)
    qpos = q0 + ii
    return ii, slope, qpos


def _flash_init(m_ref, l_ref, acc_ref):
    m_ref[...] = jnp.full(m_ref.shape, NEG_INF, F32)
    l_ref[...] = jnp.zeros(l_ref.shape, F32)
    acc_ref[...] = jnp.zeros(acc_ref.shape, F32)


def _flash_update(m_ref, l_ref, acc_ref, s, pv):
    m_old = m_ref[...]
    m_new = jnp.maximum(m_old, jnp.max(s, axis=-1, keepdims=True))
    alpha = jnp.exp(m_old - m_new)
    p = jnp.exp(s - m_new)
    l_ref[...] = alpha * l_ref[...] + jnp.sum(p, axis=-1, keepdims=True)
    acc_ref[...] = alpha * acc_ref[...] + pv(p.astype(BF16))
    m_ref[...] = m_new


def _expand_mask(mask2b, blk0, nkt, tq):
    nbp = mask2b.shape[1]
    j_e = lax.broadcasted_iota(jnp.int32, (nbp, nkt), 0)
    c_e = lax.broadcasted_iota(jnp.int32, (nbp, nkt), 1)
    e = (j_e == blk0 + (c_e >> _log2(SEL_LEN))).astype(BF16)
    me2 = _dot(mask2b, e)
    hpg = NSA_HEADS // NSA_GROUPS
    return jnp.concatenate([me2[0:tq]] * hpg + [me2[tq:2 * tq]] * hpg, axis=0)


def _nsa_sample_kernel(pt_ref, q_ref, p_ref, rows_ref, wnew_ref, wst_ref, gates_ref, kg_ref, seg_ref, tmat_ref, cache_ref,
                       o_ref, buf_ref, sem_ref, m_ref, l_ref, acc_ref, *, tq, pages_per_tile, n_pages, page_base):
    b = pl.program_id(0)
    slot = b % 2
    past = n_pages * PAGE_SIZE
    wb = wst_ref.shape[1]
    hpg = NSA_HEADS // NSA_GROUPS
    copy = functools.partial(_page_copy, pt_ref, cache_ref, buf_ref, sem_ref, page_base=page_base, kind0=2)
    wait_all = _page_pipeline(copy, b, pl.num_programs(0), n_pages)

    r = NSA_HEADS * tq
    qp = jnp.concatenate([q_ref[:, LANES * h:LANES * (h + 1)] for h in range(NSA_HEADS)], axis=0).astype(BF16)
    ii, slope, qpos = _row_meta(tq, past)

    kc, vc = _compressed_kv(p_ref[...], kg_ref[...], seg_ref[...])
    nc = kc.shape[0]
    s = _dot_nt(qp, kc.astype(BF16))
    cstart = lax.broadcasted_iota(jnp.int32, (1, nc), 1) * CMP_STRIDE
    cdist = jnp.abs(qpos.astype(F32) - (cstart.astype(F32) + 0.5 * (CMP_LEN - 1)))
    c_ok = (cstart + (CMP_LEN - 1)) <= qpos
    s = jnp.where(c_ok, s - slope * cdist, NEG_INF)
    e = jnp.exp(s - jnp.max(s, axis=-1, keepdims=True))
    p_cmp = e / jnp.sum(e, axis=-1, keepdims=True) * (qpos >= CMP_LEN - 1).astype(F32)
    o_cmp = _dot(p_cmp.astype(BF16), vc.astype(BF16))

    imps = []
    for g in range(NSA_GROUPS):
        acc = p_cmp[(g * hpg) * tq:(g * hpg + 1) * tq]
        for h in range(1, hpg):
            acc = acc + p_cmp[(g * hpg + h) * tq:(g * hpg + h + 1) * tq]
        imps.append(acc)
    tmat = tmat_ref[...]
    blk = None
    for part in _split3(jnp.concatenate(imps, axis=0)):
        d = _dot(part, tmat)
        blk = d if blk is None else blk + d
    nbp = tmat.shape[1]
    n_blk = -(-(past + tq) // SEL_LEN)
    j_l = lax.broadcasted_iota(jnp.int32, (2 * tq, nbp), 1)
    qrow = lax.broadcasted_iota(jnp.int32, (2 * tq, nbp), 0) & (tq - 1)
    back = ((past + qrow) >> _log2(SEL_LEN)) - j_l
    forced = (j_l == 0) | ((back >= 0) & (back < N_LOCAL))
    score = jnp.where(back >= 0, blk + jnp.where(forced, FORCE_BONUS, 0.0), -1.0)
    cnt = jnp.zeros((2 * tq, nbp), F32)
    for i in range(n_blk):
        col = score[:, i:i + 1]
        cnt = cnt + jnp.where(j_l > i, (col >= score).astype(F32), (col > score).astype(F32))
    mask2 = (cnt < N_SEL).astype(F32)
    mask2b = mask2.astype(BF16)

    wait_all()

    _flash_init(m_ref, l_ref, acc_ref)
    nkt = pages_per_tile * PAGE_SIZE
    dist0 = (past + ii - lax.broadcasted_iota(jnp.int32, (r, nkt), 1)).astype(F32)

    def sel_body(kt, carry):
        k0 = kt * nkt
        tiles = []
        for i in range(pages_per_tile):
            r0 = pl.multiple_of((kt * pages_per_tile + i) * 2 * LANES, 2 * LANES)
            tiles.append((buf_ref[slot, pl.ds(r0, LANES), :].astype(BF16), buf_ref[slot, pl.ds(r0 + LANES, LANES), :].astype(BF16)))
        s = jnp.concatenate([_dot(qp, kt_i) for kt_i, _ in tiles], axis=1)
        me = _expand_mask(mask2b, k0 // SEL_LEN, nkt, tq)
        s = jnp.where(me > 0.5, s - slope * (dist0 - jnp.asarray(k0, F32)), NEG_INF)

        def pv(pb):
            out = None
            for i, (_, vt_i) in enumerate(tiles):
                d = _dot_nt(pb[:, PAGE_SIZE * i:PAGE_SIZE * (i + 1)], vt_i)
                out = d if out is None else out + d
            return out
        _flash_update(m_ref, l_ref, acc_ref, s, pv)
        return carry
    lax.fori_loop(0, n_pages // pages_per_tile, sel_body, 0)

    zpad = jnp.zeros((LANES - tq, LANES), F32)
    dist_new = (ii - lax.broadcasted_iota(jnp.int32, (r, LANES), 1)).astype(F32)
    new_blk = past // SEL_LEN
    me_new = jnp.concatenate([mask2[0:tq, new_blk:new_blk + 1]] * hpg + [mask2[tq:2 * tq, new_blk:new_blk + 1]] * hpg, axis=0)
    k_new = jnp.concatenate([rows_ref[:, 2 * LANES:3 * LANES], zpad], axis=0).astype(BF16)
    v_new = jnp.concatenate([rows_ref[:, 3 * LANES:4 * LANES], zpad], axis=0).astype(BF16)
    ok = (me_new > 0.5) & (dist_new >= 0.0)
    s = jnp.where(ok, _dot_nt(qp, k_new) - slope * dist_new, NEG_INF)
    _flash_update(m_ref, l_ref, acc_ref, s, lambda pb: _dot(pb, v_new))
    o_sel = acc_ref[...] / l_ref[...]

    _flash_init(m_ref, l_ref, acc_ref)
    dist_w = (ii + wb - lax.broadcasted_iota(jnp.int32, (r, wb), 1)).astype(F32)
    ok = (dist_w >= 0.0) & (dist_w < float(WINDOW))
    s = jnp.where(ok, _dot(qp, wst_ref[0:LANES, :].astype(BF16)) - slope * dist_w, NEG_INF)
    vwt = wst_ref[LANES:2 * LANES, :].astype(BF16)
    _flash_update(m_ref, l_ref, acc_ref, s, lambda pb: _dot_nt(pb, vwt))
    wnew = wnew_ref[...]
    kw_new = jnp.concatenate([wnew[:, 0:LANES], zpad], axis=0).astype(BF16)
    vw_new = jnp.concatenate([wnew[:, LANES:2 * LANES], zpad], axis=0).astype(BF16)
    s = jnp.where(dist_new >= 0.0, _dot_nt(qp, kw_new) - slope * dist_new, NEG_INF)
    _flash_update(m_ref, l_ref, acc_ref, s, lambda pb: _dot(pb, vw_new))
    o_win = acc_ref[...] / l_ref[...]

    gates = gates_ref[...]
    lane = lax.broadcasted_iota(jnp.int32, (tq, LANES), 1)
    outs = []
    for h in range(NSA_HEADS):
        rs = slice(h * tq, (h + 1) * tq)
        outs.append(gates[:, h:h + 1] * o_cmp[rs] + gates[:, NSA_HEADS + h:NSA_HEADS + h + 1] * o_sel[rs]
                    + gates[:, 2 * NSA_HEADS + h:2 * NSA_HEADS + h + 1] * o_win[rs])
    for c in range(4):
        o_ref[:, LANES * c:LANES * (c + 1)] = jnp.where(lane < HEAD_DIM, outs[c], outs[4 + c])


def _nsa_sample(page_table, q, p, rows, wnew, wstate_t, gates, kg, seg1, tmat, cache_t, *, tq, pages_per_tile, page_base, wstate_base):
    db, n_pages = page_table.shape
    past = n_pages * PAGE_SIZE
    nc = past // CMP_STRIDE
    wb = wstate_t.shape[1]
    r = NSA_HEADS * tq
    const = lambda shape: pl.BlockSpec(shape, lambda b, pt: (0,) * len(shape))
    kern = functools.partial(_nsa_sample_kernel, tq=tq, pages_per_tile=pages_per_tile, n_pages=n_pages, page_base=page_base)
    return pl.pallas_call(
        kern,
        grid_spec=pltpu.PrefetchScalarGridSpec(
            num_scalar_prefetch=1,
            grid=(db,),
            in_specs=[
                pl.BlockSpec((tq, 2 * NSA_WIDTH), lambda b, pt: (b, 0)),
                pl.BlockSpec((nc, 4 * LANES), lambda b, pt: (b, 0)),
                pl.BlockSpec((tq, KV_COLS), lambda b, pt: (b, 0)),
                pl.BlockSpec((tq, 2 * LANES), lambda b, pt: (b, 0)),
                pl.BlockSpec((2 * LANES, wb), lambda b, pt: (wstate_base + b, 0)),
                pl.BlockSpec((tq, LANES), lambda b, pt: (b, 0)),
                const(kg.shape), const(seg1.shape), const(tmat.shape),
                pl.BlockSpec(memory_space=pl.ANY),
            ],
            out_specs=pl.BlockSpec((tq, NSA_WIDTH), lambda b, pt: (b, 0)),
            scratch_shapes=[pltpu.VMEM((2, n_pages * 2 * LANES, LANES), F32), pltpu.SemaphoreType.DMA((2,)),
                            pltpu.VMEM((r, 1), F32), pltpu.VMEM((r, 1), F32), pltpu.VMEM((r, LANES), F32)],
        ),
        out_shape=jax.ShapeDtypeStruct((db * tq, NSA_WIDTH), F32),
        compiler_params=_params(("arbitrary",)),
        name="nsa_sample",
    )(page_table, q, p, rows, wnew, wstate_t, gates, kg, seg1, tmat, cache_t)


def _hgrn_consts(tc):
    nl = int(np.log2(tc))
    t = np.arange(tc)[:, None]
    u = np.arange(tc)[None, :]
    nl_mxu = min(3, nl)
    blocks = [(u <= t)]
    masks = [np.eye(tc, dtype=bool)]
    for lv in range(nl):
        hs = 1 << lv
        mid = (t // (2 * hs)) * 2 * hs + hs
        ref = mid - 1
        if lv < nl_mxu:
            blocks.append((t >= mid) & (u > ref) & (u <= t))
            blocks.append((t < mid) & (u > t) & (u <= ref))
        masks.append((t // (2 * hs) == u // (2 * hs)) & (t % (2 * hs) >= hs) & (u % (2 * hs) < hs))
    cm = np.concatenate(blocks, axis=0).astype(np.float32)
    mk = np.concatenate(masks, axis=0).astype(np.float32)
    return jnp.asarray(cm, BF16), jnp.asarray(mk, F32), nl, nl_mxu


def _hgrn_kernel(cm_ref, mk_ref, hq_ref, hlf_ref, hk_ref, hv_ref, hg_ref, og_ref, s0_ref, o_ref, sout_ref, st_ref, *, tc, nl, nl_mxu, nb):
    t = pl.program_id(1)
    nt = pl.num_programs(1)

    @pl.when(t == 0)
    def _():
        for i in range(nb):
            for h in range(HGRN_HEADS):
                st_ref[i, h] = s0_ref[i, h].T

    cm = cm_ref[...]
    row_id = lax.broadcasted_iota(jnp.int32, (tc, LANES), 0)
    for i in range(nb):
        for h in range(HGRN_HEADS):
            sl = slice(HGRN_DK * h, HGRN_DK * (h + 1))
            hi, lo = _split2(hlf_ref[i, :, sl])
            ee = _dot(cm, jnp.concatenate([hi, lo], axis=1))
            ee = ee[:, 0:LANES] + ee[:, LANES:2 * LANES]
            b = ee[0:tc]
            q = hq_ref[i, :, sl]
            k = hk_ref[i, :, sl]
            v = hv_ref[i, :, sl].astype(BF16)
            xb = jnp.exp(b)
            xs = jnp.exp(b[tc - 1:tc, :] - b)
            a = mk_ref[0:tc, :] * _dot_nt(q.astype(BF16), k.astype(BF16))
            for lv in range(nl):
                if lv < nl_mxu:
                    xu = jnp.exp(ee[(1 + 2 * lv) * tc:(2 + 2 * lv) * tc])
                    xl = jnp.exp(ee[(2 + 2 * lv) * tc:(3 + 2 * lv) * tc])
                else:
                    hs = 1 << lv
                    bref = jnp.concatenate([jnp.broadcast_to(b[j + hs - 1:j + hs, :], (2 * hs, LANES)) for j in range(0, tc, 2 * hs)], axis=0)
                    upper = ((row_id >> lv) & 1) == 1
                    diff = b - bref
                    xu = jnp.exp(jnp.where(upper, diff, 0.0))
                    xl = jnp.exp(jnp.where(upper, 0.0, -diff))
                a = a + mk_ref[(1 + lv) * tc:(2 + lv) * tc, :] * _dot_nt((q * xu).astype(BF16), (k * xl).astype(BF16))
            st = st_ref[i, h]
            o = _dot(a.astype(BF16), v) + _dot_nt((q * xb).astype(BF16), st.astype(BF16))
            st_new = st * xb[tc - 1:tc, :] + _dot_tn(v, (k * xs).astype(BF16))
            st_ref[i, h] = st_new
            on = o * lax.rsqrt(jnp.mean(o * o, axis=-1, keepdims=True) + EPS) * og_ref[...]
            o_ref[i, :, sl] = (on * hg_ref[i, :, sl]).astype(o_ref.dtype)

    @pl.when(t == nt - 1)
    def _():
        for i in range(nb):
            for h in range(HGRN_HEADS):
                sout_ref[i, h] = st_ref[i, h].T


def _hgrn(hq, hlf, hk, hv, hg, og, s0, *, batch, seq, tc, s0_base):
    cm, mk, nl, nl_mxu = _hgrn_consts(tc)
    nt = seq // tc
    nb = 2 if batch % 2 == 0 and s0_base % 2 == 0 else 1
    const = lambda shape: pl.BlockSpec(shape, lambda b, t: (0,) * len(shape))
    row = pl.BlockSpec((nb, tc, HGRN_WIDTH), lambda b, t: (b, t, 0))
    state = lambda base: pl.BlockSpec((nb, HGRN_HEADS, HGRN_DK, HGRN_DV), lambda b, t: (base + b, 0, 0, 0))
    kern = functools.partial(_hgrn_kernel, tc=tc, nl=nl, nl_mxu=nl_mxu, nb=nb)
    seq3 = lambda a: a.reshape(batch, seq, HGRN_WIDTH)
    o, s_out = pl.pallas_call(
        kern,
        grid=(batch // nb, nt),
        in_specs=[const(cm.shape), const(mk.shape), row, row, row, row, row, const(og.shape), state(s0_base // nb)],
        out_specs=[row, state(0)],
        out_shape=[jax.ShapeDtypeStruct((batch, seq, HGRN_WIDTH), BF16),
                   jax.ShapeDtypeStruct((batch, HGRN_HEADS, HGRN_DK, HGRN_DV), F32)],
        scratch_shapes=[pltpu.VMEM((nb, HGRN_HEADS, HGRN_DV, HGRN_DK), F32)],
        compiler_params=_params(("arbitrary", "arbitrary")),
        name="hgrn",
    )(cm, mk, seq3(hq), seq3(hlf), seq3(hk), seq3(hv), seq3(hg), og, s0)
    return o.reshape(batch * seq, HGRN_WIDTH), s_out


def _outproj_kernel(*refs, moe):
    if moe:
        x_ref, on_ref, oh_ref, w_ref, g_ref, r_ref, xo_ref, h_ref, gate_ref = refs
    else:
        x_ref, on_ref, oh_ref, w_ref, g_ref, xo_ref, h_ref = refs
    xn = x_ref[...] + _dot(on_ref[...], w_ref[0:NSA_WIDTH, :]) + _dot(oh_ref[...], w_ref[NSA_WIDTH:NSA_WIDTH + HGRN_WIDTH, :])
    xo_ref[...] = xn
    h = xn * lax.rsqrt(jnp.mean(xn * xn, axis=-1, keepdims=True) + EPS) * g_ref[...]
    h_ref[...] = h.astype(BF16)
    if moe:
        logits = None
        rparts = _split3(r_ref[...])
        hparts = _split3(h)
        for i in range(3):
            for j in range(3 - i):
                d = _dot(hparts[i], rparts[j])
                logits = d if logits is None else logits + d
        lane = lax.broadcasted_iota(jnp.int32, logits.shape, 1).astype(F32)
        lg = jnp.where(lane < N_EXPERTS, logits, NEG_INF)
        m1 = jnp.max(lg, axis=-1, keepdims=True)
        i1 = jnp.min(jnp.where(lg == m1, lane, float(LANES)), axis=-1, keepdims=True)
        lg2 = jnp.where(lane == i1, NEG_INF, lg)
        m2 = jnp.max(lg2, axis=-1, keepdims=True)
        i2 = jnp.min(jnp.where(lg2 == m2, lane, float(LANES)), axis=-1, keepdims=True)
        e2 = jnp.exp(m2 - m1)
        den = 1.0 + e2
        gate_ref[...] = (jnp.where(lane == 0.0, i1, 0.0) + jnp.where(lane == 1.0, i2, 0.0)
                         + jnp.where(lane == 2.0, 1.0 / den, 0.0) + jnp.where(lane == 3.0, e2 / den, 0.0))


def _outproj(x, o_nsa, o_hg, w, g, router, tm):
    t, d = x.shape
    moe = router is not None
    const = lambda shape: pl.BlockSpec(shape, lambda i: (0,) * len(shape))
    row = lambda n: pl.BlockSpec((tm, n), lambda i: (i, 0))
    in_specs = [row(d), row(NSA_WIDTH), row(HGRN_WIDTH), const(w.shape), const((1, d))]
    args = [x, o_nsa, o_hg, w, g]
    out_specs = [row(d), row(d)]
    out_shape = [jax.ShapeDtypeStruct((t, d), F32), jax.ShapeDtypeStruct((t, d), BF16)]
    if moe:
        in_specs.append(const(router.shape))
        args.append(router)
        out_specs.append(row(LANES))
        out_shape.append(jax.ShapeDtypeStruct((t, LANES), F32))
    return pl.pallas_call(
        functools.partial(_outproj_kernel, moe=moe),
        grid=(t // tm,),
        in_specs=in_specs, out_specs=out_specs, out_shape=out_shape,
        compiler_params=_params(("arbitrary",)),
        name="outproj_moe" if moe else "outproj",
    )(*args)


def _ffn_kernel(x_ref, h_ref, wg_ref, wu_ref, wd_ref, o_ref):
    f = pl.program_id(1)
    h = h_ref[...]
    a = _dot(h, wg_ref[...])
    b = _dot(h, wu_ref[...])
    d = _dot((a * _sigmoid(a) * b).astype(BF16), wd_ref[...])

    @pl.when(f == 0)
    def _():
        o_ref[...] = x_ref[...] + d

    @pl.when(f > 0)
    def _():
        o_ref[...] = o_ref[...] + d


def _ffn(x, h, wg, wu, wd, tm, tf):
    t, d = x.shape
    ff = wg.shape[1]
    return pl.pallas_call(
        _ffn_kernel,
        grid=(t // tm, ff // tf),
        in_specs=[pl.BlockSpec((tm, d), lambda i, f: (i, 0)), pl.BlockSpec((tm, d), lambda i, f: (i, 0)),
                  pl.BlockSpec((d, tf), lambda i, f: (0, f)), pl.BlockSpec((d, tf), lambda i, f: (0, f)),
                  pl.BlockSpec((tf, d), lambda i, f: (f, 0))],
        out_specs=pl.BlockSpec((tm, d), lambda i, f: (i, 0)),
        out_shape=jax.ShapeDtypeStruct((t, d), F32),
        compiler_params=_params(("arbitrary", "arbitrary")),
        name="ffn",
    )(x, h, wg, wu, wd)


def _route_kernel(rt_ref, pos_ref, post_ref, cnt_ref, carry_ref, *, ts):
    s = pl.program_id(0)

    @pl.when(s == 0)
    def _():
        carry_ref[...] = jnp.zeros(carry_ref.shape, F32)

    rt = rt_ref[...]
    lane = lax.broadcasted_iota(jnp.int32, (ts, LANES), 1).astype(F32)
    c = jnp.where((lane == rt[:, 0:1]) | (lane == rt[:, 1:2]), 1.0, 0.0)
    lt = (lax.broadcasted_iota(jnp.int32, (ts, ts), 1) < lax.broadcasted_iota(jnp.int32, (ts, ts), 0)).astype(BF16)
    rank = _dot(lt, c.astype(BF16)) + carry_ref[...]
    pos = jnp.where(c > 0.0, rank, -1.0)
    pos_ref[...] = pos
    post_ref[...] = pos.T[0:8, :]
    n = jnp.sum(c, axis=0, keepdims=True)
    cnt_ref[...] = jnp.broadcast_to(n, cnt_ref.shape)
    carry_ref[...] = carry_ref[...] + n


def _route(rt, ts):
    t = rt.shape[0]
    ns = t // ts
    return pl.pallas_call(
        functools.partial(_route_kernel, ts=ts),
        grid=(ns,),
        in_specs=[pl.BlockSpec((ts, LANES), lambda s: (s, 0))],
        out_specs=[pl.BlockSpec((ts, LANES), lambda s: (s, 0)), pl.BlockSpec((8, ts), lambda s: (0, s)),
                   pl.BlockSpec((None, 8, LANES), lambda s: (s, 0, 0))],
        out_shape=[jax.ShapeDtypeStruct((t, LANES), F32), jax.ShapeDtypeStruct((8, t), F32),
                   jax.ShapeDtypeStruct((ns, 8, LANES), F32)],
        scratch_shapes=[pltpu.VMEM((1, LANES), F32)],
        compiler_params=_params(("arbitrary",)),
        name="moe_route",
    )(rt)


def _moe_gather_kernel(texp_ref, trank_ref, slo_ref, nsrc_ref, post_ref, h_ref, o_ref, hbuf_ref, sem_ref, acc_ref, *, tm, ts):
    j = pl.program_id(0)
    e = texp_ref[j]
    r0 = trank_ref[j]
    s0 = slo_ref[j]
    n = nsrc_ref[j]
    acc_ref[...] = jnp.zeros(acc_ref.shape, F32)
    row = lax.broadcasted_iota(jnp.int32, (tm, 1), 0).astype(F32) + r0.astype(F32)

    def copy(s, slot):
        return pltpu.make_async_copy(h_ref.at[pl.ds(pl.multiple_of(s * ts, ts), ts), :], hbuf_ref.at[slot], sem_ref.at[slot])

    for i in range(GATHER_BUFS - 1):
        @pl.when(n > i)
        def _():
            copy(s0 + i, i).start()

    def body(k, carry):
        slot = k % GATHER_BUFS

        @pl.when(k + GATHER_BUFS - 1 < n)
        def _():
            copy(s0 + k + GATHER_BUFS - 1, (k + GATHER_BUFS - 1) % GATHER_BUFS).start()
        copy(s0 + k, slot).wait()
        prow = post_ref[pl.ds(e, 1), pl.ds(pl.multiple_of((s0 + k) * ts, ts), ts)]
        onehot = jnp.where(prow == row, 1.0, 0.0).astype(BF16)
        acc_ref[...] = acc_ref[...] + _dot(onehot, hbuf_ref[slot])
        return carry
    lax.fori_loop(0, n, body, 0)
    o_ref[...] = acc_ref[...].astype(o_ref.dtype)


def _moe_gather(tables, post, h, *, n_tiles, tm, ts):
    t, d = h.shape
    return pl.pallas_call(
        functools.partial(_moe_gather_kernel, tm=tm, ts=ts),
        grid_spec=pltpu.PrefetchScalarGridSpec(
            num_scalar_prefetch=4,
            grid=(n_tiles,),
            in_specs=[pl.BlockSpec(post.shape, lambda j, *_: (0, 0)), pl.BlockSpec(memory_space=pl.ANY)],
            out_specs=pl.BlockSpec((tm, d), lambda j, *_: (j, 0)),
            scratch_shapes=[pltpu.VMEM((GATHER_BUFS, ts, d), BF16), pltpu.SemaphoreType.DMA((GATHER_BUFS,)), pltpu.VMEM((tm, d), F32)],
        ),
        out_shape=jax.ShapeDtypeStruct((n_tiles * tm, d), BF16),
        compiler_params=_params(("arbitrary",)),
        name="moe_gather",
    )(*tables, post, h)


def _moe_ffn_kernel(texp_ref, tused_ref, h_ref, wg_ref, wu_ref, wd_ref, o_ref, acc_ref):
    f = pl.program_id(1)
    last = pl.num_programs(1) - 1
    used = tused_ref[pl.program_id(0)] > 0

    @pl.when(used)
    def _():
        h = h_ref[...]
        a = _dot(h, wg_ref[...])
        b = _dot(h, wu_ref[...])
        d = _dot((a * _sigmoid(a) * b).astype(BF16), wd_ref[...])

        @pl.when(f == 0)
        def _():
            acc_ref[...] = d

        @pl.when(f > 0)
        def _():
            acc_ref[...] = acc_ref[...] + d

        @pl.when(f == last)
        def _():
            o_ref[...] = acc_ref[...].astype(o_ref.dtype)

    @pl.when(jnp.logical_not(used) & (f == last))
    def _():
        o_ref[...] = jnp.zeros(o_ref.shape, o_ref.dtype)


def _moe_ffn(texp, tused, hs, wg, wu, wd, *, tm, tf):
    rows, d = hs.shape
    ff = wg.shape[2]
    return pl.pallas_call(
        _moe_ffn_kernel,
        grid_spec=pltpu.PrefetchScalarGridSpec(
            num_scalar_prefetch=2,
            grid=(rows // tm, ff // tf),
            in_specs=[pl.BlockSpec((tm, d), lambda j, f, te, tu: (j, 0)),
                      pl.BlockSpec((None, d, tf), lambda j, f, te, tu: (te[j], 0, f)),
                      pl.BlockSpec((None, d, tf), lambda j, f, te, tu: (te[j], 0, f)),
                      pl.BlockSpec((None, tf, d), lambda j, f, te, tu: (te[j], f, 0))],
            out_specs=pl.BlockSpec((tm, d), lambda j, f, te, tu: (j, 0)),
            scratch_shapes=[pltpu.VMEM((tm, d), F32)],
        ),
        out_shape=jax.ShapeDtypeStruct((rows, d), BF16),
        compiler_params=_params(("arbitrary", "arbitrary")),
        name="moe_ffn",
    )(texp, tused, hs, wg, wu, wd)


def _moe_combine_kernel(start_ref, delta_ref, x_ref, rt_ref, pos_ref, ys_ref, o_ref, ybuf_ref, sem_ref, *, ts, win):
    s = pl.program_id(0)
    ns = pl.num_programs(0)
    slot = s % 2

    def copy(step, e, sl):
        st = pl.multiple_of(start_ref[step * N_EXPERTS + e], LANES)
        return pltpu.make_async_copy(ys_ref.at[pl.ds(st, win), :], ybuf_ref.at[sl, e], sem_ref.at[sl])

    def start_all(step, sl):
        for e in range(N_EXPERTS):
            copy(step, e, sl).start()

    @pl.when(s == 0)
    def _():
        start_all(s, slot)

    @pl.when(s + 1 < ns)
    def _():
        start_all(s + 1, 1 - slot)

    rt = rt_ref[...]
    pos = pos_ref[...]
    col = lax.broadcasted_iota(jnp.int32, (1, win), 1).astype(F32)
    for e in range(N_EXPERTS):
        copy(s, e, slot).wait()
    y = x_ref[...]
    for e in range(N_EXPERTS):
        rel = pos[:, e:e + 1] + delta_ref[s * N_EXPERTS + e].astype(F32)
        g = jnp.where(rel == col, 1.0, 0.0).astype(BF16)
        w = jnp.where(rt[:, 0:1] == float(e), rt[:, 2:3], 0.0) + jnp.where(rt[:, 1:2] == float(e), rt[:, 3:4], 0.0)
        y = y + w * _dot(g, ybuf_ref[slot, e])
    o_ref[...] = y


def _moe_combine(start, delta, x, rt, pos, ys, *, ts, win):
    t, d = x.shape
    row = lambda n: pl.BlockSpec((ts, n), lambda s, *_: (s, 0))
    return pl.pallas_call(
        functools.partial(_moe_combine_kernel, ts=ts, win=win),
        grid_spec=pltpu.PrefetchScalarGridSpec(
            num_scalar_prefetch=2,
            grid=(t // ts,),
            in_specs=[row(d), row(LANES), row(LANES), pl.BlockSpec(memory_space=pl.ANY)],
            out_specs=row(d),
            scratch_shapes=[pltpu.VMEM((2, N_EXPERTS, win, d), BF16), pltpu.SemaphoreType.DMA((2,))],
        ),
        out_shape=jax.ShapeDtypeStruct((t, d), F32),
        compiler_params=_params(("arbitrary",)),
        name="moe_combine",
    )(start, delta, x, rt, pos, ys)


def _moe(x, h, rt, wg, wu, wd, *, tm, ts, tf):
    t, d = x.shape
    win = ts + LANES
    n_tiles = -(-2 * t // tm) + N_EXPERTS + 1
    pos, post, cnt = _route(rt, ts)
    counts = cnt[:, 0, :N_EXPERTS].astype(jnp.int32)
    cum = jnp.concatenate([jnp.zeros((1, N_EXPERTS), jnp.int32), jnp.cumsum(counts, axis=0)], axis=0)
    tiles_e = -(-cum[-1] // tm)
    tstart = jnp.concatenate([jnp.zeros((1,), jnp.int32), jnp.cumsum(tiles_e)])
    jt = jnp.arange(n_tiles, dtype=jnp.int32)
    texp = jnp.minimum(jnp.sum(jt[:, None] >= tstart[None, 1:], axis=1), N_EXPERTS - 1).astype(jnp.int32)
    used = jt < tstart[-1]
    trank = jnp.where(used, (jt - tstart[texp]) * tm, -2 * tm).astype(jnp.int32)
    lo = cum[:-1].T[texp]
    hi = cum[1:].T[texp]
    hit = used[:, None] & (lo < trank[:, None] + tm) & (hi > trank[:, None])
    slo = jnp.sum(used[:, None] & (hi <= trank[:, None]), axis=1).astype(jnp.int32)
    nsrc = jnp.sum(hit, axis=1).astype(jnp.int32)
    row0 = tstart[:-1][None, :] * tm + cum[:-1]
    start = (row0 // LANES) * LANES
    delta = tstart[:-1][None, :] * tm - start
    hs = _moe_gather((texp, trank, slo, nsrc), post, h, n_tiles=n_tiles, tm=tm, ts=ts)
    ys = _moe_ffn(texp, used.astype(jnp.int32), hs, wg, wu, wd, tm=tm, tf=tf)
    return _moe_combine(start.reshape(-1).astype(jnp.int32), delta.reshape(-1).astype(jnp.int32), x, rt, pos, ys, ts=ts, win=win)


def _head_perm():
    idx = []
    for c in range(4):
        idx += list(range(HEAD_DIM * c, HEAD_DIM * (c + 1))) + list(range(HEAD_DIM * (4 + c), HEAD_DIM * (5 + c)))
    return np.asarray(idx, np.int32)


def _tap_matrix(n_cmp, nc_pad, n_blk, nb_pad):
    r_s = SEL_LEN // CMP_STRIDE
    r_c = CMP_LEN // CMP_STRIDE
    taps = np.convolve(np.ones(r_s), np.ones(r_c)) / r_c
    tm = np.zeros((nc_pad, nb_pad), np.float32)
    for j in range(n_blk):
        for kk, w in enumerate(taps):
            n = j * r_s + kk - (r_c - 1)
            if 0 <= n < n_cmp:
                tm[n, j] = w
    return tm


def _layer_weights(l, w_in, q_gain, k_gain, cmp_pe, cmp_w, w_out):
    perm = _head_perm()
    wl = w_in[l]
    q_end = NSA_WIDTH
    kv_end = q_end + 6 * NSA_GROUPS * HEAD_DIM
    gate_end = kv_end + 3 * NSA_HEADS
    d = wl.shape[0]
    w_pad = jnp.concatenate([wl[:, :q_end][:, perm], wl[:, q_end:kv_end], wl[:, kv_end:gate_end],
                             jnp.zeros((d, LANES - 3 * NSA_HEADS), wl.dtype), wl[:, gate_end:]], axis=1).astype(BF16)
    qg = jnp.tile(q_gain[l], NSA_HEADS)[None, :]
    kg_proj = jnp.stack([jnp.tile(k_gain[l, 1], NSA_GROUPS), jnp.tile(k_gain[l, 2], NSA_GROUPS)])
    kg_cmp = jnp.tile(k_gain[l, 0], NSA_GROUPS)[None, :]
    pairs = CMP_STRIDE // 2
    cw = cmp_w[l].reshape(2, 2, pairs, 2, HEAD_DIM, HEAD_DIM)
    eye = jnp.eye(NSA_GROUPS, dtype=cw.dtype)
    w_cmp = jnp.einsum('krjsde,gh->kjsgdrhe', cw, eye).reshape(2 * pairs * 2 * LANES, 2 * LANES).astype(BF16)
    pe = cmp_pe[l].reshape(2, 2, pairs, 2, 1, HEAD_DIM)
    pe = jnp.broadcast_to(pe, (2, 2, pairs, 2, NSA_GROUPS, HEAD_DIM)).reshape(4 * pairs, 2 * LANES)
    wo = w_out[l]
    wo = jnp.concatenate([wo[:NSA_WIDTH][perm], wo[NSA_WIDTH:]], axis=0).astype(BF16)
    return w_pad, qg, kg_proj, kg_cmp, w_cmp, pe, wo


def _token_minor(a):
    n = a.ndim
    return jnp.transpose(a, tuple(range(n - 4)) + (n - 3, n - 2, n - 1, n - 4))


def _token_major(a):
    n = a.ndim
    return jnp.transpose(a, tuple(range(n - 4)) + (n - 1, n - 4, n - 3, n - 2))


def kernel(x_prompt, x_sample, cache_kv, state_win_kv, state_hgrn, page_table, norm_mix, norm_ffn, w_in, q_gain, k_gain, cmp_pe, cmp_w, hgrn_lb_logits, hgrn_o_gain, w_out, ffn_w_gate, ffn_w_up, ffn_w_down, moe_router, moe_w_gate, moe_w_up, moe_w_down):
    depth = w_in.shape[0]
    batch, seq, d_model = x_prompt.shape
    db, ds, _ = x_sample.shape
    n_pool = cache_kv.shape[1]
    n_pages = page_table.shape[1]
    past = n_pages * PAGE_SIZE
    wb = state_win_kv.shape[2]
    assert wb == WINDOW and seq % 256 == 0 and seq >= WINDOW and ds == 8 and n_pages % 16 == 0

    tq, nkt_p, tc = 128, 512, 128
    tm_p = 256
    tm_f = 512 if (batch * seq) % 512 == 0 else 256
    tm_e = 512
    ts = db * ds
    d_ff = ffn_w_gate.shape[-1]
    tf = d_ff // 2

    sm = jax.nn.softmax(hgrn_lb_logits.astype(F32), axis=0)
    lower = jnp.concatenate([jnp.zeros_like(sm[:1]), jnp.cumsum(sm[1:], axis=0)], axis=0)
    seg = jnp.asarray(np.kron(np.eye(NSA_HEADS), np.ones((HEAD_DIM, HEAD_DIM))), BF16)
    seg1 = seg[:LANES, :LANES]

    nc_p = seq // CMP_STRIDE
    nb_p = -(-seq // SEL_LEN)
    tt_p = jnp.asarray(_tap_matrix(nc_p - 1, nc_p, nb_p, -(-nb_p // 8) * 8).T, BF16)
    nc_s = past // CMP_STRIDE
    nb_s = -(-(past + ds) // SEL_LEN)
    nbp_s = -(-nb_s // LANES) * LANES
    tm_s = jnp.asarray(_tap_matrix(nc_s - 1, nc_s, nb_s, nbp_s), BF16)

    cache_t = _token_minor(cache_kv).reshape(depth * n_pool * KV_COLS, PAGE_SIZE)
    wstate_t = _token_minor(state_win_kv).reshape(depth * db * 2 * LANES, wb)
    hstate = state_hgrn.reshape(depth * db, HGRN_HEADS, HGRN_DK, HGRN_DV)
    zero_state = jnp.zeros((batch, HGRN_HEADS, HGRN_DK, HGRN_DV), F32)

    xp = x_prompt.reshape(batch * seq, d_model)
    xs = x_sample.reshape(ts, d_model)
    kv_p, kv_s, win_p, win_s, hs_p, hs_s = [], [], [], [], [], []
    for l in range(depth):
        w_pad, qg, kg_proj, kg_cmp, w_cmp, pe, wo = _layer_weights(l, w_in, q_gain, k_gain, cmp_pe, cmp_w, w_out)
        lb = lower[l]
        hp = jnp.stack([jnp.log(lb), jnp.log1p(-lb), 1.0 - lb])
        og = hgrn_o_gain[l][None, :]
        g1 = norm_mix[l][None, :]
        g2 = norm_ffn[l][None, :]
        i = l // 2
        if l % 2 == 0:
            router = None
            wg, wu, wd = ffn_w_gate[i].astype(BF16), ffn_w_up[i].astype(BF16), ffn_w_down[i].astype(BF16)
        else:
            router = jnp.pad(moe_router[i], ((0, 0), (0, LANES - N_EXPERTS)))
            wg, wu, wd = moe_w_gate[i].astype(BF16), moe_w_up[i].astype(BF16), moe_w_down[i].astype(BF16)

        def mixer(x, h, rt):
            tm = tm_f if x.shape[0] % tm_f == 0 else x.shape[0]
            if router is None:
                return _ffn(x, h, wg, wu, wd, tm, tf)
            return _moe(x, h, rt, wg, wu, wd, tm=tm_e, ts=min(tm_p, x.shape[0]), tf=tf)

        qt, rows_t, win_t, kb, vt, kc, vc, gt, hq, hlf, hk, hv, hg = _proj_prompt(xp, g1, w_pad, qg, kg_proj, seg, hp, batch=batch, seq=seq, tm=tm_p)
        pp = _compress_prompt(kc, vc, pe, w_cmp, batch=batch, seq=seq)
        o_nsa = _nsa_prompt(qt, kb, vt, pp, gt, kg_cmp, seg1, tt_p, batch=batch, seq=seq, tq=tq, nkt=nkt_p)
        o_hg, s_fin = _hgrn(hq, hlf, hk, hv, hg, og, zero_state, batch=batch, seq=seq, tc=tc, s0_base=0)
        res = _outproj(xp, o_nsa, o_hg, wo, g2, router, tm_p)
        xp = mixer(res[0], res[1], res[2] if router is not None else None)
        kv_p.append(rows_t.reshape(batch, 4, NSA_GROUPS, HEAD_DIM, seq))
        win_p.append(win_t.reshape(batch, 2, NSA_GROUPS, HEAD_DIM, seq)[..., seq - min(WINDOW, seq):])
        hs_p.append(s_fin)

        q, rows, win, gates, hq, hlf, hk, hv, hg = _proj_sample(xs, g1, w_pad, qg, kg_proj, seg, hp)
        ps = _compress_sample(page_table, cache_t, pe, w_cmp, page_base=l * n_pool)
        o_nsa = _nsa_sample(page_table, q, ps, rows, win, wstate_t, gates, kg_cmp, seg1, tm_s, cache_t,
                            tq=ds, pages_per_tile=16, page_base=l * n_pool, wstate_base=l * db)
        padt = lambda a: jnp.pad(a.reshape(db, ds, HGRN_WIDTH), ((0, 0), (0, tc - ds), (0, 0))).reshape(db * tc, HGRN_WIDTH)
        o_hg, s_new = _hgrn(padt(hq), padt(hlf), padt(hk), padt(hv), padt(hg), og, hstate, batch=db, seq=tc, tc=tc, s0_base=l * db)
        o_hg = o_hg.reshape(db, tc, HGRN_WIDTH)[:, :ds].reshape(ts, HGRN_WIDTH)
        res = _outproj(xs, o_nsa.astype(BF16), o_hg, wo, g2, router, ts)
        xs = mixer(res[0], res[1], res[2] if router is not None else None)
        kv_s.append(rows.reshape(db, ds, 4, NSA_GROUPS, HEAD_DIM))
        win_s.append(win.reshape(db, ds, 2, NSA_GROUPS, HEAD_DIM))
        hs_s.append(s_new)

    new_win_sample = jnp.concatenate([state_win_kv[:, :, ds:], jnp.stack(win_s)], axis=2)
    return (xp.reshape(batch, seq, d_model), xs.reshape(db, ds, d_model), _token_major(jnp.stack(kv_p)), jnp.stack(kv_s),
            _token_major(jnp.stack(win_p)), new_win_sample, jnp.stack(hs_p), jnp.stack(hs_s))
```
